```python
import math
import jax, jax.numpy as jnp
from jax import lax
import numpy as np

D_MODEL = 1024
BATCH = 16
SEQ = 4096
DEPTH = 1

HEAD_DIM = 64
ROPE_THETA = 500000.0
NORM_EPS = 1e-6
Q_BLOCK = 128

MLA_HEADS = 8
MLA_Q_RANK = 256
MLA_KV_RANK = 128
MLA_NOPE_DIM = 64
MLA_ROPE_DIM = 32
MLA_V_DIM = 64
MLA_WIDTH = MLA_HEADS * MLA_V_DIM

NSA_HEADS = 8
NSA_KV_GROUPS = 2
NSA_HPG = NSA_HEADS // NSA_KV_GROUPS
NSA_ROPE_DIM = HEAD_DIM // 4
NSA_WIDTH = NSA_HEADS * HEAD_DIM
NSA_KV_WIDTH = NSA_KV_GROUPS * HEAD_DIM
CMP_BLOCK = 32
CMP_STRIDE = 16
CMP_HIDDEN = 2 * HEAD_DIM
SLC_BLOCK = 64
SLC_TOPK = 16
WINDOW = 512
FORCE_SCORE = 1e9

MIX_WIDTH = MLA_WIDTH + NSA_WIDTH
D_FF = 4 * D_MODEL

IN_SIZES = (MLA_Q_RANK, MLA_KV_RANK, MLA_ROPE_DIM, NSA_WIDTH, 6 * NSA_KV_WIDTH, 3 * NSA_HEADS)
IN_COLS = MLA_Q_RANK + MLA_KV_RANK + MLA_ROPE_DIM + NSA_WIDTH + 6 * NSA_KV_WIDTH + 3 * NSA_HEADS

kernel_name = 'hymba_mla_nsa_hybrid_layer'


def rmsnorm(x, g):
    xf = x.astype(jnp.float32)
    y = xf * lax.rsqrt(jnp.mean(xf * xf, axis=-1, keepdims=True) + NORM_EPS)
    return (y * g.astype(jnp.float32)).astype(x.dtype)


def rope_tables(pos, dim):
    inv_freq = jnp.exp(-math.log(ROPE_THETA) * jnp.arange(0, dim, 2, dtype=jnp.float32) / dim)
    ang = pos.astype(jnp.float32)[:, None] * inv_freq[None, :]
    return jnp.cos(ang), jnp.sin(ang)


def apply_rope(x, cos, sin):
    half = x.shape[-1] // 2
    xf = x.astype(jnp.float32)
    x1, x2 = xf[..., :half], xf[..., half:]
    c = cos[None, :, None, :]
    s = sin[None, :, None, :]
    return jnp.concatenate([x1 * c - x2 * s, x2 * c + x1 * s], axis=-1).astype(x.dtype)


def partial_rope(x, cos, sin):
    return jnp.concatenate([apply_rope(x[..., :NSA_ROPE_DIM], cos, sin), x[..., NSA_ROPE_DIM:]], axis=-1)


def masked_softmax(s, mask):
    s = jnp.where(mask, s, -jnp.inf)
    m = jnp.max(s, axis=-1, keepdims=True)
    m = jnp.where(jnp.isfinite(m), m, 0.0)
    p = jnp.exp(s - m)
    return p / jnp.maximum(jnp.sum(p, axis=-1, keepdims=True), 1e-30)


def to_blocks(a):
    b, s = a.shape[0], a.shape[1]
    return a.reshape(b, s // Q_BLOCK, Q_BLOCK, *a.shape[2:]).swapaxes(0, 1)


def from_blocks(a):
    nb, b, qb = a.shape[0], a.shape[1], a.shape[2]
    return a.swapaxes(0, 1).reshape(b, nb * qb, *a.shape[3:])


def mla_mixer(c_q, c_kv, k_rope, g_cq, w_uq, g_ckv, w_ukv, cos, sin):
    B, S, _ = c_q.shape
    q = (rmsnorm(c_q, g_cq) @ w_uq).reshape(B, S, MLA_HEADS, MLA_NOPE_DIM + MLA_ROPE_DIM)
    q_nope = q[..., :MLA_NOPE_DIM]
    q_pe = apply_rope(q[..., MLA_NOPE_DIM:], cos, sin)
    kv = (rmsnorm(c_kv, g_ckv) @ w_ukv).reshape(B, S, MLA_HEADS, MLA_NOPE_DIM + MLA_V_DIM)
    k_nope = kv[..., :MLA_NOPE_DIM]
    v = kv[..., MLA_NOPE_DIM:]
    k_pe = apply_rope(k_rope[:, :, None, :], cos, sin)[:, :, 0]
    scale = (MLA_NOPE_DIM + MLA_ROPE_DIM) ** -0.5
    kpos = jnp.arange(S)

    def attend(blk):
        qn, qp, qpos = blk
        s = (jnp.einsum('bqhd,bkhd->bhqk', qn, k_nope, preferred_element_type=jnp.float32)
             + jnp.einsum('bqhr,bkr->bhqk', qp, k_pe, preferred_element_type=jnp.float32)) * scale
        p = masked_softmax(s, kpos[None, :] <= qpos[:, None])
        return jnp.einsum('bhqk,bkhd->bqhd', p.astype(v.dtype), v)

    out = lax.map(attend, (to_blocks(q_nope), to_blocks(q_pe), kpos.reshape(-1, Q_BLOCK)))
    return from_blocks(out).reshape(B, S, MLA_WIDTH)


def compress_blocks(x, pe, w1, b1, w2, b2):
    B, S, G, dh = x.shape
    n_cmp = (S - CMP_BLOCK) // CMP_STRIDE + 1
    idx = jnp.arange(n_cmp)[:, None] * CMP_STRIDE + jnp.arange(CMP_BLOCK)[None, :]
    blocks = x[:, idx] + pe[None, None, :, None, :].astype(x.dtype)
    flat = blocks.transpose(0, 1, 3, 2, 4).reshape(B, n_cmp, G, CMP_BLOCK * dh)
    hid = jax.nn.gelu(flat @ w1 + b1)
    return hid @ w2 + b2


def nsa_mixer(q, kv_tok, gate_logits, cmp_pe_k, cmp_w1_k, cmp_b1_k, cmp_w2_k, cmp_b2_k,
              cmp_pe_v, cmp_w1_v, cmp_b1_v, cmp_w2_v, cmp_b2_v, cos, sin):
    B, S, H, dh = q.shape
    G = NSA_KV_GROUPS
    dt = q.dtype
    scale = dh ** -0.5
    q = partial_rope(q, cos, sin)
    k_slc = partial_rope(kv_tok[:, :, 2], cos, sin)
    v_slc = kv_tok[:, :, 3]
    k_win = partial_rope(kv_tok[:, :, 4], cos, sin)
    v_win = kv_tok[:, :, 5]

    n_cmp = (S - CMP_BLOCK) // CMP_STRIDE + 1
    cmp_end = jnp.arange(n_cmp) * CMP_STRIDE + CMP_BLOCK - 1
    k_cmp = compress_blocks(kv_tok[:, :, 0], cmp_pe_k, cmp_w1_k, cmp_b1_k, cmp_w2_k, cmp_b2_k)
    k_cmp = partial_rope(k_cmp, cos[cmp_end], sin[cmp_end])
    v_cmp = compress_blocks(kv_tok[:, :, 1], cmp_pe_v, cmp_w1_v, cmp_b1_v, cmp_w2_v, cmp_b2_v)

    n_slc = S // SLC_BLOCK
    k_top = min(SLC_TOPK, n_slc)
    cs = jnp.arange(n_cmp)[:, None] * CMP_STRIDE
    ss = jnp.arange(n_slc)[None, :] * SLC_BLOCK
    overlap = jnp.clip(jnp.minimum(cs + CMP_BLOCK, ss + SLC_BLOCK) - jnp.maximum(cs, ss), 0, None).astype(jnp.float32) / CMP_BLOCK
    k_slc_blocks = k_slc.reshape(B, n_slc, SLC_BLOCK, G, dh).transpose(0, 3, 1, 2, 4)
    v_slc_blocks = v_slc.reshape(B, n_slc, SLC_BLOCK, G, dh).transpose(0, 3, 1, 2, 4)
    gather = jax.vmap(jax.vmap(lambda blocks, ix: blocks[ix]))

    pad = ((0, 0), (WINDOW, 0), (0, 0), (0, 0))
    k_win_pad = jnp.pad(k_win, pad)
    v_win_pad = jnp.pad(v_win, pad)

    def attend(blk):
        qb, gb, bi = blk
        q0 = bi * Q_BLOCK
        qpos = q0 + jnp.arange(Q_BLOCK)
        qg = qb.reshape(B, Q_BLOCK, G, NSA_HPG, dh)

        s_c = jnp.einsum('bqghd,bngd->bghqn', qg, k_cmp, preferred_element_type=jnp.float32) * scale
        p_c = masked_softmax(s_c, cmp_end[None, :] <= qpos[:, None])
        o_c = jnp.einsum('bghqn,bngd->bqghd', p_c.astype(dt), v_cmp)

        imp = jnp.einsum('bghqn,nj->bgqj', p_c, overlap)
        j = jnp.arange(n_slc)[None, :]
        cur = (qpos // SLC_BLOCK)[:, None]
        forced = (j == 0) | (j == cur) | (j == cur - 1)
        causal = j * SLC_BLOCK <= qpos[:, None]
        score = jnp.where(forced, FORCE_SCORE, jnp.where(causal, imp, -FORCE_SCORE))
        _, sel = lax.top_k(score, k_top)
        kg = gather(k_slc_blocks, sel).reshape(B, G, Q_BLOCK, k_top * SLC_BLOCK, dh)
        vg = gather(v_slc_blocks, sel).reshape(B, G, Q_BLOCK, k_top * SLC_BLOCK, dh)
        tok = (sel[..., None] * SLC_BLOCK + jnp.arange(SLC_BLOCK)).reshape(B, G, Q_BLOCK, k_top * SLC_BLOCK)
        s_s = jnp.einsum('bqghd,bgqmd->bghqm', qg, kg, preferred_element_type=jnp.float32) * scale
        p_s = masked_softmax(s_s, (tok <= qpos[None, None, :, None])[:, :, None])
        o_s = jnp.einsum('bghqm,bgqmd->bqghd', p_s.astype(dt), vg)

        kw = lax.dynamic_slice_in_dim(k_win_pad, q0, Q_BLOCK + WINDOW, axis=1)
        vw = lax.dynamic_slice_in_dim(v_win_pad, q0, Q_BLOCK + WINDOW, axis=1)
        kpos = q0 - WINDOW + jnp.arange(Q_BLOCK + WINDOW)
        dist = qpos[:, None] - kpos[None, :]
        mask_w = (dist >= 0) & (dist < WINDOW) & (kpos[None, :] >= 0)
        s_w = jnp.einsum('bqghd,bkgd->bghqk', qg, kw, preferred_element_type=jnp.float32) * scale
        p_w = masked_softmax(s_w, mask_w)
        o_w = jnp.einsum('bghqk,bkgd->bqghd', p_w.astype(dt), vw)

        g = jax.nn.sigmoid(gb.astype(jnp.float32)).reshape(B, Q_BLOCK, 3, G, NSA_HPG)[..., None]
        o = g[:, :, 0] * o_c + g[:, :, 1] * o_s + g[:, :, 2] * o_w
        return o.reshape(B, Q_BLOCK, H * dh).astype(dt)

    nb = S // Q_BLOCK
    out = lax.map(attend, (to_blocks(q), to_blocks(gate_logits), jnp.arange(nb)))
    return from_blocks(out)


def setup_inputs(seed: int = 0) -> dict:
    key = jax.random.key(seed)
    keys = jax.random.split(key, 32)
    f32 = jnp.float32
    L = DEPTH
    flat = CMP_BLOCK * HEAD_DIM

    def w(i, shape, fan_in):
        return jax.random.normal(keys[i], shape, f32) * fan_in ** -0.5

    def gain(i, shape):
        return 1.0 + 0.02 * jax.random.normal(keys[i], shape, f32)

    def small(i, shape, s=0.01):
        return s * jax.random.normal(keys[i], shape, f32)

    return {
        'x': jax.random.normal(keys[0], (BATCH, SEQ, D_MODEL), f32),
        'g_mix_norm': gain(1, (L, D_MODEL)),
        'w_in': w(2, (L, D_MODEL, IN_COLS), D_MODEL),
        'g_cq': gain(3, (L, MLA_Q_RANK)),
        'w_uq': w(4, (L, MLA_Q_RANK, MLA_HEADS * (MLA_NOPE_DIM + MLA_ROPE_DIM)), MLA_Q_RANK),
        'g_ckv': gain(5, (L, MLA_KV_RANK)),
        'w_ukv': w(6, (L, MLA_KV_RANK, MLA_HEADS * (MLA_NOPE_DIM + MLA_V_DIM)), MLA_KV_RANK),
        'cmp_pe_k': small(7, (L, CMP_BLOCK, HEAD_DIM), 0.1),
        'cmp_w1_k': w(8, (L, flat, CMP_HIDDEN), flat),
        'cmp_b1_k': small(9, (L, CMP_HIDDEN)),
        'cmp_w2_k': w(10, (L, CMP_HIDDEN, HEAD_DIM), CMP_HIDDEN),
        'cmp_b2_k': small(11, (L, HEAD_DIM)),
        'cmp_pe_v': small(12, (L, CMP_BLOCK, HEAD_DIM), 0.1),
        'cmp_w1_v': w(13, (L, flat, CMP_HIDDEN), flat),
        'cmp_b1_v': small(14, (L, CMP_HIDDEN)),
        'cmp_w2_v': w(15, (L, CMP_HIDDEN, HEAD_DIM), CMP_HIDDEN),
        'cmp_b2_v': small(16, (L, HEAD_DIM)),
        'g_out_mla': gain(17, (L, MLA_WIDTH)),
        'g_out_nsa': gain(18, (L, NSA_WIDTH)),
        'w_o': w(19, (L, MIX_WIDTH, D_MODEL), MIX_WIDTH),
        'g_mlp_norm': gain(20, (L, D_MODEL)),
        'w_up': w(21, (L, D_MODEL, D_FF), D_MODEL),
        'w_down': w(22, (L, D_FF, D_MODEL), D_FF),
        'g_final': gain(23, (D_MODEL,)),
    }


def reference(x, g_mix_norm, w_in, g_cq, w_uq, g_ckv, w_ukv,
              cmp_pe_k, cmp_w1_k, cmp_b1_k, cmp_w2_k, cmp_b2_k,
              cmp_pe_v, cmp_w1_v, cmp_b1_v, cmp_w2_v, cmp_b2_v,
              g_out_mla, g_out_nsa, w_o, g_mlp_norm, w_up, w_down, g_final):
    B, S, _ = x.shape
    pos = jnp.arange(S)
    cos_m, sin_m = rope_tables(pos, MLA_ROPE_DIM)
    cos_n, sin_n = rope_tables(pos, NSA_ROPE_DIM)
    split_points = [int(v) for v in np.cumsum(IN_SIZES)[:-1]]
    h = x
    for l in range(DEPTH):
        u = rmsnorm(h, g_mix_norm[l]) @ w_in[l]
        c_q, c_kv, k_rope, q_nsa, kv_nsa, gate_nsa = jnp.split(u, split_points, axis=-1)
        y_mla = mla_mixer(c_q, c_kv, k_rope, g_cq[l], w_uq[l], g_ckv[l], w_ukv[l], cos_m, sin_m)
        y_nsa = nsa_mixer(q_nsa.reshape(B, S, NSA_HEADS, HEAD_DIM),
                          kv_nsa.reshape(B, S, 6, NSA_KV_GROUPS, HEAD_DIM),
                          gate_nsa.reshape(B, S, 3, NSA_HEADS),
                          cmp_pe_k[l], cmp_w1_k[l], cmp_b1_k[l], cmp_w2_k[l], cmp_b2_k[l],
                          cmp_pe_v[l], cmp_w1_v[l], cmp_b1_v[l], cmp_w2_v[l], cmp_b2_v[l],
                          cos_n, sin_n)
        mixed = jnp.concatenate([rmsnorm(y_mla, g_out_mla[l]), rmsnorm(y_nsa, g_out_nsa[l])], axis=-1)
        h = h + mixed @ w_o[l]
        a = jax.nn.relu(rmsnorm(h, g_mlp_norm[l]) @ w_up[l])
        h = h + (a * a) @ w_down[l]
    return rmsnorm(h, g_final)
```

```python
import functools
import math

import jax
import jax.numpy as jnp
import numpy as np
from jax import lax
from jax.experimental import pallas as pl
from jax.experimental.pallas import tpu as pltpu

D_MODEL = 1024
HEAD_DIM = 64
ROPE_THETA = 500000.0
NORM_EPS = 1e-6

MLA_HEADS = 8
MLA_Q_RANK = 256
MLA_KV_RANK = 128
MLA_NOPE_DIM = 64
MLA_ROPE_DIM = 32
MLA_V_DIM = 64
MLA_WIDTH = MLA_HEADS * MLA_V_DIM

NSA_HEADS = 8
NSA_KV_GROUPS = 2
NSA_HPG = NSA_HEADS // NSA_KV_GROUPS
NSA_ROPE_DIM = HEAD_DIM // 4
NSA_WIDTH = NSA_HEADS * HEAD_DIM
NSA_KV_WIDTH = NSA_KV_GROUPS * HEAD_DIM
CMP_BLOCK = 32
CMP_STRIDE = 16
CMP_HIDDEN = 2 * HEAD_DIM
SLC_BLOCK = 64
SLC_TOPK = 16
WINDOW = 512
FORCE_SCORE = 1e9
D_FF = 4 * D_MODEL

LANES = 128
VMEM_LIMIT = 56 * 1024 * 1024

MXU_DTYPE = jnp.bfloat16
PROJ_ROWS = 256
MLA_TQ = 256
NSA_TQ = 128
OUT_ROWS = 256
FF_CHUNK = 1024
MASK_BIAS = -1e9

_U_CQ = 0
_U_CKV = 256
_U_KPE = 384
_U_QN = 512
_U_KCV = 1024
_U_KSLC = 1280
_U_VSLC = 1408
_U_KWIN = 1536
_U_VWIN = 1664
_U_GATE = 1792
_U_COLS = 1920
_N_TABS = 7


def _dot(a, b):
    return jnp.dot(a, b, preferred_element_type=jnp.float32)


def _dot_nt(a, b):
    return lax.dot_general(a, b, (((1,), (1,)), ((), ())), preferred_element_type=jnp.float32)


def _rms(x, g):
    return x * lax.rsqrt(jnp.mean(x * x, axis=-1, keepdims=True) + NORM_EPS) * g


def _lane(shape):
    return lax.broadcasted_iota(jnp.int32, shape, len(shape) - 1)


def _swap_pairs(x, lo_end, width):
    n = x.shape[-1]
    lane = _lane(x.shape) % LANES
    return jnp.where(lane < lo_end, pltpu.roll(x, n - width, axis=1), pltpu.roll(x, width, axis=1))


def _nsa_rope(x, cn, sn):
    half = NSA_ROPE_DIM // 2
    lane = _lane(x.shape) % HEAD_DIM
    sw = jnp.where(lane < half, pltpu.roll(x, LANES - half, axis=1), pltpu.roll(x, half, axis=1))
    return x * cn + sw * sn


def _proj_kernel(x_ref, tab_ref, gmix_ref, win_ref, gcq_ref, wq_ref, gckv_ref, wk_ref, wv_ref,
                 qm_ref, km_ref, vm_ref, qn_ref, ksa_ref, vs4_ref, kwp_ref, vw4_ref, kcv_ref, gate_ref):
    x = x_ref[0]
    n = _rms(x, gmix_ref[...]).astype(MXU_DTYPE)
    u = _dot(n, win_ref[...])

    def tab(t):
        return tab_ref[:, t * LANES:(t + 1) * LANES]

    cq_t, sq_t, ck_t, sk_t, cn_t, sn_t, eneg_t = (tab(t) for t in range(_N_TABS))
    lane = _lane((x.shape[0], LANES))
    low = lane < HEAD_DIM

    cqn = _rms(u[:, _U_CQ:_U_CQ + MLA_Q_RANK], gcq_ref[...]).astype(MXU_DTYPE)
    qm = _dot(cqn, wq_ref[...])
    rope_end = MLA_NOPE_DIM + MLA_ROPE_DIM // 2
    for h in range(MLA_HEADS):
        qh = qm[:, h * LANES:(h + 1) * LANES]
        qsw = _swap_pairs(qh, rope_end, MLA_ROPE_DIM // 2)
        qm_ref[0, :, h * LANES:(h + 1) * LANES] = (qh * cq_t + qsw * sq_t).astype(qm_ref.dtype)

    ckvn = _rms(u[:, _U_CKV:_U_CKV + MLA_KV_RANK], gckv_ref[...]).astype(MXU_DTYPE)
    km = _dot(ckvn, wk_ref[...])
    kpe = u[:, _U_KPE:_U_KPE + LANES]
    kpe = kpe * ck_t + _swap_pairs(kpe, rope_end, MLA_ROPE_DIM // 2) * sk_t
    for h in range(MLA_HEADS):
        km_ref[0, :, h * LANES:(h + 1) * LANES] = (km[:, h * LANES:(h + 1) * LANES] + kpe).astype(km_ref.dtype)
    vm_ref[0] = _dot(ckvn, wv_ref[...]).astype(vm_ref.dtype)

    nsa_scale = HEAD_DIM ** -0.5
    for c in range(NSA_WIDTH // LANES):
        qc = u[:, _U_QN + c * LANES:_U_QN + (c + 1) * LANES]
        qn_ref[0, :, c * LANES:(c + 1) * LANES] = (_nsa_rope(qc, cn_t, sn_t) * nsa_scale).astype(qn_ref.dtype)

    ks = _nsa_rope(u[:, _U_KSLC:_U_KSLC + LANES], cn_t, sn_t)
    ks_r = pltpu.roll(ks, HEAD_DIM, axis=1)
    ksa_ref[0, :, 0:LANES] = jnp.where(low, ks, eneg_t).astype(ksa_ref.dtype)
    ksa_ref[0, :, LANES:2 * LANES] = jnp.where(low, ks_r, eneg_t).astype(ksa_ref.dtype)

    kw = _nsa_rope(u[:, _U_KWIN:_U_KWIN + LANES], cn_t, sn_t)
    kw_r = pltpu.roll(kw, HEAD_DIM, axis=1)
    kwp_ref[0, :, 0:LANES] = jnp.where(low, kw, 0.0).astype(kwp_ref.dtype)
    kwp_ref[0, :, LANES:2 * LANES] = jnp.where(low, kw_r, 0.0).astype(kwp_ref.dtype)

    def four_way(v, ref):
        v_r = pltpu.roll(v, HEAD_DIM, axis=1)
        ref[0, :, 0 * LANES:1 * LANES] = jnp.where(low, v, 0.0).astype(ref.dtype)
        ref[0, :, 1 * LANES:2 * LANES] = jnp.where(low, 0.0, v_r).astype(ref.dtype)
        ref[0, :, 2 * LANES:3 * LANES] = jnp.where(low, v_r, 0.0).astype(ref.dtype)
        ref[0, :, 3 * LANES:4 * LANES] = jnp.where(low, 0.0, v).astype(ref.dtype)

    four_way(u[:, _U_VSLC:_U_VSLC + LANES], vs4_ref)
    four_way(u[:, _U_VWIN:_U_VWIN + LANES], vw4_ref)

    kcv_ref[0] = u[:, _U_KCV:_U_KCV + 2 * LANES]
    gate_ref[0] = u[:, _U_GATE:_U_GATE + LANES]


def _proj_call(x, tabs, gmix, win, gcq, wq, gckv, wk, wv):
    B, S, D = x.shape
    rows = PROJ_ROWS
    ns = S // rows
    tok = lambda w: pl.BlockSpec((1, rows, w), lambda i, b: (b, i, 0))
    full = lambda a: pl.BlockSpec(a.shape, lambda i, b: (0,) * a.ndim)
    bf = MXU_DTYPE
    out_shapes = [
        jax.ShapeDtypeStruct((B, S, MLA_HEADS * LANES), bf),
        jax.ShapeDtypeStruct((B, S, MLA_HEADS * LANES), bf),
        jax.ShapeDtypeStruct((B, S, MLA_HEADS * LANES), bf),
        jax.ShapeDtypeStruct((B, S, NSA_WIDTH), bf),
        jax.ShapeDtypeStruct((B, S, 2 * LANES), bf),
        jax.ShapeDtypeStruct((B, S, 4 * LANES), bf),
        jax.ShapeDtypeStruct((B, S, 2 * LANES), bf),
        jax.ShapeDtypeStruct((B, S, 4 * LANES), bf),
        jax.ShapeDtypeStruct((B, S, 2 * LANES), jnp.float32),
        jax.ShapeDtypeStruct((B, S, LANES), jnp.float32),
    ]
    return pl.pallas_call(
        _proj_kernel,
        grid=(ns, B),
        in_specs=[tok(D), pl.BlockSpec((rows, _N_TABS * LANES), lambda i, b: (i, 0)),
                  full(gmix), full(win), full(gcq), full(wq), full(gckv), full(wk), full(wv)],
        out_specs=[tok(s.shape[-1]) for s in out_shapes],
        out_shape=out_shapes,
        compiler_params=pltpu.CompilerParams(
            dimension_semantics=("arbitrary", "arbitrary"), vmem_limit_bytes=VMEM_LIMIT),
        name="proj",
    )(x, tabs, gmix, win, gcq, wq, gckv, wk, wv)


def _gelu_tanh(x):
    return 0.5 * x * (1.0 + jnp.tanh(math.sqrt(2.0 / math.pi) * (x + 0.044715 * (x * x * x))))


def _compress_kernel(x_ref, pea_ref, peb_ref, wa_ref, wb_ref, b1_ref, w2k_ref, b2k_ref,
                     w2vl_ref, b2vl_ref, w2vh_ref, b2vh_ref, ctab_ref, kc_ref, vc_ref):
    x = x_ref[0]
    nch = x.shape[0]
    a = _dot((x + pea_ref[...]).astype(MXU_DTYPE), wa_ref[...])
    b = _dot((x + peb_ref[...]).astype(MXU_DTYPE), wb_ref[...])
    hid = _gelu_tanh(a + pltpu.roll(b, nch - 1, axis=0) + b1_ref[...]).astype(MXU_DTYPE)
    cc = ctab_ref[:, 0:LANES]
    sc = ctab_ref[:, LANES:2 * LANES]
    for g in range(NSA_KV_GROUPS):
        hk = hid[:, g * LANES:(g + 1) * LANES]
        kc = _dot(hk, w2k_ref[...]) + b2k_ref[...]
        kc_ref[0, :, g * LANES:(g + 1) * LANES] = _nsa_rope(kc, cc, sc).astype(kc_ref.dtype)
        hv = hid[:, (NSA_KV_GROUPS + g) * LANES:(NSA_KV_GROUPS + g + 1) * LANES]
        vc_ref[0, :, (2 * g) * LANES:(2 * g + 1) * LANES] = (
            _dot(hv, w2vl_ref[...]) + b2vl_ref[...]).astype(vc_ref.dtype)
        vc_ref[0, :, (2 * g + 1) * LANES:(2 * g + 2) * LANES] = (
            _dot(hv, w2vh_ref[...]) + b2vh_ref[...]).astype(vc_ref.dtype)


def _compress_call(kcv_chunks, consts, ctab):
    B, nch, width = kcv_chunks.shape
    full = lambda a: pl.BlockSpec(a.shape, lambda b: (0,) * a.ndim)
    return pl.pallas_call(
        _compress_kernel,
        grid=(B,),
        in_specs=[pl.BlockSpec((1, nch, width), lambda b: (b, 0, 0))] + [full(c) for c in consts] + [full(ctab)],
        out_specs=[pl.BlockSpec((1, nch, 2 * LANES), lambda b: (b, 0, 0)),
                   pl.BlockSpec((1, nch, 4 * LANES), lambda b: (b, 0, 0))],
        out_shape=[jax.ShapeDtypeStruct((B, nch, 2 * LANES), MXU_DTYPE),
                   jax.ShapeDtypeStruct((B, nch, 4 * LANES), MXU_DTYPE)],
        compiler_params=pltpu.CompilerParams(
            dimension_semantics=("arbitrary",), vmem_limit_bytes=VMEM_LIMIT),
        name="compress",
    )(kcv_chunks, *consts, ctab)


def _first_tile(s):
    m = jnp.max(s, axis=-1, keepdims=True)
    p = jnp.exp(s - m)
    return m, jnp.sum(p, axis=-1, keepdims=True), p


def _next_tile(s, m, l):
    m_new = jnp.maximum(m, jnp.max(s, axis=-1, keepdims=True))
    alpha = jnp.exp(m - m_new)
    p = jnp.exp(s - m_new)
    return m_new, alpha * l + jnp.sum(p, axis=-1, keepdims=True), p, alpha


def _mla_kernel(q_ref, k_ref, v_ref, o_ref):
    i = pl.program_id(2)
    tq = q_ref.shape[1]
    ql = lax.broadcasted_iota(jnp.int32, (tq, tq), 0)
    kl = lax.broadcasted_iota(jnp.int32, (tq, tq), 1)
    out = None
    for hh in range(2):
        cols = slice(hh * LANES, (hh + 1) * LANES)
        qh = q_ref[0, :, cols]

        def kv(j):
            rows = pl.ds(pl.multiple_of(j * tq, tq), tq)
            return k_ref[0, rows, cols], v_ref[0, rows, cols]

        kt, vt = kv(i)
        s = jnp.where(kl <= ql, _dot_nt(qh, kt), -jnp.inf)
        m, l, p = _first_tile(s)
        acc = _dot(p.astype(MXU_DTYPE), vt)

        def body(j, carry):
            m, l, acc = carry
            kt, vt = kv(j)
            m, l, p, alpha = _next_tile(_dot_nt(qh, kt), m, l)
            return m, l, alpha * acc + _dot(p.astype(MXU_DTYPE), vt)

        m, l, acc = lax.fori_loop(0, i, body, (m, l, acc))
        o = acc * (1.0 / l)
        out = o if out is None else out + o
    o_ref[0] = out


def _mla_call(qm, km, vm):
    B, S, _ = qm.shape
    tq = MLA_TQ
    pairs = MLA_HEADS // 2
    return pl.pallas_call(
        _mla_kernel,
        grid=(B, pairs, S // tq),
        in_specs=[pl.BlockSpec((1, tq, 2 * LANES), lambda b, p, i: (b, i, p)),
                  pl.BlockSpec((1, S, 2 * LANES), lambda b, p, i: (b, 0, p)),
                  pl.BlockSpec((1, S, 2 * LANES), lambda b, p, i: (b, 0, p))],
        out_specs=pl.BlockSpec((1, tq, LANES), lambda b, p, i: (b, i, p)),
        out_shape=jax.ShapeDtypeStruct((B, S, MLA_WIDTH), jnp.float32),
        compiler_params=pltpu.CompilerParams(
            dimension_semantics=("arbitrary", "arbitrary", "arbitrary"), vmem_limit_bytes=VMEM_LIMIT),
        name="mla",
    )(qm, km, vm)


def _nsa_cmp_kernel(q_ref, kc_ref, vc_ref, ovt_ref, oc_ref, qaug_ref, score_ref):
    i = pl.program_id(1)
    tq = q_ref.shape[1]
    nch = kc_ref.shape[1]
    nblk = score_ref.shape[0]
    q0 = i * tq
    qpos_c = q0 + lax.broadcasted_iota(jnp.int32, (tq, nch), 0)
    cmp_end = lax.broadcasted_iota(jnp.int32, (tq, nch), 1) * CMP_STRIDE + (CMP_BLOCK - 1)
    mask_c = cmp_end <= qpos_c
    low = _lane((tq, LANES)) < HEAD_DIM

    blk = lax.broadcasted_iota(jnp.int32, (nblk, tq), 0)
    qpos_r = q0 + lax.broadcasted_iota(jnp.int32, (nblk, tq), 1)
    cur = qpos_r // SLC_BLOCK
    forced = (blk == 0) | (blk == cur) | (blk == cur - 1)
    causal = blk * SLC_BLOCK <= qpos_r

    for g in range(NSA_KV_GROUPS):
        kc = kc_ref[0, :, g * LANES:(g + 1) * LANES]
        psum = jnp.zeros((tq, nch), jnp.float32)
        q_heads = []
        for p in range(NSA_HPG // 2):
            pair = g * (NSA_HPG // 2) + p
            qp = q_ref[0, :, pair * LANES:(pair + 1) * LANES].astype(jnp.float32)
            qr = pltpu.roll(qp, HEAD_DIM, axis=1)
            o_pair = None
            for hh, qsrc in enumerate((qp, qr)):
                qh = jnp.where(low, qsrc, 0.0)
                q_heads.append(qh)
                s = jnp.where(mask_c, _dot_nt(qh.astype(MXU_DTYPE), kc), -jnp.inf)
                m = jnp.max(s, axis=-1, keepdims=True)
                m = jnp.where(m == -jnp.inf, 0.0, m)
                e = jnp.exp(s - m)
                pr = e / jnp.maximum(jnp.sum(e, axis=-1, keepdims=True), 1e-30)
                psum = psum + pr
                o = _dot(pr.astype(MXU_DTYPE), vc_ref[0, :, (2 * g + hh) * LANES:(2 * g + hh + 1) * LANES])
                o_pair = o if o_pair is None else o_pair + o
            oc_ref[0, :, pair * LANES:(pair + 1) * LANES] = o_pair

        p_hi = psum.astype(MXU_DTYPE)
        p_lo = (psum - p_hi.astype(jnp.float32)).astype(MXU_DTYPE)
        imp_t = _dot_nt(ovt_ref[...], p_hi) + _dot_nt(ovt_ref[...], p_lo)
        score = jnp.where(forced, FORCE_SCORE, jnp.where(causal, imp_t[nblk:2 * nblk, :], -FORCE_SCORE))
        score_ref[...] = score

        def count(r, cnt):
            row = score_ref[pl.ds(r, 1), :]
            beats = (row > score) | ((row == score) & (r < blk))
            return cnt + jnp.where(beats, 1.0, 0.0)

        cnt = lax.fori_loop(0, nblk, count, jnp.zeros((nblk, tq), jnp.float32), unroll=8)
        notsel_t = jnp.where(cnt < float(SLC_TOPK), 0.0, 1.0)
        notsel = jnp.concatenate([jnp.zeros((nblk, tq), jnp.float32), notsel_t], axis=0).T
        for hq, qh in enumerate(q_heads):
            head = g * NSA_HPG + hq
            qaug_ref[0, :, head * LANES:(head + 1) * LANES] = (qh + notsel).astype(qaug_ref.dtype)


def _nsa_cmp_call(qn, kc, vc, ovt):
    B, S, _ = qn.shape
    tq = NSA_TQ
    nch = kc.shape[1]
    return pl.pallas_call(
        _nsa_cmp_kernel,
        grid=(B, S // tq),
        in_specs=[pl.BlockSpec((1, tq, NSA_WIDTH), lambda b, i: (b, i, 0)),
                  pl.BlockSpec((1, nch, 2 * LANES), lambda b, i: (b, 0, 0)),
                  pl.BlockSpec((1, nch, 4 * LANES), lambda b, i: (b, 0, 0)),
                  pl.BlockSpec(ovt.shape, lambda b, i: (0, 0))],
        out_specs=[pl.BlockSpec((1, tq, NSA_WIDTH), lambda b, i: (b, i, 0)),
                   pl.BlockSpec((1, tq, NSA_HEADS * LANES), lambda b, i: (b, i, 0))],
        out_shape=[jax.ShapeDtypeStruct((B, S, NSA_WIDTH), jnp.float32),
                   jax.ShapeDtypeStruct((B, S, NSA_HEADS * LANES), MXU_DTYPE)],
        scratch_shapes=[pltpu.VMEM((LANES // 2, tq), jnp.float32)],
        compiler_params=pltpu.CompilerParams(
            dimension_semantics=("arbitrary", "arbitrary"), vmem_limit_bytes=VMEM_LIMIT),
        name="nsa_cmp",
    )(qn, kc, vc, ovt)


def _nsa_attn_kernel(qaug_ref, ksa_ref, vs4_ref, kwp_ref, vw4_ref, oc_ref, gate_ref, y_ref):
    i = pl.program_id(1)
    tq = qaug_ref.shape[1]
    win_tiles = WINDOW // tq
    rows4 = NSA_HPG * tq
    ql = lax.broadcasted_iota(jnp.int32, (rows4, tq), 0) % tq
    kl = lax.broadcasted_iota(jnp.int32, (rows4, tq), 1)
    lower = kl <= ql
    sig = jax.nn.sigmoid(gate_ref[0])
    low = _lane((tq, LANES)) < HEAD_DIM
    head_order = (0, 2, 1, 3)

    for g in range(NSA_KV_GROUPS):
        qs = jnp.concatenate(
            [qaug_ref[0, :, (g * NSA_HPG + h) * LANES:(g * NSA_HPG + h + 1) * LANES] for h in head_order], axis=0)
        kcols = slice(g * LANES, (g + 1) * LANES)
        vlo = slice((2 * g) * LANES, (2 * g + 1) * LANES)
        vhi = slice((2 * g + 1) * LANES, (2 * g + 2) * LANES)

        def tile_rows(j):
            return pl.ds(pl.multiple_of(j * tq, tq), tq)

        def pv(p, v_ref, rows):
            pb = p.astype(MXU_DTYPE)
            return jnp.concatenate([_dot(pb[:rows4 // 2], v_ref[0, rows, vlo]),
                                    _dot(pb[rows4 // 2:], v_ref[0, rows, vhi])], axis=0)

        def branch(k_ref, v_ref, j_lo, far_masked):
            rows = tile_rows(i)
            s = jnp.where(lower, _dot_nt(qs, k_ref[0, rows, kcols]), -jnp.inf)
            m, l, p = _first_tile(s)
            acc = pv(p, v_ref, rows)

            def step(j, carry, masked):
                m, l, acc = carry
                rows = tile_rows(j)
                s = _dot_nt(qs, k_ref[0, rows, kcols])
                if masked:
                    s = jnp.where(lower, -jnp.inf, s)
                m, l, p, alpha = _next_tile(s, m, l)
                return m, l, alpha * acc + pv(p, v_ref, rows)

            carry = lax.fori_loop(j_lo, i, lambda j, c: step(j, c, False), (m, l, acc))
            if far_masked:
                carry = lax.cond(i >= win_tiles, lambda c: step(i - win_tiles, c, True), lambda c: c, carry)
            m, l, acc = carry
            return acc * (1.0 / l)

        o_s = branch(ksa_ref, vs4_ref, 0, False)
        o_w = branch(kwp_ref, vw4_ref, jnp.maximum(i - win_tiles + 1, 0), True)

        for p in range(NSA_HPG // 2):
            pair = g * (NSA_HPG // 2) + p
            r0 = slice(p * tq, (p + 1) * tq)
            r1 = slice(rows4 // 2 + p * tq, rows4 // 2 + (p + 1) * tq)

            def gate(br):
                c = br * NSA_HEADS + 2 * pair
                return jnp.where(low, sig[:, c:c + 1], sig[:, c + 1:c + 2])

            o_c = oc_ref[0, :, pair * LANES:(pair + 1) * LANES]
            y_ref[0, :, pair * LANES:(pair + 1) * LANES] = (
                gate(0) * o_c + gate(1) * (o_s[r0] + o_s[r1]) + gate(2) * (o_w[r0] + o_w[r1]))


def _nsa_attn_call(qaug, ksa, vs4, kwp, vw4, oc, gate):
    B, S, _ = qaug.shape
    tq = NSA_TQ
    tile = lambda w: pl.BlockSpec((1, tq, w), lambda b, i: (b, i, 0))
    seq = lambda w: pl.BlockSpec((1, S, w), lambda b, i: (b, 0, 0))
    return pl.pallas_call(
        _nsa_attn_kernel,
        grid=(B, S // tq),
        in_specs=[tile(NSA_HEADS * LANES), seq(2 * LANES), seq(4 * LANES), seq(2 * LANES), seq(4 * LANES),
                  tile(NSA_WIDTH), tile(LANES)],
        out_specs=tile(NSA_WIDTH),
        out_shape=jax.ShapeDtypeStruct((B, S, NSA_WIDTH), jnp.float32),
        compiler_params=pltpu.CompilerParams(
            dimension_semantics=("arbitrary", "arbitrary"), vmem_limit_bytes=VMEM_LIMIT),
        name="nsa_attn",
    )(qaug, ksa, vs4, kwp, vw4, oc, gate)


def _out_kernel(x_ref, ym_ref, yn_ref, gm_ref, gn_ref, wo_ref, gmlp_ref, wup_ref, wdn_ref, gfin_ref, o_ref):
    mixed = jnp.concatenate([_rms(ym_ref[...], gm_ref[...]), _rms(yn_ref[...], gn_ref[...])], axis=1)
    h = x_ref[...] + _dot(mixed.astype(MXU_DTYPE), wo_ref[...])
    hn = _rms(h, gmlp_ref[...]).astype(MXU_DTYPE)
    acc = h
    for c in range(D_FF // FF_CHUNK):
        a = jnp.maximum(_dot(hn, wup_ref[:, c * FF_CHUNK:(c + 1) * FF_CHUNK]), 0.0)
        acc = acc + _dot((a * a).astype(MXU_DTYPE), wdn_ref[c * FF_CHUNK:(c + 1) * FF_CHUNK, :])
    o_ref[...] = _rms(acc, gfin_ref[...])


def _out_call(x2, ym2, yn2, gm, gn, wo, gmlp, wup, wdn, gfin):
    T, D = x2.shape
    rows = OUT_ROWS
    tile = lambda w: pl.BlockSpec((rows, w), lambda i: (i, 0))
    full = lambda a: pl.BlockSpec(a.shape, lambda i: (0,) * a.ndim)
    return pl.pallas_call(
        _out_kernel,
        grid=(T // rows,),
        in_specs=[tile(D), tile(MLA_WIDTH), tile(NSA_WIDTH), full(gm), full(gn), full(wo), full(gmlp),
                  full(wup), full(wdn), full(gfin)],
        out_specs=tile(D),
        out_shape=jax.ShapeDtypeStruct((T, D), jnp.float32),
        compiler_params=pltpu.CompilerParams(
            dimension_semantics=("arbitrary",), vmem_limit_bytes=VMEM_LIMIT),
        name="out",
    )(x2, ym2, yn2, gm, gn, wo, gmlp, wup, wdn, gfin)


def _rope_cs(pos, dim):
    inv_freq = jnp.exp(-math.log(ROPE_THETA) * jnp.arange(0, dim, 2, dtype=jnp.float32) / dim)
    ang = pos.astype(jnp.float32)[:, None] * inv_freq[None, :]
    return jnp.cos(ang), jnp.sin(ang)


def _nsa_rope_tables(pos, width):
    c, s = _rope_cs(pos, NSA_ROPE_DIM)
    n = pos.shape[0]
    rest = HEAD_DIM - NSA_ROPE_DIM
    ch = jnp.concatenate([c, c, jnp.ones((n, rest), jnp.float32)], axis=1)
    sh = jnp.concatenate([-s, s, jnp.zeros((n, rest), jnp.float32)], axis=1)
    reps = width // HEAD_DIM
    return jnp.tile(ch, (1, reps)), jnp.tile(sh, (1, reps))


def _proj_tables(S):
    pos = jnp.arange(S)
    c, s = _rope_cs(pos, MLA_ROPE_DIM)
    z = lambda w: jnp.zeros((S, w), jnp.float32)
    scale = (MLA_NOPE_DIM + MLA_ROPE_DIM) ** -0.5
    pad = LANES - MLA_NOPE_DIM - MLA_ROPE_DIM
    cq = jnp.concatenate([jnp.ones((S, MLA_NOPE_DIM), jnp.float32), c, c, z(pad)], axis=1) * scale
    sq = jnp.concatenate([z(MLA_NOPE_DIM), -s, s, z(pad)], axis=1) * scale
    ck = jnp.concatenate([z(MLA_NOPE_DIM), c, c, z(pad)], axis=1)
    sk = jnp.concatenate([z(MLA_NOPE_DIM), -s, s, z(pad)], axis=1)
    cn, sn = _nsa_rope_tables(pos, LANES)
    onehot = (pos[:, None] // SLC_BLOCK == jnp.arange(LANES - HEAD_DIM)[None, :]).astype(jnp.float32)
    eneg = jnp.concatenate([z(HEAD_DIM), MASK_BIAS * onehot], axis=1)
    return jnp.concatenate([cq, sq, ck, sk, cn, sn, eneg], axis=1)


def _compress_tables(nch):
    end = jnp.arange(nch) * CMP_STRIDE + CMP_BLOCK - 1
    c, s = _nsa_rope_tables(end, HEAD_DIM)
    z = jnp.zeros((nch, LANES - HEAD_DIM), jnp.float32)
    return jnp.concatenate([c, z, s, z], axis=1)


def _overlap_t(nch, nblk):
    cs = jnp.arange(nch)[None, :] * CMP_STRIDE
    ss = jnp.arange(nblk)[:, None] * SLC_BLOCK
    ov = jnp.clip(jnp.minimum(cs + CMP_BLOCK, ss + SLC_BLOCK) - jnp.maximum(cs, ss), 0, None)
    ov = ov.astype(jnp.float32) / CMP_BLOCK
    ov = jnp.where(jnp.arange(nch)[None, :] < nch - 1, ov, 0.0)
    return jnp.concatenate([jnp.zeros_like(ov), ov], axis=0).astype(MXU_DTYPE)


def _pad_cols(w, width):
    return jnp.pad(w, ((0, 0), (0, width - w.shape[1])))


def _compress_consts(pe_k, w1_k, b1_k, w2_k, b2_k, pe_v, w1_v, b1_v, w2_v, b2_v):
    half = CMP_BLOCK // 2
    ncol = 2 * NSA_KV_GROUPS
    w1 = jnp.stack([w1_k, w1_k, w1_v, w1_v]).reshape(ncol, 2, half, HEAD_DIM, CMP_HIDDEN)
    eye = jnp.eye(ncol, dtype=w1.dtype)

    def spread(wh):
        return jnp.einsum('cldh,ce->lcdeh', wh, eye).reshape(half * ncol * HEAD_DIM, ncol * CMP_HIDDEN)

    pe = jnp.stack([pe_k, pe_k, pe_v, pe_v]).reshape(ncol, 2, half, HEAD_DIM)
    pe_row = lambda ph: ph.transpose(1, 0, 2).reshape(1, half * ncol * HEAD_DIM)
    b1 = jnp.concatenate([b1_k, b1_k, b1_v, b1_v]).reshape(1, ncol * CMP_HIDDEN)
    zc = jnp.zeros((CMP_HIDDEN, HEAD_DIM), w2_v.dtype)
    zb = jnp.zeros((HEAD_DIM,), b2_v.dtype)
    return [
        pe_row(pe[:, 0]), pe_row(pe[:, 1]),
        spread(w1[:, 0]).astype(MXU_DTYPE), spread(w1[:, 1]).astype(MXU_DTYPE), b1,
        _pad_cols(w2_k, LANES).astype(MXU_DTYPE), _pad_cols(b2_k[None, :], LANES),
        jnp.concatenate([w2_v, zc], axis=1).astype(MXU_DTYPE), jnp.concatenate([b2_v, zb])[None, :],
        jnp.concatenate([zc, w2_v], axis=1).astype(MXU_DTYPE), jnp.concatenate([zb, b2_v])[None, :],
    ]


def _proj_weights(w_in, w_uq, w_ukv):
    d = w_in.shape[0]
    z = lambda w: jnp.zeros((d, w), w_in.dtype)
    o_kr = MLA_Q_RANK + MLA_KV_RANK
    o_qn = o_kr + MLA_ROPE_DIM
    o_gate = o_qn + NSA_WIDTH + 6 * NSA_KV_WIDTH
    n_gate = 3 * NSA_HEADS
    win = jnp.concatenate([
        w_in[:, :o_kr],
        z(MLA_NOPE_DIM), w_in[:, o_kr:o_qn], z(LANES - MLA_NOPE_DIM - MLA_ROPE_DIM),
        w_in[:, o_qn:o_gate],
        w_in[:, o_gate:], z(LANES - n_gate)], axis=1)
    assert win.shape[1] == _U_COLS
    qd = MLA_NOPE_DIM + MLA_ROPE_DIM
    wq = jnp.pad(w_uq.reshape(MLA_Q_RANK, MLA_HEADS, qd), ((0, 0), (0, 0), (0, LANES - qd)))
    wq = wq.reshape(MLA_Q_RANK, MLA_HEADS * LANES)
    kv = w_ukv.reshape(MLA_KV_RANK, MLA_HEADS, MLA_NOPE_DIM + MLA_V_DIM)
    zk = jnp.zeros((MLA_KV_RANK, MLA_HEADS, LANES - MLA_NOPE_DIM), w_ukv.dtype)
    wk = jnp.concatenate([kv[..., :MLA_NOPE_DIM], zk], axis=-1).reshape(MLA_KV_RANK, MLA_HEADS * LANES)
    v = kv[..., MLA_NOPE_DIM:]
    zv = jnp.zeros_like(v)
    even = (jnp.arange(MLA_HEADS) % 2 == 0)[None, :, None]
    wv = jnp.concatenate([jnp.where(even, v, zv), jnp.where(even, zv, v)], axis=-1)
    wv = wv.reshape(MLA_KV_RANK, MLA_HEADS * LANES)
    return win.astype(MXU_DTYPE), wq.astype(MXU_DTYPE), wk.astype(MXU_DTYPE), wv.astype(MXU_DTYPE)


def kernel(x, g_mix_norm, w_in, g_cq, w_uq, g_ckv, w_ukv, cmp_pe_k, cmp_w1_k, cmp_b1_k, cmp_w2_k, cmp_b2_k,
           cmp_pe_v, cmp_w1_v, cmp_b1_v, cmp_w2_v, cmp_b2_v, g_out_mla, g_out_nsa, w_o, g_mlp_norm, w_up,
           w_down, g_final):
    B, S, D = x.shape
    depth = w_in.shape[0]
    nch = S // CMP_STRIDE
    nblk = S // SLC_BLOCK
    assert D == D_MODEL and S % MLA_TQ == 0 and S % PROJ_ROWS == 0 and (B * S) % OUT_ROWS == 0
    assert WINDOW % NSA_TQ == 0 and S % NSA_TQ == 0 and NSA_TQ % SLC_BLOCK == 0
    assert SLC_TOPK <= nblk <= LANES - HEAD_DIM
    row = lambda g: g.reshape(1, -1)

    tabs = _proj_tables(S)
    ctab = _compress_tables(nch)
    ovt = _overlap_t(nch, LANES // 2)

    assert depth == 1

    win, wq, wk, wv = _proj_weights(w_in[0], w_uq[0], w_ukv[0])
    qm, km, vm, qn, ksa, vs4, kwp, vw4, kcv, gate = _proj_call(
        x, tabs, row(g_mix_norm[0]), win, row(g_cq[0]), wq, row(g_ckv[0]), wk, wv)
    consts = _compress_consts(cmp_pe_k[0], cmp_w1_k[0], cmp_b1_k[0], cmp_w2_k[0], cmp_b2_k[0],
                              cmp_pe_v[0], cmp_w1_v[0], cmp_b1_v[0], cmp_w2_v[0], cmp_b2_v[0])
    kc, vc = _compress_call(kcv.reshape(B, nch, CMP_STRIDE * 2 * LANES), consts, ctab)
    y_mla = _mla_call(qm, km, vm)
    oc, qaug = _nsa_cmp_call(qn, kc, vc, ovt)
    y_nsa = _nsa_attn_call(qaug, ksa, vs4, kwp, vw4, oc, gate)
    out = _out_call(x.reshape(B * S, D), y_mla.reshape(B * S, MLA_WIDTH), y_nsa.reshape(B * S, NSA_WIDTH),
                    row(g_out_mla[0]), row(g_out_nsa[0]), w_o[0].astype(MXU_DTYPE), row(g_mlp_norm[0]),
                    w_up[0].astype(MXU_DTYPE), w_down[0].astype(MXU_DTYPE), row(g_final))
    return out.reshape(B, S, D)
```

```python
import functools
import math

import jax
import jax.numpy as jnp
import numpy as np
from jax import lax
from jax.experimental import pallas as pl
from jax.experimental.pallas import tpu as pltpu

D_MODEL = 1024
HEAD_DIM = 64
ROPE_THETA = 500000.0
NORM_EPS = 1e-6

MLA_HEADS = 8
MLA_Q_RANK = 256
MLA_KV_RANK = 128
MLA_NOPE_DIM = 64
MLA_ROPE_DIM = 32
MLA_V_DIM = 64
MLA_WIDTH = MLA_HEADS * MLA_V_DIM

NSA_HEADS = 8
NSA_KV_GROUPS = 2
NSA_HPG = NSA_HEADS // NSA_KV_GROUPS
NSA_ROPE_DIM = HEAD_DIM // 4
NSA_WIDTH = NSA_HEADS * HEAD_DIM
NSA_KV_WIDTH = NSA_KV_GROUPS * HEAD_DIM
CMP_BLOCK = 32
CMP_STRIDE = 16
CMP_HIDDEN = 2 * HEAD_DIM
SLC_BLOCK = 64
SLC_TOPK = 16
WINDOW = 512
FORCE_SCORE = 1e9
D_FF = 4 * D_MODEL

LANES = 128
VMEM_LIMIT = 56 * 1024 * 1024

MXU_DTYPE = jnp.bfloat16
PROJ_ROWS = 256
MLA_TQ = 256
NSA_TQ = 128
NSA_ATTN_TQ = 256
OUT_ROWS = 256
FF_CHUNK = 1024
LOG2E = math.log2(math.e)
MASK_BIAS = -1e9

_U_CQ = 0
_U_CKV = 256
_U_KPE = 384
_U_QN = 512
_U_KCV = 1024
_U_KSLC = 1280
_U_VSLC = 1408
_U_KWIN = 1536
_U_VWIN = 1664
_U_GATE = 1792
_U_COLS = 1920
_N_TABS = 7


def _dot(a, b):
    return jnp.dot(a, b, preferred_element_type=jnp.float32)


def _dot_nt(a, b):
    return lax.dot_general(a, b, (((1,), (1,)), ((), ())), preferred_element_type=jnp.float32)


def _rms(x, g):
    return x * lax.rsqrt(jnp.mean(x * x, axis=-1, keepdims=True) + NORM_EPS) * g


def _lane(shape):
    return lax.broadcasted_iota(jnp.int32, shape, len(shape) - 1)


def _swap_pairs(x, lo_end, width):
    n = x.shape[-1]
    lane = _lane(x.shape) % LANES
    return jnp.where(lane < lo_end, pltpu.roll(x, n - width, axis=1), pltpu.roll(x, width, axis=1))


def _nsa_rope(x, cn, sn):
    half = NSA_ROPE_DIM // 2
    lane = _lane(x.shape) % HEAD_DIM
    sw = jnp.where(lane < half, pltpu.roll(x, LANES - half, axis=1), pltpu.roll(x, half, axis=1))
    return x * cn + sw * sn


def _proj_kernel(x_ref, tab_ref, gmix_ref, win_ref, gcq_ref, wq_ref, gckv_ref, wk_ref, wv_ref,
                 qm_ref, km_ref, vm_ref, qn_ref, ksa_ref, vs4_ref, kwp_ref, vw4_ref, kcv_ref, gate_ref):
    x = x_ref[0]
    n = _rms(x, gmix_ref[...]).astype(MXU_DTYPE)
    u = _dot(n, win_ref[...])

    def tab(t):
        return tab_ref[:, t * LANES:(t + 1) * LANES]

    cq_t, sq_t, ck_t, sk_t, cn_t, sn_t, eneg_t = (tab(t) for t in range(_N_TABS))
    lane = _lane((x.shape[0], LANES))
    low = lane < HEAD_DIM

    cqn = _rms(u[:, _U_CQ:_U_CQ + MLA_Q_RANK], gcq_ref[...]).astype(MXU_DTYPE)
    qm = _dot(cqn, wq_ref[...])
    rope_end = MLA_NOPE_DIM + MLA_ROPE_DIM // 2
    for h in range(MLA_HEADS):
        qh = qm[:, h * LANES:(h + 1) * LANES]
        qsw = _swap_pairs(qh, rope_end, MLA_ROPE_DIM // 2)
        qm_ref[0, :, h * LANES:(h + 1) * LANES] = (qh * cq_t + qsw * sq_t).astype(qm_ref.dtype)

    ckvn = _rms(u[:, _U_CKV:_U_CKV + MLA_KV_RANK], gckv_ref[...]).astype(MXU_DTYPE)
    km = _dot(ckvn, wk_ref[...])
    kpe = u[:, _U_KPE:_U_KPE + LANES]
    kpe = kpe * ck_t + _swap_pairs(kpe, rope_end, MLA_ROPE_DIM // 2) * sk_t
    for h in range(MLA_HEADS):
        km_ref[0, :, h * LANES:(h + 1) * LANES] = (km[:, h * LANES:(h + 1) * LANES] + kpe).astype(km_ref.dtype)
    vm_ref[0] = _dot(ckvn, wv_ref[...]).astype(vm_ref.dtype)

    nsa_scale = HEAD_DIM ** -0.5 * LOG2E
    for c in range(NSA_WIDTH // LANES):
        qc = u[:, _U_QN + c * LANES:_U_QN + (c + 1) * LANES]
        qn_ref[0, :, c * LANES:(c + 1) * LANES] = (_nsa_rope(qc, cn_t, sn_t) * nsa_scale).astype(qn_ref.dtype)

    ks = _nsa_rope(u[:, _U_KSLC:_U_KSLC + LANES], cn_t, sn_t)
    ks_r = pltpu.roll(ks, HEAD_DIM, axis=1)
    ksa_ref[0, :, 0:LANES] = jnp.where(low, ks, eneg_t).astype(ksa_ref.dtype)
    ksa_ref[0, :, LANES:2 * LANES] = jnp.where(low, ks_r, eneg_t).astype(ksa_ref.dtype)

    kw = _nsa_rope(u[:, _U_KWIN:_U_KWIN + LANES], cn_t, sn_t)
    kw_r = pltpu.roll(kw, HEAD_DIM, axis=1)
    kwp_ref[0, :, 0:LANES] = jnp.where(low, kw, 0.0).astype(kwp_ref.dtype)
    kwp_ref[0, :, LANES:2 * LANES] = jnp.where(low, kw_r, 0.0).astype(kwp_ref.dtype)

    def four_way(v, ref):
        v_r = pltpu.roll(v, HEAD_DIM, axis=1)
        ref[0, :, 0 * LANES:1 * LANES] = jnp.where(low, v, 0.0).astype(ref.dtype)
        ref[0, :, 1 * LANES:2 * LANES] = jnp.where(low, 0.0, v_r).astype(ref.dtype)
        ref[0, :, 2 * LANES:3 * LANES] = jnp.where(low, v_r, 0.0).astype(ref.dtype)
        ref[0, :, 3 * LANES:4 * LANES] = jnp.where(low, 0.0, v).astype(ref.dtype)

    four_way(u[:, _U_VSLC:_U_VSLC + LANES], vs4_ref)
    four_way(u[:, _U_VWIN:_U_VWIN + LANES], vw4_ref)

    kcv_ref[0] = u[:, _U_KCV:_U_KCV + 2 * LANES]
    gate_ref[0] = u[:, _U_GATE:_U_GATE + LANES]


def _proj_call(x, tabs, gmix, win, gcq, wq, gckv, wk, wv):
    B, S, D = x.shape
    rows = PROJ_ROWS
    ns = S // rows
    tok = lambda w: pl.BlockSpec((1, rows, w), lambda i, b: (b, i, 0))
    full = lambda a: pl.BlockSpec(a.shape, lambda i, b: (0,) * a.ndim)
    bf = MXU_DTYPE
    out_shapes = [
        jax.ShapeDtypeStruct((B, S, MLA_HEADS * LANES), bf),
        jax.ShapeDtypeStruct((B, S, MLA_HEADS * LANES), bf),
        jax.ShapeDtypeStruct((B, S, MLA_HEADS * LANES), bf),
        jax.ShapeDtypeStruct((B, S, NSA_WIDTH), bf),
        jax.ShapeDtypeStruct((B, S, 2 * LANES), bf),
        jax.ShapeDtypeStruct((B, S, 4 * LANES), bf),
        jax.ShapeDtypeStruct((B, S, 2 * LANES), bf),
        jax.ShapeDtypeStruct((B, S, 4 * LANES), bf),
        jax.ShapeDtypeStruct((B, S, 2 * LANES), jnp.float32),
        jax.ShapeDtypeStruct((B, S, LANES), jnp.float32),
    ]
    return pl.pallas_call(
        _proj_kernel,
        grid=(ns, B),
        in_specs=[tok(D), pl.BlockSpec((rows, _N_TABS * LANES), lambda i, b: (i, 0)),
                  full(gmix), full(win), full(gcq), full(wq), full(gckv), full(wk), full(wv)],
        out_specs=[tok(s.shape[-1]) for s in out_shapes],
        out_shape=out_shapes,
        compiler_params=pltpu.CompilerParams(
            dimension_semantics=("arbitrary", "arbitrary"), vmem_limit_bytes=VMEM_LIMIT),
        name="proj",
    )(x, tabs, gmix, win, gcq, wq, gckv, wk, wv)


def _gelu_tanh(x):
    return 0.5 * x * (1.0 + jnp.tanh(math.sqrt(2.0 / math.pi) * (x + 0.044715 * (x * x * x))))


def _compress_kernel(x_ref, pea_ref, peb_ref, wa_ref, wb_ref, b1_ref, w2k_ref, b2k_ref,
                     w2vl_ref, b2vl_ref, w2vh_ref, b2vh_ref, ctab_ref, kc_ref, vc_ref):
    x = x_ref[0]
    nch = x.shape[0]
    a = _dot((x + pea_ref[...]).astype(MXU_DTYPE), wa_ref[...])
    b = _dot((x + peb_ref[...]).astype(MXU_DTYPE), wb_ref[...])
    hid = _gelu_tanh(a + pltpu.roll(b, nch - 1, axis=0) + b1_ref[...]).astype(MXU_DTYPE)
    cc = ctab_ref[:, 0:LANES]
    sc = ctab_ref[:, LANES:2 * LANES]
    for g in range(NSA_KV_GROUPS):
        hk = hid[:, g * LANES:(g + 1) * LANES]
        kc = _dot(hk, w2k_ref[...]) + b2k_ref[...]
        kc_ref[0, :, g * LANES:(g + 1) * LANES] = _nsa_rope(kc, cc, sc).astype(kc_ref.dtype)
        hv = hid[:, (NSA_KV_GROUPS + g) * LANES:(NSA_KV_GROUPS + g + 1) * LANES]
        vc_ref[0, :, (2 * g) * LANES:(2 * g + 1) * LANES] = (
            _dot(hv, w2vl_ref[...]) + b2vl_ref[...]).astype(vc_ref.dtype)
        vc_ref[0, :, (2 * g + 1) * LANES:(2 * g + 2) * LANES] = (
            _dot(hv, w2vh_ref[...]) + b2vh_ref[...]).astype(vc_ref.dtype)


def _compress_call(kcv_chunks, consts, ctab):
    B, nch, width = kcv_chunks.shape
    full = lambda a: pl.BlockSpec(a.shape, lambda b: (0,) * a.ndim)
    return pl.pallas_call(
        _compress_kernel,
        grid=(B,),
        in_specs=[pl.BlockSpec((1, nch, width), lambda b: (b, 0, 0))] + [full(c) for c in consts] + [full(ctab)],
        out_specs=[pl.BlockSpec((1, nch, 2 * LANES), lambda b: (b, 0, 0)),
                   pl.BlockSpec((1, nch, 4 * LANES), lambda b: (b, 0, 0))],
        out_shape=[jax.ShapeDtypeStruct((B, nch, 2 * LANES), MXU_DTYPE),
                   jax.ShapeDtypeStruct((B, nch, 4 * LANES), MXU_DTYPE)],
        compiler_params=pltpu.CompilerParams(
            dimension_semantics=("arbitrary",), vmem_limit_bytes=VMEM_LIMIT),
        name="compress",
    )(kcv_chunks, *consts, ctab)


def _first_tile(s):
    m = jnp.max(s, axis=-1, keepdims=True)
    p = jnp.exp2(s - m)
    return m, jnp.sum(p, axis=-1, keepdims=True), p


def _next_tile(s, m, l):
    m_new = jnp.maximum(m, jnp.max(s, axis=-1, keepdims=True))
    alpha = jnp.exp2(m - m_new)
    p = jnp.exp2(s - m_new)
    return m_new, alpha * l + jnp.sum(p, axis=-1, keepdims=True), p, alpha


def _flash_streams(n_streams, scores, pv, lower, i, tq):
    start_d = pl.multiple_of(i * tq, tq)
    carry = []
    for st in range(n_streams):
        m, l, p = _first_tile(jnp.where(lower, scores(st, start_d, tq), -jnp.inf))
        carry.append((m, l, pv(st, p, start_d, tq)))

    def step(start, width, carry):
        out = []
        for st, (m, l, acc) in enumerate(carry):
            m, l, p, alpha = _next_tile(scores(st, start, width), m, l)
            out.append((m, l, alpha * acc + pv(st, p, start, width)))
        return tuple(out)

    carry = lax.fori_loop(0, i // 2, lambda jj, c: step(pl.multiple_of(jj * (2 * tq), 2 * tq), 2 * tq, c),
                          tuple(carry))
    carry = lax.cond(i % 2 == 1, lambda c: step(pl.multiple_of((i - 1) * tq, tq), tq, c), lambda c: c, carry)
    return [acc * (1.0 / l) for (m, l, acc) in carry]


def _mla_kernel(q_ref, k_ref, v_ref, o_ref):
    i = pl.program_id(2)
    tq = q_ref.shape[1]
    lower = lax.broadcasted_iota(jnp.int32, (tq, tq), 1) <= lax.broadcasted_iota(jnp.int32, (tq, tq), 0)
    cols = [slice(hh * LANES, (hh + 1) * LANES) for hh in range(2)]
    qs = [q_ref[0, :, c] for c in cols]

    def scores(hh, start, width):
        return _dot_nt(qs[hh], k_ref[0, pl.ds(start, width), cols[hh]])

    def pv(hh, p, start, width):
        return _dot(p.astype(MXU_DTYPE), v_ref[0, pl.ds(start, width), cols[hh]])

    o0, o1 = _flash_streams(2, scores, pv, lower, i, tq)
    o_ref[0] = o0 + o1


def _mla_call(qm, km, vm):
    B, S, _ = qm.shape
    tq = MLA_TQ
    pairs = MLA_HEADS // 2
    return pl.pallas_call(
        _mla_kernel,
        grid=(B, pairs, S // tq),
        in_specs=[pl.BlockSpec((1, tq, 2 * LANES), lambda b, p, i: (b, i, p)),
                  pl.BlockSpec((1, S, 2 * LANES), lambda b, p, i: (b, 0, p)),
                  pl.BlockSpec((1, S, 2 * LANES), lambda b, p, i: (b, 0, p))],
        out_specs=pl.BlockSpec((1, tq, LANES), lambda b, p, i: (b, i, p)),
        out_shape=jax.ShapeDtypeStruct((B, S, MLA_WIDTH), jnp.float32),
        compiler_params=pltpu.CompilerParams(
            dimension_semantics=("arbitrary", "arbitrary", "arbitrary"), vmem_limit_bytes=VMEM_LIMIT),
        name="mla",
    )(qm, km, vm)


def _nsa_cmp_kernel(q_ref, kc_ref, vc_ref, ovt_ref, oc_ref, qaug_ref, score_ref):
    i = pl.program_id(1)
    tq = q_ref.shape[1]
    nch = kc_ref.shape[1]
    nblk = score_ref.shape[0]
    q0 = i * tq
    qpos_c = q0 + lax.broadcasted_iota(jnp.int32, (tq, nch), 0)
    cmp_end = lax.broadcasted_iota(jnp.int32, (tq, nch), 1) * CMP_STRIDE + (CMP_BLOCK - 1)
    mask_c = cmp_end <= qpos_c
    low = _lane((tq, LANES)) < HEAD_DIM

    blk = lax.broadcasted_iota(jnp.int32, (nblk, tq), 0)
    qpos_r = q0 + lax.broadcasted_iota(jnp.int32, (nblk, tq), 1)
    cur = qpos_r // SLC_BLOCK
    forced = (blk == 0) | (blk == cur) | (blk == cur - 1)
    causal = blk * SLC_BLOCK <= qpos_r

    for g in range(NSA_KV_GROUPS):
        kc = kc_ref[0, :, g * LANES:(g + 1) * LANES]
        psum = jnp.zeros((tq, nch), jnp.float32)
        q_heads = []
        for p in range(NSA_HPG // 2):
            pair = g * (NSA_HPG // 2) + p
            qp = q_ref[0, :, pair * LANES:(pair + 1) * LANES].astype(jnp.float32)
            qr = pltpu.roll(qp, HEAD_DIM, axis=1)
            o_pair = None
            for hh, qsrc in enumerate((qp, qr)):
                qh = jnp.where(low, qsrc, 0.0)
                q_heads.append(qh)
                s = jnp.where(mask_c, _dot_nt(qh.astype(MXU_DTYPE), kc), -jnp.inf)
                m = jnp.max(s, axis=-1, keepdims=True)
                m = jnp.where(m == -jnp.inf, 0.0, m)
                e = jnp.exp2(s - m)
                pr = e / jnp.maximum(jnp.sum(e, axis=-1, keepdims=True), 1e-30)
                psum = psum + pr
                o = _dot(pr.astype(MXU_DTYPE), vc_ref[0, :, (2 * g + hh) * LANES:(2 * g + hh + 1) * LANES])
                o_pair = o if o_pair is None else o_pair + o
            oc_ref[0, :, pair * LANES:(pair + 1) * LANES] = o_pair

        p_hi = psum.astype(MXU_DTYPE)
        p_lo = (psum - p_hi.astype(jnp.float32)).astype(MXU_DTYPE)
        imp_t = _dot_nt(ovt_ref[...], p_hi) + _dot_nt(ovt_ref[...], p_lo)
        score = jnp.where(forced, FORCE_SCORE, jnp.where(causal, imp_t[nblk:2 * nblk, :], -FORCE_SCORE))
        score_ref[...] = score

        def count(r, cnt):
            row = score_ref[pl.ds(r, 1), :]
            beats = (row > score) | ((row == score) & (r < blk))
            return cnt + jnp.where(beats, 1.0, 0.0)

        cnt = lax.fori_loop(0, nblk, count, jnp.zeros((nblk, tq), jnp.float32), unroll=8)
        notsel_t = jnp.where(cnt < float(SLC_TOPK), 0.0, 1.0)
        notsel = jnp.concatenate([jnp.zeros((nblk, tq), jnp.float32), notsel_t], axis=0).T
        for hq, qh in enumerate(q_heads):
            head = g * NSA_HPG + hq
            qaug_ref[0, :, head * LANES:(head + 1) * LANES] = (qh + notsel).astype(qaug_ref.dtype)


def _nsa_cmp_call(qn, kc, vc, ovt):
    B, S, _ = qn.shape
    tq = NSA_TQ
    nch = kc.shape[1]
    return pl.pallas_call(
        _nsa_cmp_kernel,
        grid=(B, S // tq),
        in_specs=[pl.BlockSpec((1, tq, NSA_WIDTH), lambda b, i: (b, i, 0)),
                  pl.BlockSpec((1, nch, 2 * LANES), lambda b, i: (b, 0, 0)),
                  pl.BlockSpec((1, nch, 4 * LANES), lambda b, i: (b, 0, 0)),
                  pl.BlockSpec(ovt.shape, lambda b, i: (0, 0))],
        out_specs=[pl.BlockSpec((1, tq, NSA_WIDTH), lambda b, i: (b, i, 0)),
                   pl.BlockSpec((1, tq, NSA_HEADS * LANES), lambda b, i: (b, i, 0))],
        out_shape=[jax.ShapeDtypeStruct((B, S, NSA_WIDTH), jnp.float32),
                   jax.ShapeDtypeStruct((B, S, NSA_HEADS * LANES), MXU_DTYPE)],
        scratch_shapes=[pltpu.VMEM((LANES // 2, tq), jnp.float32)],
        compiler_params=pltpu.CompilerParams(
            dimension_semantics=("arbitrary", "arbitrary"), vmem_limit_bytes=VMEM_LIMIT),
        name="nsa_cmp",
    )(qn, kc, vc, ovt)


def _nsa_attn_kernel(qaug_ref, ksa_ref, vs4_ref, kwp_ref, vw4_ref, oc_ref, gate_ref, y_ref):
    i = pl.program_id(1)
    tq = qaug_ref.shape[1]
    rows4 = NSA_HPG * tq
    half = rows4 // 2
    wwin = WINDOW + tq
    ql = lax.broadcasted_iota(jnp.int32, (rows4, tq), 0) % tq
    lower = lax.broadcasted_iota(jnp.int32, (rows4, tq), 1) <= ql
    sig = jax.nn.sigmoid(gate_ref[0])
    low = _lane((tq, LANES)) < HEAD_DIM
    head_order = (0, 2, 1, 3)
    groups = range(NSA_KV_GROUPS)
    qs = [jnp.concatenate(
        [qaug_ref[0, :, (g * NSA_HPG + h) * LANES:(g * NSA_HPG + h + 1) * LANES] for h in head_order], axis=0)
        for g in groups]

    def pv_from(v_ref):
        def pv(g, p, start, width):
            pb = p.astype(MXU_DTYPE)
            rows = pl.ds(start, width)
            return jnp.concatenate(
                [_dot(pb[:half], v_ref[0, rows, (2 * g) * LANES:(2 * g + 1) * LANES]),
                 _dot(pb[half:], v_ref[0, rows, (2 * g + 1) * LANES:(2 * g + 2) * LANES])], axis=0)
        return pv

    o_slc = _flash_streams(
        NSA_KV_GROUPS,
        lambda g, start, width: _dot_nt(qs[g], ksa_ref[0, pl.ds(start, width), g * LANES:(g + 1) * LANES]),
        pv_from(vs4_ref), lower, i, tq)

    wstart = pl.multiple_of(jnp.maximum(i * tq - WINDOW, 0), tq)
    qpos = i * tq + lax.broadcasted_iota(jnp.int32, (rows4, wwin), 0) % tq
    dist = qpos - (wstart + lax.broadcasted_iota(jnp.int32, (rows4, wwin), 1))
    band = (dist >= 0) & (dist < WINDOW)
    pv_w = pv_from(vw4_ref)
    o_win = []
    for g in groups:
        s = _dot_nt(qs[g], kwp_ref[0, pl.ds(wstart, wwin), g * LANES:(g + 1) * LANES])
        m, l, p = _first_tile(jnp.where(band, s, -jnp.inf))
        o_win.append(pv_w(g, p, wstart, wwin) * (1.0 / l))

    for g in groups:
        for p in range(NSA_HPG // 2):
            pair = g * (NSA_HPG // 2) + p
            r0 = slice(p * tq, (p + 1) * tq)
            r1 = slice(half + p * tq, half + (p + 1) * tq)

            def gate(br):
                c = br * NSA_HEADS + 2 * pair
                return jnp.where(low, sig[:, c:c + 1], sig[:, c + 1:c + 2])

            o_c = oc_ref[0, :, pair * LANES:(pair + 1) * LANES]
            y_ref[0, :, pair * LANES:(pair + 1) * LANES] = (
                gate(0) * o_c + gate(1) * (o_slc[g][r0] + o_slc[g][r1]) + gate(2) * (o_win[g][r0] + o_win[g][r1]))


def _nsa_attn_call(qaug, ksa, vs4, kwp, vw4, oc, gate):
    B, S, _ = qaug.shape
    tq = NSA_ATTN_TQ
    tile = lambda w: pl.BlockSpec((1, tq, w), lambda b, i: (b, i, 0))
    seq = lambda w: pl.BlockSpec((1, S, w), lambda b, i: (b, 0, 0))
    return pl.pallas_call(
        _nsa_attn_kernel,
        grid=(B, S // tq),
        in_specs=[tile(NSA_HEADS * LANES), seq(2 * LANES), seq(4 * LANES), seq(2 * LANES), seq(4 * LANES),
                  tile(NSA_WIDTH), tile(LANES)],
        out_specs=tile(NSA_WIDTH),
        out_shape=jax.ShapeDtypeStruct((B, S, NSA_WIDTH), jnp.float32),
        compiler_params=pltpu.CompilerParams(
            dimension_semantics=("arbitrary", "arbitrary"), vmem_limit_bytes=VMEM_LIMIT),
        name="nsa_attn",
    )(qaug, ksa, vs4, kwp, vw4, oc, gate)


def _out_kernel(x_ref, ym_ref, yn_ref, gm_ref, gn_ref, wo_ref, gmlp_ref, wup_ref, wdn_ref, gfin_ref, o_ref):
    mixed = jnp.concatenate([_rms(ym_ref[...], gm_ref[...]), _rms(yn_ref[...], gn_ref[...])], axis=1)
    h = x_ref[...] + _dot(mixed.astype(MXU_DTYPE), wo_ref[...])
    hn = _rms(h, gmlp_ref[...]).astype(MXU_DTYPE)
    acc = h
    for c in range(D_FF // FF_CHUNK):
        a = jnp.maximum(_dot(hn, wup_ref[:, c * FF_CHUNK:(c + 1) * FF_CHUNK]), 0.0)
        acc = acc + _dot((a * a).astype(MXU_DTYPE), wdn_ref[c * FF_CHUNK:(c + 1) * FF_CHUNK, :])
    o_ref[...] = _rms(acc, gfin_ref[...])


def _out_call(x2, ym2, yn2, gm, gn, wo, gmlp, wup, wdn, gfin):
    T, D = x2.shape
    rows = OUT_ROWS
    tile = lambda w: pl.BlockSpec((rows, w), lambda i: (i, 0))
    full = lambda a: pl.BlockSpec(a.shape, lambda i: (0,) * a.ndim)
    return pl.pallas_call(
        _out_kernel,
        grid=(T // rows,),
        in_specs=[tile(D), tile(MLA_WIDTH), tile(NSA_WIDTH), full(gm), full(gn), full(wo), full(gmlp),
                  full(wup), full(wdn), full(gfin)],
        out_specs=tile(D),
        out_shape=jax.ShapeDtypeStruct((T, D), jnp.float32),
        compiler_params=pltpu.CompilerParams(
            dimension_semantics=("arbitrary",), vmem_limit_bytes=VMEM_LIMIT),
        name="out",
    )(x2, ym2, yn2, gm, gn, wo, gmlp, wup, wdn, gfin)


def _rope_cs(pos, dim):
    inv_freq = jnp.exp(-math.log(ROPE_THETA) * jnp.arange(0, dim, 2, dtype=jnp.float32) / dim)
    ang = pos.astype(jnp.float32)[:, None] * inv_freq[None, :]
    return jnp.cos(ang), jnp.sin(ang)


def _nsa_rope_tables(pos, width):
    c, s = _rope_cs(pos, NSA_ROPE_DIM)
    n = pos.shape[0]
    rest = HEAD_DIM - NSA_ROPE_DIM
    ch = jnp.concatenate([c, c, jnp.ones((n, rest), jnp.float32)], axis=1)
    sh = jnp.concatenate([-s, s, jnp.zeros((n, rest), jnp.float32)], axis=1)
    reps = width // HEAD_DIM
    return jnp.tile(ch, (1, reps)), jnp.tile(sh, (1, reps))


def _proj_tables(S):
    pos = jnp.arange(S)
    c, s = _rope_cs(pos, MLA_ROPE_DIM)
    z = lambda w: jnp.zeros((S, w), jnp.float32)
    scale = (MLA_NOPE_DIM + MLA_ROPE_DIM) ** -0.5 * LOG2E
    pad = LANES - MLA_NOPE_DIM - MLA_ROPE_DIM
    cq = jnp.concatenate([jnp.ones((S, MLA_NOPE_DIM), jnp.float32), c, c, z(pad)], axis=1) * scale
    sq = jnp.concatenate([z(MLA_NOPE_DIM), -s, s, z(pad)], axis=1) * scale
    ck = jnp.concatenate([z(MLA_NOPE_DIM), c, c, z(pad)], axis=1)
    sk = jnp.concatenate([z(MLA_NOPE_DIM), -s, s, z(pad)], axis=1)
    cn, sn = _nsa_rope_tables(pos, LANES)
    onehot = (pos[:, None] // SLC_BLOCK == jnp.arange(LANES - HEAD_DIM)[None, :]).astype(jnp.float32)
    eneg = jnp.concatenate([z(HEAD_DIM), MASK_BIAS * onehot], axis=1)
    return jnp.concatenate([cq, sq, ck, sk, cn, sn, eneg], axis=1)


def _compress_tables(nch):
    end = jnp.arange(nch) * CMP_STRIDE + CMP_BLOCK - 1
    c, s = _nsa_rope_tables(end, HEAD_DIM)
    z = jnp.zeros((nch, LANES - HEAD_DIM), jnp.float32)
    return jnp.concatenate([c, z, s, z], axis=1)


def _overlap_t(nch, nblk):
    cs = jnp.arange(nch)[None, :] * CMP_STRIDE
    ss = jnp.arange(nblk)[:, None] * SLC_BLOCK
    ov = jnp.clip(jnp.minimum(cs + CMP_BLOCK, ss + SLC_BLOCK) - jnp.maximum(cs, ss), 0, None)
    ov = ov.astype(jnp.float32) / CMP_BLOCK
    ov = jnp.where(jnp.arange(nch)[None, :] < nch - 1, ov, 0.0)
    return jnp.concatenate([jnp.zeros_like(ov), ov], axis=0).astype(MXU_DTYPE)


def _pad_cols(w, width):
    return jnp.pad(w, ((0, 0), (0, width - w.shape[1])))


def _compress_consts(pe_k, w1_k, b1_k, w2_k, b2_k, pe_v, w1_v, b1_v, w2_v, b2_v):
    half = CMP_BLOCK // 2
    ncol = 2 * NSA_KV_GROUPS
    w1 = jnp.stack([w1_k, w1_k, w1_v, w1_v]).reshape(ncol, 2, half, HEAD_DIM, CMP_HIDDEN)
    eye = jnp.eye(ncol, dtype=w1.dtype)

    def spread(wh):
        return jnp.einsum('cldh,ce->lcdeh', wh, eye).reshape(half * ncol * HEAD_DIM, ncol * CMP_HIDDEN)

    pe = jnp.stack([pe_k, pe_k, pe_v, pe_v]).reshape(ncol, 2, half, HEAD_DIM)
    pe_row = lambda ph: ph.transpose(1, 0, 2).reshape(1, half * ncol * HEAD_DIM)
    b1 = jnp.concatenate([b1_k, b1_k, b1_v, b1_v]).reshape(1, ncol * CMP_HIDDEN)
    zc = jnp.zeros((CMP_HIDDEN, HEAD_DIM), w2_v.dtype)
    zb = jnp.zeros((HEAD_DIM,), b2_v.dtype)
    return [
        pe_row(pe[:, 0]), pe_row(pe[:, 1]),
        spread(w1[:, 0]).astype(MXU_DTYPE), spread(w1[:, 1]).astype(MXU_DTYPE), b1,
        _pad_cols(w2_k, LANES).astype(MXU_DTYPE), _pad_cols(b2_k[None, :], LANES),
        jnp.concatenate([w2_v, zc], axis=1).astype(MXU_DTYPE), jnp.concatenate([b2_v, zb])[None, :],
        jnp.concatenate([zc, w2_v], axis=1).astype(MXU_DTYPE), jnp.concatenate([zb, b2_v])[None, :],
    ]


def _proj_weights(w_in, w_uq, w_ukv):
    d = w_in.shape[0]
    z = lambda w: jnp.zeros((d, w), w_in.dtype)
    o_kr = MLA_Q_RANK + MLA_KV_RANK
    o_qn = o_kr + MLA_ROPE_DIM
    o_gate = o_qn + NSA_WIDTH + 6 * NSA_KV_WIDTH
    n_gate = 3 * NSA_HEADS
    win = jnp.concatenate([
        w_in[:, :o_kr],
        z(MLA_NOPE_DIM), w_in[:, o_kr:o_qn], z(LANES - MLA_NOPE_DIM - MLA_ROPE_DIM),
        w_in[:, o_qn:o_gate],
        w_in[:, o_gate:], z(LANES - n_gate)], axis=1)
    assert win.shape[1] == _U_COLS
    qd = MLA_NOPE_DIM + MLA_ROPE_DIM
    wq = jnp.pad(w_uq.reshape(MLA_Q_RANK, MLA_HEADS, qd), ((0, 0), (0, 0), (0, LANES - qd)))
    wq = wq.reshape(MLA_Q_RANK, MLA_HEADS * LANES)
    kv = w_ukv.reshape(MLA_KV_RANK, MLA_HEADS, MLA_NOPE_DIM + MLA_V_DIM)
    zk = jnp.zeros((MLA_KV_RANK, MLA_HEADS, LANES - MLA_NOPE_DIM), w_ukv.dtype)
    wk = jnp.concatenate([kv[..., :MLA_NOPE_DIM], zk], axis=-1).reshape(MLA_KV_RANK, MLA_HEADS * LANES)
    v = kv[..., MLA_NOPE_DIM:]
    zv = jnp.zeros_like(v)
    even = (jnp.arange(MLA_HEADS) % 2 == 0)[None, :, None]
    wv = jnp.concatenate([jnp.where(even, v, zv), jnp.where(even, zv, v)], axis=-1)
    wv = wv.reshape(MLA_KV_RANK, MLA_HEADS * LANES)
    return win.astype(MXU_DTYPE), wq.astype(MXU_DTYPE), wk.astype(MXU_DTYPE), wv.astype(MXU_DTYPE)


def kernel(x, g_mix_norm, w_in, g_cq, w_uq, g_ckv, w_ukv, cmp_pe_k, cmp_w1_k, cmp_b1_k, cmp_w2_k, cmp_b2_k,
           cmp_pe_v, cmp_w1_v, cmp_b1_v, cmp_w2_v, cmp_b2_v, g_out_mla, g_out_nsa, w_o, g_mlp_norm, w_up,
           w_down, g_final):
    B, S, D = x.shape
    depth = w_in.shape[0]
    nch = S // CMP_STRIDE
    nblk = S // SLC_BLOCK
    assert D == D_MODEL and S % MLA_TQ == 0 and S % PROJ_ROWS == 0 and (B * S) % OUT_ROWS == 0
    assert S % NSA_TQ == 0 and S % NSA_ATTN_TQ == 0 and WINDOW % NSA_ATTN_TQ == 0 and S >= WINDOW + NSA_ATTN_TQ
    assert SLC_TOPK <= nblk <= LANES - HEAD_DIM
    row = lambda g: g.reshape(1, -1)

    tabs = _proj_tables(S)
    ctab = _compress_tables(nch)
    ovt = _overlap_t(nch, LANES // 2)

    assert depth == 1

    win, wq, wk, wv = _proj_weights(w_in[0], w_uq[0], w_ukv[0])
    qm, km, vm, qn, ksa, vs4, kwp, vw4, kcv, gate = _proj_call(
        x, tabs, row(g_mix_norm[0]), win, row(g_cq[0]), wq, row(g_ckv[0]), wk, wv)
    consts = _compress_consts(cmp_pe_k[0], cmp_w1_k[0], cmp_b1_k[0], cmp_w2_k[0], cmp_b2_k[0],
                              cmp_pe_v[0], cmp_w1_v[0], cmp_b1_v[0], cmp_w2_v[0], cmp_b2_v[0])
    kc, vc = _compress_call(kcv.reshape(B, nch, CMP_STRIDE * 2 * LANES), consts, ctab)
    y_mla = _mla_call(qm, km, vm)
    oc, qaug = _nsa_cmp_call(qn, kc, vc, ovt)
    y_nsa = _nsa_attn_call(qaug, ksa, vs4, kwp, vw4, oc, gate)
    out = _out_call(x.reshape(B * S, D), y_mla.reshape(B * S, MLA_WIDTH), y_nsa.reshape(B * S, NSA_WIDTH),
                    row(g_out_mla[0]), row(g_out_nsa[0]), w_o[0].astype(MXU_DTYPE), row(g_mlp_norm[0]),
                    w_up[0].astype(MXU_DTYPE), w_down[0].astype(MXU_DTYPE), row(g_final))
    return out.reshape(B, S, D)
```

```python
import functools
import math

import jax
import jax.numpy as jnp
import numpy as np
from jax import lax
from jax.experimental import pallas as pl
from jax.experimental.pallas import tpu as pltpu

D_MODEL = 1024
HEAD_DIM = 64
ROPE_THETA = 500000.0
NORM_EPS = 1e-6

MLA_HEADS = 8
MLA_Q_RANK = 256
MLA_KV_RANK = 128
MLA_NOPE_DIM = 64
MLA_ROPE_DIM = 32
MLA_V_DIM = 64
MLA_WIDTH = MLA_HEADS * MLA_V_DIM

NSA_HEADS = 8
NSA_KV_GROUPS = 2
NSA_HPG = NSA_HEADS // NSA_KV_GROUPS
NSA_ROPE_DIM = HEAD_DIM // 4
NSA_WIDTH = NSA_HEADS * HEAD_DIM
NSA_KV_WIDTH = NSA_KV_GROUPS * HEAD_DIM
CMP_BLOCK = 32
CMP_STRIDE = 16
CMP_HIDDEN = 2 * HEAD_DIM
SLC_BLOCK = 64
SLC_TOPK = 16
WINDOW = 512
FORCE_SCORE = 1e9
D_FF = 4 * D_MODEL

LANES = 128
VMEM_LIMIT = 56 * 1024 * 1024

MXU_DTYPE = jnp.bfloat16
PROJ_ROWS = 256
MLA_TQ = 1024
MLA_WIDE = 2
NSA_TQ = 256
NSA_ATTN_TQ = 256
NSA_WIDE = 4
RANK_UNROLL = NSA_TQ // SLC_BLOCK
OUT_ROWS = 256
FF_CHUNK = 1024
LOG2E = math.log2(math.e)
MASK_BIAS = -1e9

_U_CQ = 0
_U_CKV = 256
_U_KPE = 384
_U_QN = 512
_U_KCV = 1024
_U_KSLC = 1280
_U_VSLC = 1408
_U_KWIN = 1536
_U_VWIN = 1664
_U_GATE = 1792
_U_COLS = 1920
_N_TABS = 7


def _dot(a, b):
    return jnp.dot(a, b, preferred_element_type=jnp.float32)


def _dot_nt(a, b):
    return lax.dot_general(a, b, (((1,), (1,)), ((), ())), preferred_element_type=jnp.float32)


def _rms(x, g):
    return x * lax.rsqrt(jnp.mean(x * x, axis=-1, keepdims=True) + NORM_EPS) * g


def _lane(shape):
    return lax.broadcasted_iota(jnp.int32, shape, len(shape) - 1)


def _swap_pairs(x, lo_end, width):
    n = x.shape[-1]
    lane = _lane(x.shape) % LANES
    return jnp.where(lane < lo_end, pltpu.roll(x, n - width, axis=1), pltpu.roll(x, width, axis=1))


def _nsa_rope(x, cn, sn):
    half = NSA_ROPE_DIM // 2
    lane = _lane(x.shape) % HEAD_DIM
    sw = jnp.where(lane < half, pltpu.roll(x, LANES - half, axis=1), pltpu.roll(x, half, axis=1))
    return x * cn + sw * sn


def _proj_kernel(x_ref, tab_ref, gmix_ref, win_ref, gcq_ref, wq_ref, gckv_ref, wk_ref, wv_ref,
                 qm_ref, km_ref, vm_ref, qn_ref, ksa_ref, vs4_ref, kwp_ref, vw4_ref, kcv_ref, gate_ref):
    x = x_ref[0]
    n = _rms(x, gmix_ref[...]).astype(MXU_DTYPE)
    u = _dot(n, win_ref[...])

    def tab(t):
        return tab_ref[:, t * LANES:(t + 1) * LANES]

    cq_t, sq_t, ck_t, sk_t, cn_t, sn_t, eneg_t = (tab(t) for t in range(_N_TABS))
    lane = _lane((x.shape[0], LANES))
    low = lane < HEAD_DIM

    cqn = _rms(u[:, _U_CQ:_U_CQ + MLA_Q_RANK], gcq_ref[...]).astype(MXU_DTYPE)
    qm = _dot(cqn, wq_ref[...])
    rope_end = MLA_NOPE_DIM + MLA_ROPE_DIM // 2
    for h in range(MLA_HEADS):
        qh = qm[:, h * LANES:(h + 1) * LANES]
        qsw = _swap_pairs(qh, rope_end, MLA_ROPE_DIM // 2)
        qm_ref[0, :, h * LANES:(h + 1) * LANES] = (qh * cq_t + qsw * sq_t).astype(qm_ref.dtype)

    ckvn = _rms(u[:, _U_CKV:_U_CKV + MLA_KV_RANK], gckv_ref[...]).astype(MXU_DTYPE)
    km = _dot(ckvn, wk_ref[...])
    kpe = u[:, _U_KPE:_U_KPE + LANES]
    kpe = kpe * ck_t + _swap_pairs(kpe, rope_end, MLA_ROPE_DIM // 2) * sk_t
    for h in range(MLA_HEADS):
        km_ref[0, :, h * LANES:(h + 1) * LANES] = (km[:, h * LANES:(h + 1) * LANES] + kpe).astype(km_ref.dtype)
    vm_ref[0] = _dot(ckvn, wv_ref[...]).astype(vm_ref.dtype)

    nsa_scale = HEAD_DIM ** -0.5 * LOG2E
    for c in range(NSA_WIDTH // LANES):
        qc = u[:, _U_QN + c * LANES:_U_QN + (c + 1) * LANES]
        qn_ref[0, :, c * LANES:(c + 1) * LANES] = (_nsa_rope(qc, cn_t, sn_t) * nsa_scale).astype(qn_ref.dtype)

    ks = _nsa_rope(u[:, _U_KSLC:_U_KSLC + LANES], cn_t, sn_t)
    ks_r = pltpu.roll(ks, HEAD_DIM, axis=1)
    ksa_ref[0, :, 0:LANES] = jnp.where(low, ks, eneg_t).astype(ksa_ref.dtype)
    ksa_ref[0, :, LANES:2 * LANES] = jnp.where(low, ks_r, eneg_t).astype(ksa_ref.dtype)

    kw = _nsa_rope(u[:, _U_KWIN:_U_KWIN + LANES], cn_t, sn_t)
    kw_r = pltpu.roll(kw, HEAD_DIM, axis=1)
    kwp_ref[0, :, 0:LANES] = jnp.where(low, kw, 0.0).astype(kwp_ref.dtype)
    kwp_ref[0, :, LANES:2 * LANES] = jnp.where(low, kw_r, 0.0).astype(kwp_ref.dtype)

    def four_way(v, ref):
        v_r = pltpu.roll(v, HEAD_DIM, axis=1)
        ref[0, :, 0 * LANES:1 * LANES] = jnp.where(low, v, 0.0).astype(ref.dtype)
        ref[0, :, 1 * LANES:2 * LANES] = jnp.where(low, 0.0, v_r).astype(ref.dtype)
        ref[0, :, 2 * LANES:3 * LANES] = jnp.where(low, v_r, 0.0).astype(ref.dtype)
        ref[0, :, 3 * LANES:4 * LANES] = jnp.where(low, 0.0, v).astype(ref.dtype)

    four_way(u[:, _U_VSLC:_U_VSLC + LANES], vs4_ref)
    four_way(u[:, _U_VWIN:_U_VWIN + LANES], vw4_ref)

    kcv_ref[0] = u[:, _U_KCV:_U_KCV + 2 * LANES]
    gate_ref[0] = u[:, _U_GATE:_U_GATE + LANES]


def _proj_call(x, tabs, gmix, win, gcq, wq, gckv, wk, wv):
    B, S, D = x.shape
    rows = PROJ_ROWS
    ns = S // rows
    tok = lambda w: pl.BlockSpec((1, rows, w), lambda i, b: (b, i, 0))
    full = lambda a: pl.BlockSpec(a.shape, lambda i, b: (0,) * a.ndim)
    bf = MXU_DTYPE
    out_shapes = [
        jax.ShapeDtypeStruct((B, S, MLA_HEADS * LANES), bf),
        jax.ShapeDtypeStruct((B, S, MLA_HEADS * LANES), bf),
        jax.ShapeDtypeStruct((B, S, MLA_HEADS * LANES), bf),
        jax.ShapeDtypeStruct((B, S, NSA_WIDTH), bf),
        jax.ShapeDtypeStruct((B, S, 2 * LANES), bf),
        jax.ShapeDtypeStruct((B, S, 4 * LANES), bf),
        jax.ShapeDtypeStruct((B, S, 2 * LANES), bf),
        jax.ShapeDtypeStruct((B, S, 4 * LANES), bf),
        jax.ShapeDtypeStruct((B, S, 2 * LANES), jnp.float32),
        jax.ShapeDtypeStruct((B, S, LANES), jnp.float32),
    ]
    return pl.pallas_call(
        _proj_kernel,
        grid=(ns, B),
        in_specs=[tok(D), pl.BlockSpec((rows, _N_TABS * LANES), lambda i, b: (i, 0)),
                  full(gmix), full(win), full(gcq), full(wq), full(gckv), full(wk), full(wv)],
        out_specs=[tok(s.shape[-1]) for s in out_shapes],
        out_shape=out_shapes,
        compiler_params=pltpu.CompilerParams(
            dimension_semantics=("arbitrary", "arbitrary"), vmem_limit_bytes=VMEM_LIMIT),
        name="proj",
    )(x, tabs, gmix, win, gcq, wq, gckv, wk, wv)


def _gelu_tanh(x):
    return 0.5 * x * (1.0 + jnp.tanh(math.sqrt(2.0 / math.pi) * (x + 0.044715 * (x * x * x))))


def _compress_kernel(x_ref, pea_ref, peb_ref, wa_ref, wb_ref, b1_ref, w2k_ref, b2k_ref,
                     w2vl_ref, b2vl_ref, w2vh_ref, b2vh_ref, ctab_ref, kc_ref, vc_ref):
    x = x_ref[0]
    nch = x.shape[0]
    a = _dot((x + pea_ref[...]).astype(MXU_DTYPE), wa_ref[...])
    b = _dot((x + peb_ref[...]).astype(MXU_DTYPE), wb_ref[...])
    hid = _gelu_tanh(a + pltpu.roll(b, nch - 1, axis=0) + b1_ref[...]).astype(MXU_DTYPE)
    cc = ctab_ref[:, 0:LANES]
    sc = ctab_ref[:, LANES:2 * LANES]
    for g in range(NSA_KV_GROUPS):
        hk = hid[:, g * LANES:(g + 1) * LANES]
        kc = _dot(hk, w2k_ref[...]) + b2k_ref[...]
        kc_ref[0, :, g * LANES:(g + 1) * LANES] = _nsa_rope(kc, cc, sc).astype(kc_ref.dtype)
        hv = hid[:, (NSA_KV_GROUPS + g) * LANES:(NSA_KV_GROUPS + g + 1) * LANES]
        vc_ref[0, :, (2 * g) * LANES:(2 * g + 1) * LANES] = (
            _dot(hv, w2vl_ref[...]) + b2vl_ref[...]).astype(vc_ref.dtype)
        vc_ref[0, :, (2 * g + 1) * LANES:(2 * g + 2) * LANES] = (
            _dot(hv, w2vh_ref[...]) + b2vh_ref[...]).astype(vc_ref.dtype)


def _compress_call(kcv_chunks, consts, ctab):
    B, nch, width = kcv_chunks.shape
    full = lambda a: pl.BlockSpec(a.shape, lambda b: (0,) * a.ndim)
    return pl.pallas_call(
        _compress_kernel,
        grid=(B,),
        in_specs=[pl.BlockSpec((1, nch, width), lambda b: (b, 0, 0))] + [full(c) for c in consts] + [full(ctab)],
        out_specs=[pl.BlockSpec((1, nch, 2 * LANES), lambda b: (b, 0, 0)),
                   pl.BlockSpec((1, nch, 4 * LANES), lambda b: (b, 0, 0))],
        out_shape=[jax.ShapeDtypeStruct((B, nch, 2 * LANES), MXU_DTYPE),
                   jax.ShapeDtypeStruct((B, nch, 4 * LANES), MXU_DTYPE)],
        compiler_params=pltpu.CompilerParams(
            dimension_semantics=("arbitrary",), vmem_limit_bytes=VMEM_LIMIT),
        name="compress",
    )(kcv_chunks, *consts, ctab)


def _first_tile(s):
    m = jnp.max(s, axis=-1, keepdims=True)
    p = jnp.exp2(s - m)
    return m, jnp.sum(p, axis=-1, keepdims=True), p


def _next_tile(s, m, l):
    m_new = jnp.maximum(m, jnp.max(s, axis=-1, keepdims=True))
    alpha = jnp.exp2(m - m_new)
    p = jnp.exp2(s - m_new)
    return m_new, alpha * l + jnp.sum(p, axis=-1, keepdims=True), p, alpha


def _flash_streams(n_streams, scores, pv, lower, i, tq, wide):
    start_d = pl.multiple_of(i * tq, tq)
    carry = []
    for st in range(n_streams):
        m, l, p = _first_tile(jnp.where(lower, scores(st, start_d, tq), -jnp.inf))
        carry.append((m, l, pv(st, p, start_d, tq)))

    def step(start, width, carry):
        out = []
        for st, (m, l, acc) in enumerate(carry):
            m, l, p, alpha = _next_tile(scores(st, start, width), m, l)
            out.append((m, l, alpha * acc + pv(st, p, start, width)))
        return tuple(out)

    wt = wide * tq
    n_wide = i // wide
    carry = lax.fori_loop(0, n_wide, lambda jj, c: step(pl.multiple_of(jj * wt, wt), wt, c), tuple(carry))
    carry = lax.fori_loop(n_wide * wide, i, lambda jj, c: step(pl.multiple_of(jj * tq, tq), tq, c), carry)
    return [acc * (1.0 / l) for (m, l, acc) in carry]


def _mla_kernel(q_ref, k_ref, v_ref, o_ref):
    i = pl.program_id(2)
    tq = q_ref.shape[1]
    lower = lax.broadcasted_iota(jnp.int32, (tq, tq), 1) <= lax.broadcasted_iota(jnp.int32, (tq, tq), 0)
    cols = [slice(hh * LANES, (hh + 1) * LANES) for hh in range(2)]
    qs = [q_ref[0, :, c] for c in cols]

    def scores(hh, start, width):
        return _dot_nt(qs[hh], k_ref[0, pl.ds(start, width), cols[hh]])

    def pv(hh, p, start, width):
        return _dot(p.astype(MXU_DTYPE), v_ref[0, pl.ds(start, width), cols[hh]])

    o0, o1 = _flash_streams(2, scores, pv, lower, i, tq, MLA_WIDE)
    o_ref[0] = o0 + o1


def _mla_call(qm, km, vm):
    B, S, _ = qm.shape
    tq = MLA_TQ
    pairs = MLA_HEADS // 2
    return pl.pallas_call(
        _mla_kernel,
        grid=(B, pairs, S // tq),
        in_specs=[pl.BlockSpec((1, tq, 2 * LANES), lambda b, p, i: (b, i, p)),
                  pl.BlockSpec((1, S, 2 * LANES), lambda b, p, i: (b, 0, p)),
                  pl.BlockSpec((1, S, 2 * LANES), lambda b, p, i: (b, 0, p))],
        out_specs=pl.BlockSpec((1, tq, LANES), lambda b, p, i: (b, i, p)),
        out_shape=jax.ShapeDtypeStruct((B, S, MLA_WIDTH), jnp.float32),
        compiler_params=pltpu.CompilerParams(
            dimension_semantics=("arbitrary", "arbitrary", "arbitrary"), vmem_limit_bytes=VMEM_LIMIT),
        name="mla",
    )(qm, km, vm)


def _nsa_cmp_kernel(q_ref, kc_ref, vc_ref, ovt_ref, oc_ref, qaug_ref, score_ref):
    i = pl.program_id(1)
    tq = q_ref.shape[1]
    nch = kc_ref.shape[1]
    nblk = score_ref.shape[1]
    rows4 = NSA_HPG * tq
    half = rows4 // 2
    q0 = i * tq
    qpos_c = q0 + lax.broadcasted_iota(jnp.int32, (rows4, nch), 0) % tq
    cmp_end = lax.broadcasted_iota(jnp.int32, (rows4, nch), 1) * CMP_STRIDE + (CMP_BLOCK - 1)
    mask_c = cmp_end <= qpos_c
    low = _lane((tq, LANES)) < HEAD_DIM

    blk = lax.broadcasted_iota(jnp.int32, (nblk, tq), 0)
    qpos_r = q0 + lax.broadcasted_iota(jnp.int32, (nblk, tq), 1)
    cur = qpos_r // SLC_BLOCK
    forced = (blk == 0) | (blk == cur) | (blk == cur - 1)
    causal = blk * SLC_BLOCK <= qpos_r
    groups = range(NSA_KV_GROUPS)

    q_heads = []
    scores = []
    for g in groups:
        pairs = [q_ref[0, :, (2 * g + p) * LANES:(2 * g + p + 1) * LANES].astype(jnp.float32) for p in range(2)]
        heads = ([jnp.where(low, qp, 0.0) for qp in pairs]
                 + [jnp.where(low, pltpu.roll(qp, HEAD_DIM, axis=1), 0.0) for qp in pairs])
        q_heads.append(heads)
        qs = jnp.concatenate(heads, axis=0).astype(MXU_DTYPE)
        s = jnp.where(mask_c, _dot_nt(qs, kc_ref[0, :, g * LANES:(g + 1) * LANES]), -jnp.inf)
        m = jnp.max(s, axis=-1, keepdims=True)
        m = jnp.where(m == -jnp.inf, 0.0, m)
        e = jnp.exp2(s - m)
        pr = e / jnp.maximum(jnp.sum(e, axis=-1, keepdims=True), 1e-30)
        pb = pr.astype(MXU_DTYPE)
        o = jnp.concatenate([_dot(pb[:half], vc_ref[0, :, (2 * g) * LANES:(2 * g + 1) * LANES]),
                             _dot(pb[half:], vc_ref[0, :, (2 * g + 1) * LANES:(2 * g + 2) * LANES])], axis=0)
        for p in range(2):
            oc_ref[0, :, (2 * g + p) * LANES:(2 * g + p + 1) * LANES] = (
                o[p * tq:(p + 1) * tq] + o[half + p * tq:half + (p + 1) * tq])

        psum = (pr[0:tq] + pr[tq:2 * tq]) + (pr[2 * tq:3 * tq] + pr[3 * tq:4 * tq])
        p_hi = psum.astype(MXU_DTYPE)
        p_lo = (psum - p_hi.astype(jnp.float32)).astype(MXU_DTYPE)
        imp_t = _dot_nt(ovt_ref[...], p_hi) + _dot_nt(ovt_ref[...], p_lo)
        score = jnp.where(forced, FORCE_SCORE, jnp.where(causal, imp_t[nblk:2 * nblk, :], -FORCE_SCORE))
        score_ref[g] = score
        scores.append(score)

    def count(c, cnts):
        out = list(cnts)
        for t in range(RANK_UNROLL):
            r = c * RANK_UNROLL + t
            lower_idx = r < blk
            for g in groups:
                row = score_ref[g, pl.ds(r, 1), :]
                beats = (row > scores[g]) | ((row == scores[g]) & lower_idx)
                out[g] = out[g] + jnp.where(beats, 1.0, 0.0)
        return tuple(out)

    zero = jnp.zeros((nblk, tq), jnp.float32)
    cnts = lax.fori_loop(0, i + 1, count, (zero,) * NSA_KV_GROUPS)
    for g in groups:
        notsel_t = jnp.where(cnts[g] < float(SLC_TOPK), 0.0, 1.0)
        notsel = jnp.concatenate([zero, notsel_t], axis=0).T
        for hq, qh in zip((0, 2, 1, 3), q_heads[g]):
            head = g * NSA_HPG + hq
            qaug_ref[0, :, head * LANES:(head + 1) * LANES] = (qh + notsel).astype(qaug_ref.dtype)


def _nsa_cmp_call(qn, kc, vc, ovt):
    B, S, _ = qn.shape
    tq = NSA_TQ
    nch = kc.shape[1]
    return pl.pallas_call(
        _nsa_cmp_kernel,
        grid=(B, S // tq),
        in_specs=[pl.BlockSpec((1, tq, NSA_WIDTH), lambda b, i: (b, i, 0)),
                  pl.BlockSpec((1, nch, 2 * LANES), lambda b, i: (b, 0, 0)),
                  pl.BlockSpec((1, nch, 4 * LANES), lambda b, i: (b, 0, 0)),
                  pl.BlockSpec(ovt.shape, lambda b, i: (0, 0))],
        out_specs=[pl.BlockSpec((1, tq, NSA_WIDTH), lambda b, i: (b, i, 0)),
                   pl.BlockSpec((1, tq, NSA_HEADS * LANES), lambda b, i: (b, i, 0))],
        out_shape=[jax.ShapeDtypeStruct((B, S, NSA_WIDTH), jnp.float32),
                   jax.ShapeDtypeStruct((B, S, NSA_HEADS * LANES), MXU_DTYPE)],
        scratch_shapes=[pltpu.VMEM((NSA_KV_GROUPS, LANES // 2, tq), jnp.float32)],
        compiler_params=pltpu.CompilerParams(
            dimension_semantics=("arbitrary", "arbitrary"), vmem_limit_bytes=VMEM_LIMIT),
        name="nsa_cmp",
    )(qn, kc, vc, ovt)


def _nsa_attn_kernel(qaug_ref, ksa_ref, vs4_ref, kwp_ref, vw4_ref, oc_ref, gate_ref, y_ref):
    i = pl.program_id(1)
    tq = qaug_ref.shape[1]
    rows4 = NSA_HPG * tq
    half = rows4 // 2
    wwin = WINDOW + tq
    ql = lax.broadcasted_iota(jnp.int32, (rows4, tq), 0) % tq
    lower = lax.broadcasted_iota(jnp.int32, (rows4, tq), 1) <= ql
    sig = jax.nn.sigmoid(gate_ref[0])
    low = _lane((tq, LANES)) < HEAD_DIM
    head_order = (0, 2, 1, 3)
    groups = range(NSA_KV_GROUPS)
    qs = [jnp.concatenate(
        [qaug_ref[0, :, (g * NSA_HPG + h) * LANES:(g * NSA_HPG + h + 1) * LANES] for h in head_order], axis=0)
        for g in groups]

    def pv_from(v_ref):
        def pv(g, p, start, width):
            pb = p.astype(MXU_DTYPE)
            rows = pl.ds(start, width)
            return jnp.concatenate(
                [_dot(pb[:half], v_ref[0, rows, (2 * g) * LANES:(2 * g + 1) * LANES]),
                 _dot(pb[half:], v_ref[0, rows, (2 * g + 1) * LANES:(2 * g + 2) * LANES])], axis=0)
        return pv

    o_slc = _flash_streams(
        NSA_KV_GROUPS,
        lambda g, start, width: _dot_nt(qs[g], ksa_ref[0, pl.ds(start, width), g * LANES:(g + 1) * LANES]),
        pv_from(vs4_ref), lower, i, tq, NSA_WIDE)

    wstart = pl.multiple_of(jnp.maximum(i * tq - WINDOW, 0), tq)
    qpos = i * tq + lax.broadcasted_iota(jnp.int32, (rows4, wwin), 0) % tq
    dist = qpos - (wstart + lax.broadcasted_iota(jnp.int32, (rows4, wwin), 1))
    band = (dist >= 0) & (dist < WINDOW)
    pv_w = pv_from(vw4_ref)
    o_win = []
    for g in groups:
        s = _dot_nt(qs[g], kwp_ref[0, pl.ds(wstart, wwin), g * LANES:(g + 1) * LANES])
        m, l, p = _first_tile(jnp.where(band, s, -jnp.inf))
        o_win.append(pv_w(g, p, wstart, wwin) * (1.0 / l))

    for g in groups:
        for p in range(NSA_HPG // 2):
            pair = g * (NSA_HPG // 2) + p
            r0 = slice(p * tq, (p + 1) * tq)
            r1 = slice(half + p * tq, half + (p + 1) * tq)

            def gate(br):
                c = br * NSA_HEADS + 2 * pair
                return jnp.where(low, sig[:, c:c + 1], sig[:, c + 1:c + 2])

            o_c = oc_ref[0, :, pair * LANES:(pair + 1) * LANES]
            y_ref[0, :, pair * LANES:(pair + 1) * LANES] = (
                gate(0) * o_c + gate(1) * (o_slc[g][r0] + o_slc[g][r1]) + gate(2) * (o_win[g][r0] + o_win[g][r1]))


def _nsa_attn_call(qaug, ksa, vs4, kwp, vw4, oc, gate):
    B, S, _ = qaug.shape
    tq = NSA_ATTN_TQ
    tile = lambda w: pl.BlockSpec((1, tq, w), lambda b, i: (b, i, 0))
    seq = lambda w: pl.BlockSpec((1, S, w), lambda b, i: (b, 0, 0))
    return pl.pallas_call(
        _nsa_attn_kernel,
        grid=(B, S // tq),
        in_specs=[tile(NSA_HEADS * LANES), seq(2 * LANES), seq(4 * LANES), seq(2 * LANES), seq(4 * LANES),
                  tile(NSA_WIDTH), tile(LANES)],
        out_specs=tile(NSA_WIDTH),
        out_shape=jax.ShapeDtypeStruct((B, S, NSA_WIDTH), jnp.float32),
        compiler_params=pltpu.CompilerParams(
            dimension_semantics=("arbitrary", "arbitrary"), vmem_limit_bytes=VMEM_LIMIT),
        name="nsa_attn",
    )(qaug, ksa, vs4, kwp, vw4, oc, gate)


def _out_kernel(x_ref, ym_ref, yn_ref, gm_ref, gn_ref, wo_ref, gmlp_ref, wup_ref, wdn_ref, gfin_ref, o_ref):
    mixed = jnp.concatenate([_rms(ym_ref[...], gm_ref[...]), _rms(yn_ref[...], gn_ref[...])], axis=1)
    h = x_ref[...] + _dot(mixed.astype(MXU_DTYPE), wo_ref[...])
    hn = _rms(h, gmlp_ref[...]).astype(MXU_DTYPE)
    acc = h
    for c in range(D_FF // FF_CHUNK):
        a = jnp.maximum(_dot(hn, wup_ref[:, c * FF_CHUNK:(c + 1) * FF_CHUNK]), 0.0)
        acc = acc + _dot((a * a).astype(MXU_DTYPE), wdn_ref[c * FF_CHUNK:(c + 1) * FF_CHUNK, :])
    o_ref[...] = _rms(acc, gfin_ref[...])


def _out_call(x2, ym2, yn2, gm, gn, wo, gmlp, wup, wdn, gfin):
    T, D = x2.shape
    rows = OUT_ROWS
    tile = lambda w: pl.BlockSpec((rows, w), lambda i: (i, 0))
    full = lambda a: pl.BlockSpec(a.shape, lambda i: (0,) * a.ndim)
    return pl.pallas_call(
        _out_kernel,
        grid=(T // rows,),
        in_specs=[tile(D), tile(MLA_WIDTH), tile(NSA_WIDTH), full(gm), full(gn), full(wo), full(gmlp),
                  full(wup), full(wdn), full(gfin)],
        out_specs=tile(D),
        out_shape=jax.ShapeDtypeStruct((T, D), jnp.float32),
        compiler_params=pltpu.CompilerParams(
            dimension_semantics=("arbitrary",), vmem_limit_bytes=VMEM_LIMIT),
        name="out",
    )(x2, ym2, yn2, gm, gn, wo, gmlp, wup, wdn, gfin)


def _rope_cs(pos, dim):
    inv_freq = jnp.exp(-math.log(ROPE_THETA) * jnp.arange(0, dim, 2, dtype=jnp.float32) / dim)
    ang = pos.astype(jnp.float32)[:, None] * inv_freq[None, :]
    return jnp.cos(ang), jnp.sin(ang)


def _nsa_rope_tables(pos, width):
    c, s = _rope_cs(pos, NSA_ROPE_DIM)
    n = pos.shape[0]
    rest = HEAD_DIM - NSA_ROPE_DIM
    ch = jnp.concatenate([c, c, jnp.ones((n, rest), jnp.float32)], axis=1)
    sh = jnp.concatenate([-s, s, jnp.zeros((n, rest), jnp.float32)], axis=1)
    reps = width // HEAD_DIM
    return jnp.tile(ch, (1, reps)), jnp.tile(sh, (1, reps))


def _proj_tables(S):
    pos = jnp.arange(S)
    c, s = _rope_cs(pos, MLA_ROPE_DIM)
    z = lambda w: jnp.zeros((S, w), jnp.float32)
    scale = (MLA_NOPE_DIM + MLA_ROPE_DIM) ** -0.5 * LOG2E
    pad = LANES - MLA_NOPE_DIM - MLA_ROPE_DIM
    cq = jnp.concatenate([jnp.ones((S, MLA_NOPE_DIM), jnp.float32), c, c, z(pad)], axis=1) * scale
    sq = jnp.concatenate([z(MLA_NOPE_DIM), -s, s, z(pad)], axis=1) * scale
    ck = jnp.concatenate([z(MLA_NOPE_DIM), c, c, z(pad)], axis=1)
    sk = jnp.concatenate([z(MLA_NOPE_DIM), -s, s, z(pad)], axis=1)
    cn, sn = _nsa_rope_tables(pos, LANES)
    onehot = (pos[:, None] // SLC_BLOCK == jnp.arange(LANES - HEAD_DIM)[None, :]).astype(jnp.float32)
    eneg = jnp.concatenate([z(HEAD_DIM), MASK_BIAS * onehot], axis=1)
    return jnp.concatenate([cq, sq, ck, sk, cn, sn, eneg], axis=1)


def _compress_tables(nch):
    end = jnp.arange(nch) * CMP_STRIDE + CMP_BLOCK - 1
    c, s = _nsa_rope_tables(end, HEAD_DIM)
    z = jnp.zeros((nch, LANES - HEAD_DIM), jnp.float32)
    return jnp.concatenate([c, z, s, z], axis=1)


def _overlap_t(nch, nblk):
    cs = jnp.arange(nch)[None, :] * CMP_STRIDE
    ss = jnp.arange(nblk)[:, None] * SLC_BLOCK
    ov = jnp.clip(jnp.minimum(cs + CMP_BLOCK, ss + SLC_BLOCK) - jnp.maximum(cs, ss), 0, None)
    ov = ov.astype(jnp.float32) / CMP_BLOCK
    ov = jnp.where(jnp.arange(nch)[None, :] < nch - 1, ov, 0.0)
    return jnp.concatenate([jnp.zeros_like(ov), ov], axis=0).astype(MXU_DTYPE)


def _pad_cols(w, width):
    return jnp.pad(w, ((0, 0), (0, width - w.shape[1])))


def _compress_consts(pe_k, w1_k, b1_k, w2_k, b2_k, pe_v, w1_v, b1_v, w2_v, b2_v):
    half = CMP_BLOCK // 2
    ncol = 2 * NSA_KV_GROUPS
    w1 = jnp.stack([w1_k, w1_k, w1_v, w1_v]).reshape(ncol, 2, half, HEAD_DIM, CMP_HIDDEN)
    eye = jnp.eye(ncol, dtype=w1.dtype)

    def spread(wh):
        return jnp.einsum('cldh,ce->lcdeh', wh, eye).reshape(half * ncol * HEAD_DIM, ncol * CMP_HIDDEN)

    pe = jnp.stack([pe_k, pe_k, pe_v, pe_v]).reshape(ncol, 2, half, HEAD_DIM)
    pe_row = lambda ph: ph.transpose(1, 0, 2).reshape(1, half * ncol * HEAD_DIM)
    b1 = jnp.concatenate([b1_k, b1_k, b1_v, b1_v]).reshape(1, ncol * CMP_HIDDEN)
    zc = jnp.zeros((CMP_HIDDEN, HEAD_DIM), w2_v.dtype)
    zb = jnp.zeros((HEAD_DIM,), b2_v.dtype)
    return [
        pe_row(pe[:, 0]), pe_row(pe[:, 1]),
        spread(w1[:, 0]).astype(MXU_DTYPE), spread(w1[:, 1]).astype(MXU_DTYPE), b1,
        _pad_cols(w2_k, LANES).astype(MXU_DTYPE), _pad_cols(b2_k[None, :], LANES),
        jnp.concatenate([w2_v, zc], axis=1).astype(MXU_DTYPE), jnp.concatenate([b2_v, zb])[None, :],
        jnp.concatenate([zc, w2_v], axis=1).astype(MXU_DTYPE), jnp.concatenate([zb, b2_v])[None, :],
    ]


def _proj_weights(w_in, w_uq, w_ukv):
    d = w_in.shape[0]
    z = lambda w: jnp.zeros((d, w), w_in.dtype)
    o_kr = MLA_Q_RANK + MLA_KV_RANK
    o_qn = o_kr + MLA_ROPE_DIM
    o_gate = o_qn + NSA_WIDTH + 6 * NSA_KV_WIDTH
    n_gate = 3 * NSA_HEADS
    win = jnp.concatenate([
        w_in[:, :o_kr],
        z(MLA_NOPE_DIM), w_in[:, o_kr:o_qn], z(LANES - MLA_NOPE_DIM - MLA_ROPE_DIM),
        w_in[:, o_qn:o_gate],
        w_in[:, o_gate:], z(LANES - n_gate)], axis=1)
    assert win.shape[1] == _U_COLS
    qd = MLA_NOPE_DIM + MLA_ROPE_DIM
    wq = jnp.pad(w_uq.reshape(MLA_Q_RANK, MLA_HEADS, qd), ((0, 0), (0, 0), (0, LANES - qd)))
    wq = wq.reshape(MLA_Q_RANK, MLA_HEADS * LANES)
    kv = w_ukv.reshape(MLA_KV_RANK, MLA_HEADS, MLA_NOPE_DIM + MLA_V_DIM)
    zk = jnp.zeros((MLA_KV_RANK, MLA_HEADS, LANES - MLA_NOPE_DIM), w_ukv.dtype)
    wk = jnp.concatenate([kv[..., :MLA_NOPE_DIM], zk], axis=-1).reshape(MLA_KV_RANK, MLA_HEADS * LANES)
    v = kv[..., MLA_NOPE_DIM:]
    zv = jnp.zeros_like(v)
    even = (jnp.arange(MLA_HEADS) % 2 == 0)[None, :, None]
    wv = jnp.concatenate([jnp.where(even, v, zv), jnp.where(even, zv, v)], axis=-1)
    wv = wv.reshape(MLA_KV_RANK, MLA_HEADS * LANES)
    return win.astype(MXU_DTYPE), wq.astype(MXU_DTYPE), wk.astype(MXU_DTYPE), wv.astype(MXU_DTYPE)


def kernel(x, g_mix_norm, w_in, g_cq, w_uq, g_ckv, w_ukv, cmp_pe_k, cmp_w1_k, cmp_b1_k, cmp_w2_k, cmp_b2_k,
           cmp_pe_v, cmp_w1_v, cmp_b1_v, cmp_w2_v, cmp_b2_v, g_out_mla, g_out_nsa, w_o, g_mlp_norm, w_up,
           w_down, g_final):
    B, S, D = x.shape
    depth = w_in.shape[0]
    nch = S // CMP_STRIDE
    nblk = S // SLC_BLOCK
    assert D == D_MODEL and S % MLA_TQ == 0 and S % PROJ_ROWS == 0 and (B * S) % OUT_ROWS == 0
    assert S % NSA_TQ == 0 and S % NSA_ATTN_TQ == 0 and WINDOW % NSA_ATTN_TQ == 0 and S >= WINDOW + NSA_ATTN_TQ
    assert SLC_TOPK <= nblk <= LANES - HEAD_DIM
    row = lambda g: g.reshape(1, -1)

    tabs = _proj_tables(S)
    ctab = _compress_tables(nch)
    ovt = _overlap_t(nch, LANES // 2)

    assert depth == 1

    win, wq, wk, wv = _proj_weights(w_in[0], w_uq[0], w_ukv[0])
    qm, km, vm, qn, ksa, vs4, kwp, vw4, kcv, gate = _proj_call(
        x, tabs, row(g_mix_norm[0]), win, row(g_cq[0]), wq, row(g_ckv[0]), wk, wv)
    consts = _compress_consts(cmp_pe_k[0], cmp_w1_k[0], cmp_b1_k[0], cmp_w2_k[0], cmp_b2_k[0],
                              cmp_pe_v[0], cmp_w1_v[0], cmp_b1_v[0], cmp_w2_v[0], cmp_b2_v[0])
    kc, vc = _compress_call(kcv.reshape(B, nch, CMP_STRIDE * 2 * LANES), consts, ctab)
    y_mla = _mla_call(qm, km, vm)
    oc, qaug = _nsa_cmp_call(qn, kc, vc, ovt)
    y_nsa = _nsa_attn_call(qaug, ksa, vs4, kwp, vw4, oc, gate)
    out = _out_call(x.reshape(B * S, D), y_mla.reshape(B * S, MLA_WIDTH), y_nsa.reshape(B * S, NSA_WIDTH),
                    row(g_out_mla[0]), row(g_out_nsa[0]), w_o[0].astype(MXU_DTYPE), row(g_mlp_norm[0]),
                    w_up[0].astype(MXU_DTYPE), w_down[0].astype(MXU_DTYPE), row(g_final))
    return out.reshape(B, S, D)
```

```python
import math

import jax
import jax.numpy as jnp
from jax import lax
from jax.experimental import pallas as pl
from jax.experimental.pallas import tpu as pltpu

D_MODEL = 1024
HEAD_DIM = 64
ROPE_THETA = 500000.0
NORM_EPS = 1e-6

MLA_HEADS = 8
MLA_Q_RANK = 256
MLA_KV_RANK = 128
MLA_NOPE_DIM = 64
MLA_ROPE_DIM = 32
MLA_V_DIM = 64
MLA_WIDTH = MLA_HEADS * MLA_V_DIM

NSA_HEADS = 8
NSA_KV_GROUPS = 2
NSA_HPG = NSA_HEADS // NSA_KV_GROUPS
NSA_ROPE_DIM = HEAD_DIM // 4
NSA_WIDTH = NSA_HEADS * HEAD_DIM
NSA_KV_WIDTH = NSA_KV_GROUPS * HEAD_DIM
NSA_GATES = 3 * NSA_HEADS
CMP_BLOCK = 32
CMP_STRIDE = 16
CMP_HIDDEN = 2 * HEAD_DIM
SLC_BLOCK = 64
SLC_TOPK = 16
WINDOW = 512
FORCE_SCORE = 1e9
D_FF = 4 * D_MODEL

LANES = 128
SUBLANES = 8
VMEM_LIMIT = 56 * 1024 * 1024

MXU_DTYPE = jnp.bfloat16
PROJ_ROWS = 512
MLA_TQ = 1024
MLA_WIDE = 2
NSA_TQ = 256
NSA_WIDE = 4
RANK_UNROLL = NSA_TQ // SLC_BLOCK
OUT_ROWS = 512
FF_CHUNK = 1024
LOG2E = math.log2(math.e)
MASK_BIAS = -1e9
GATE_ROWS = 32
BF16_ROWS = 16
V_ROWS = HEAD_DIM + BF16_ROWS
EXP_DTYPE = jnp.float32

_U_CQ = 0
_U_CKV = 256
_U_KPE = 384
_U_KCV = 512
_U_KSLC = 768
_U_KWIN = 896
_U_COLS = 1024
_N_TABS = 5
_UT_QN = 0
_UT_VSLC = 512
_UT_VWIN = 640
_UT_GATE = 768
_UT_ROWS = 800
_TT_ROWS = MLA_ROPE_DIM + NSA_ROPE_DIM


def _dot(a, b):
    return jnp.dot(a, b, preferred_element_type=jnp.float32)


def _dot_nt(a, b):
    return lax.dot_general(a, b, (((1,), (1,)), ((), ())), preferred_element_type=jnp.float32)


def _dot_tn(a, b):
    return lax.dot_general(a, b, (((0,), (0,)), ((), ())), preferred_element_type=jnp.float32)


def _rms(x, g):
    return x * lax.rsqrt(jnp.mean(x * x, axis=-1, keepdims=True) + NORM_EPS) * g


def _lane(shape):
    return lax.broadcasted_iota(jnp.int32, shape, len(shape) - 1)


def _swap_pairs(x, lo_end, width):
    n = x.shape[-1]
    lane = _lane(x.shape) % LANES
    return jnp.where(lane < lo_end, pltpu.roll(x, n - width, axis=1), pltpu.roll(x, width, axis=1))


def _nsa_rope(x, cn, sn):
    half = NSA_ROPE_DIM // 2
    lane = _lane(x.shape) % HEAD_DIM
    sw = jnp.where(lane < half, pltpu.roll(x, LANES - half, axis=1), pltpu.roll(x, half, axis=1))
    return x * cn + sw * sn


def _rope_rows(x1, x2, c, s):
    return x1 * c - x2 * s, x2 * c + x1 * s


def _proj_kernel(x_ref, tab_ref, tabt_ref, gmix_ref, win_ref, wint_ref, gcq_ref, wqt_ref, gckv_ref, wk_ref, wvt_ref,
                 qmt_ref, km_ref, vmt_ref, qnt_ref, ksa_ref, vst_ref, kwp_ref, vwt_ref, kcv_ref, gatet_ref):
    x = x_ref[0]
    n = _rms(x, gmix_ref[...]).astype(MXU_DTYPE)
    u = _dot(n, win_ref[...])
    ut = _dot_nt(wint_ref[...], n)

    ck_t, sk_t, cn_t, sn_t, eneg_t = (tab_ref[:, t * LANES:(t + 1) * LANES] for t in range(_N_TABS))
    hm = MLA_ROPE_DIM // 2
    hn = NSA_ROPE_DIM // 2
    cm_r = tabt_ref[0:hm, :]
    sm_r = tabt_ref[hm:2 * hm, :]
    cn_r = tabt_ref[2 * hm:2 * hm + hn, :]
    sn_r = tabt_ref[2 * hm + hn:2 * hm + 2 * hn, :]
    low = _lane((x.shape[0], LANES)) < HEAD_DIM

    mla_scale = (MLA_NOPE_DIM + MLA_ROPE_DIM) ** -0.5 * LOG2E
    cqn = _rms(u[:, _U_CQ:_U_CQ + MLA_Q_RANK], gcq_ref[...]).astype(MXU_DTYPE)
    qmt = _dot_nt(wqt_ref[...], cqn)
    r0 = MLA_NOPE_DIM
    for h in range(MLA_HEADS):
        xh = qmt[h * LANES:(h + 1) * LANES]
        p1, p2 = _rope_rows(xh[r0:r0 + hm], xh[r0 + hm:r0 + 2 * hm], cm_r, sm_r)
        qh = jnp.concatenate([xh[0:r0], p1, p2, xh[r0 + 2 * hm:]], axis=0) * mla_scale
        qmt_ref[0, h * LANES:(h + 1) * LANES, :] = qh.astype(qmt_ref.dtype)

    ckvn = _rms(u[:, _U_CKV:_U_CKV + MLA_KV_RANK], gckv_ref[...]).astype(MXU_DTYPE)
    km = _dot(ckvn, wk_ref[...])
    kpe = u[:, _U_KPE:_U_KPE + LANES]
    kpe = kpe * ck_t + _swap_pairs(kpe, MLA_NOPE_DIM + hm, hm) * sk_t
    for h in range(MLA_HEADS):
        km_ref[0, :, h * LANES:(h + 1) * LANES] = (km[:, h * LANES:(h + 1) * LANES] + kpe).astype(km_ref.dtype)
    ones_tile = jnp.where(lax.broadcasted_iota(jnp.int32, (BF16_ROWS, x.shape[0]), 0) == 0, 1.0, 0.0)

    def put_values(ref, vt):
        for h in range(vt.shape[0] // HEAD_DIM):
            ref[0, h * V_ROWS:h * V_ROWS + HEAD_DIM, :] = vt[h * HEAD_DIM:(h + 1) * HEAD_DIM].astype(ref.dtype)
            ref[0, h * V_ROWS + HEAD_DIM:(h + 1) * V_ROWS, :] = ones_tile.astype(ref.dtype)

    put_values(vmt_ref, _dot_nt(wvt_ref[...], ckvn))

    nsa_scale = HEAD_DIM ** -0.5 * LOG2E
    for h in range(NSA_HEADS):
        xh = ut[_UT_QN + h * HEAD_DIM:_UT_QN + (h + 1) * HEAD_DIM]
        p1, p2 = _rope_rows(xh[0:hn], xh[hn:2 * hn], cn_r, sn_r)
        qh = jnp.concatenate([p1, p2, xh[2 * hn:]], axis=0) * nsa_scale
        qnt_ref[0, h * HEAD_DIM:(h + 1) * HEAD_DIM, :] = qh.astype(qnt_ref.dtype)

    ks = _nsa_rope(u[:, _U_KSLC:_U_KSLC + LANES], cn_t, sn_t)
    ks_r = pltpu.roll(ks, HEAD_DIM, axis=1)
    ksa_ref[0, :, 0:LANES] = jnp.where(low, ks, eneg_t).astype(ksa_ref.dtype)
    ksa_ref[0, :, LANES:2 * LANES] = jnp.where(low, ks_r, eneg_t).astype(ksa_ref.dtype)

    kw = _nsa_rope(u[:, _U_KWIN:_U_KWIN + LANES], cn_t, sn_t)
    kw_r = pltpu.roll(kw, HEAD_DIM, axis=1)
    kwp_ref[0, :, 0:LANES] = jnp.where(low, kw, 0.0).astype(kwp_ref.dtype)
    kwp_ref[0, :, LANES:2 * LANES] = jnp.where(low, kw_r, 0.0).astype(kwp_ref.dtype)

    put_values(vst_ref, ut[_UT_VSLC:_UT_VSLC + NSA_KV_WIDTH])
    put_values(vwt_ref, ut[_UT_VWIN:_UT_VWIN + NSA_KV_WIDTH])
    gatet_ref[0] = ut[_UT_GATE:_UT_GATE + GATE_ROWS]
    kcv_ref[0] = u[:, _U_KCV:_U_KCV + 2 * LANES]


def _proj_call(x, tabs, tabt, gmix, win, wint, gcq, wqt, gckv, wk, wvt):
    B, S, D = x.shape
    rows = PROJ_ROWS
    ns = S // rows
    tok = lambda w: pl.BlockSpec((1, rows, w), lambda i, b: (b, i, 0))
    feat = lambda r: pl.BlockSpec((1, r, rows), lambda i, b: (b, 0, i))
    full = lambda a: pl.BlockSpec(a.shape, lambda i, b: (0,) * a.ndim)
    bf = MXU_DTYPE
    outs = [
        (jax.ShapeDtypeStruct((B, MLA_HEADS * LANES, S), bf), feat(MLA_HEADS * LANES)),
        (jax.ShapeDtypeStruct((B, S, MLA_HEADS * LANES), bf), tok(MLA_HEADS * LANES)),
        (jax.ShapeDtypeStruct((B, MLA_HEADS * V_ROWS, S), bf), feat(MLA_HEADS * V_ROWS)),
        (jax.ShapeDtypeStruct((B, NSA_WIDTH, S), bf), feat(NSA_WIDTH)),
        (jax.ShapeDtypeStruct((B, S, 2 * LANES), bf), tok(2 * LANES)),
        (jax.ShapeDtypeStruct((B, NSA_KV_GROUPS * V_ROWS, S), bf), feat(NSA_KV_GROUPS * V_ROWS)),
        (jax.ShapeDtypeStruct((B, S, 2 * LANES), bf), tok(2 * LANES)),
        (jax.ShapeDtypeStruct((B, NSA_KV_GROUPS * V_ROWS, S), bf), feat(NSA_KV_GROUPS * V_ROWS)),
        (jax.ShapeDtypeStruct((B, S, 2 * LANES), jnp.float32), tok(2 * LANES)),
        (jax.ShapeDtypeStruct((B, GATE_ROWS, S), jnp.float32), feat(GATE_ROWS)),
    ]
    return pl.pallas_call(
        _proj_kernel,
        grid=(ns, B),
        in_specs=[tok(D), pl.BlockSpec((rows, _N_TABS * LANES), lambda i, b: (i, 0)),
                  pl.BlockSpec((_TT_ROWS, rows), lambda i, b: (0, i)),
                  full(gmix), full(win), full(wint), full(gcq), full(wqt), full(gckv), full(wk), full(wvt)],
        out_specs=[o[1] for o in outs],
        out_shape=[o[0] for o in outs],
        compiler_params=pltpu.CompilerParams(
            dimension_semantics=("arbitrary", "arbitrary"), vmem_limit_bytes=VMEM_LIMIT),
        name="proj",
    )(x, tabs, tabt, gmix, win, wint, gcq, wqt, gckv, wk, wvt)


def _gelu_tanh(x):
    return 0.5 * x * (1.0 + jnp.tanh(math.sqrt(2.0 / math.pi) * (x + 0.044715 * (x * x * x))))


def _compress_kernel(x_ref, pea_ref, peb_ref, wa_ref, wb_ref, b1_ref, w2k_ref, b2k_ref, w2vt_ref, b2vt_ref,
                     ctab_ref, kc_ref, vct_ref):
    x = x_ref[0]
    nch = x.shape[0]
    a = _dot((x + pea_ref[...]).astype(MXU_DTYPE), wa_ref[...])
    b = _dot((x + peb_ref[...]).astype(MXU_DTYPE), wb_ref[...])
    hid = _gelu_tanh(a + pltpu.roll(b, nch - 1, axis=0) + b1_ref[...]).astype(MXU_DTYPE)
    cc = ctab_ref[:, 0:LANES]
    sc = ctab_ref[:, LANES:2 * LANES]
    for g in range(NSA_KV_GROUPS):
        hk = hid[:, g * LANES:(g + 1) * LANES]
        kc = _dot(hk, w2k_ref[...]) + b2k_ref[...]
        kc_ref[0, :, g * LANES:(g + 1) * LANES] = _nsa_rope(kc, cc, sc).astype(kc_ref.dtype)
        hv = hid[:, (NSA_KV_GROUPS + g) * LANES:(NSA_KV_GROUPS + g + 1) * LANES]
        vct_ref[0, g * HEAD_DIM:(g + 1) * HEAD_DIM, :] = (
            _dot_nt(w2vt_ref[...], hv) + b2vt_ref[...]).astype(vct_ref.dtype)


def _compress_call(kcv_chunks, consts, ctab):
    B, nch, width = kcv_chunks.shape
    full = lambda a: pl.BlockSpec(a.shape, lambda b: (0,) * a.ndim)
    return pl.pallas_call(
        _compress_kernel,
        grid=(B,),
        in_specs=[pl.BlockSpec((1, nch, width), lambda b: (b, 0, 0))] + [full(c) for c in consts] + [full(ctab)],
        out_specs=[pl.BlockSpec((1, nch, 2 * LANES), lambda b: (b, 0, 0)),
                   pl.BlockSpec((1, NSA_KV_WIDTH, nch), lambda b: (b, 0, 0))],
        out_shape=[jax.ShapeDtypeStruct((B, nch, 2 * LANES), MXU_DTYPE),
                   jax.ShapeDtypeStruct((B, NSA_KV_WIDTH, nch), MXU_DTYPE)],
        compiler_params=pltpu.CompilerParams(
            dimension_semantics=("arbitrary",), vmem_limit_bytes=VMEM_LIMIT),
        name="compress",
    )(kcv_chunks, *consts, ctab)


def _numerators(s, m):
    return jnp.exp2((s - m).astype(EXP_DTYPE)).astype(MXU_DTYPE)


def _normalise(acc):
    return acc[0:HEAD_DIM] * (1.0 / acc[HEAD_DIM:HEAD_DIM + 1])


def _flash_streams(n_streams, scores, pv, lower, i, tq, wide):
    start_d = pl.multiple_of(i * tq, tq)
    carry = []
    for st in range(n_streams):
        s = jnp.where(lower, scores(st, start_d, tq), -jnp.inf)
        m = jnp.max(s, axis=0, keepdims=True)
        carry.append((m, pv(st, _numerators(s, m), start_d, tq)))

    def step(start, width, carry):
        out = []
        for st, (m, acc) in enumerate(carry):
            s = scores(st, start, width)
            m_new = jnp.maximum(m, jnp.max(s, axis=0, keepdims=True))
            out.append((m_new, jnp.exp2(m - m_new) * acc + pv(st, _numerators(s, m_new), start, width)))
        return tuple(out)

    wt = wide * tq
    n_wide = i // wide
    carry = lax.fori_loop(0, n_wide, lambda jj, c: step(pl.multiple_of(jj * wt, wt), wt, c), tuple(carry))
    carry = lax.fori_loop(n_wide * wide, i, lambda jj, c: step(pl.multiple_of(jj * tq, tq), tq, c), carry)
    return [_normalise(acc) for (m, acc) in carry]


def _mla_kernel(qt_ref, k_ref, vt_ref, o_ref):
    i = pl.program_id(2)
    tq = qt_ref.shape[2]
    lower = lax.broadcasted_iota(jnp.int32, (tq, tq), 0) <= lax.broadcasted_iota(jnp.int32, (tq, tq), 1)
    qs = [qt_ref[0, hh * LANES:(hh + 1) * LANES, :] for hh in range(2)]

    def scores(hh, start, width):
        return _dot(k_ref[0, pl.ds(start, width), hh * LANES:(hh + 1) * LANES], qs[hh])

    def pv(hh, p, start, width):
        return _dot(vt_ref[0, hh * V_ROWS:(hh + 1) * V_ROWS, pl.ds(start, width)], p)

    outs = _flash_streams(2, scores, pv, lower, i, tq, MLA_WIDE)
    for hh, o in enumerate(outs):
        o_ref[0, hh * MLA_V_DIM:(hh + 1) * MLA_V_DIM, :] = o


def _mla_call(qmt, km, vmt):
    B, S, _ = km.shape
    tq = MLA_TQ
    pairs = MLA_HEADS // 2
    return pl.pallas_call(
        _mla_kernel,
        grid=(B, pairs, S // tq),
        in_specs=[pl.BlockSpec((1, 2 * LANES, tq), lambda b, p, i: (b, p, i)),
                  pl.BlockSpec((1, S, 2 * LANES), lambda b, p, i: (b, 0, p)),
                  pl.BlockSpec((1, 2 * V_ROWS, S), lambda b, p, i: (b, p, 0))],
        out_specs=pl.BlockSpec((1, 2 * MLA_V_DIM, tq), lambda b, p, i: (b, p, i)),
        out_shape=jax.ShapeDtypeStruct((B, MLA_WIDTH, S), jnp.float32),
        compiler_params=pltpu.CompilerParams(
            dimension_semantics=("arbitrary", "arbitrary", "arbitrary"), vmem_limit_bytes=VMEM_LIMIT),
        name="mla",
    )(qmt, km, vmt)


def _group_queries(q_ref, g, rows):
    return jnp.concatenate(
        [q_ref[0, (g * NSA_HPG + h) * rows:(g * NSA_HPG + h + 1) * rows, :] for h in range(NSA_HPG)], axis=1)


def _nsa_cmp_kernel(qt_ref, kc_ref, vct_ref, ovt_ref, oct_ref, qaugt_ref, score_ref):
    i = pl.program_id(1)
    tq = qt_ref.shape[2]
    nch = kc_ref.shape[1]
    nblk = score_ref.shape[1]
    cols4 = NSA_HPG * tq
    q0 = i * tq
    qpos_c = q0 + lax.broadcasted_iota(jnp.int32, (nch, cols4), 1) % tq
    cmp_end = lax.broadcasted_iota(jnp.int32, (nch, cols4), 0) * CMP_STRIDE + (CMP_BLOCK - 1)
    mask_c = cmp_end <= qpos_c

    blk = lax.broadcasted_iota(jnp.int32, (nblk, tq), 0)
    qpos_r = q0 + lax.broadcasted_iota(jnp.int32, (nblk, tq), 1)
    cur = qpos_r // SLC_BLOCK
    forced = (blk == 0) | (blk == cur) | (blk == cur - 1)
    causal = blk * SLC_BLOCK <= qpos_r
    groups = range(NSA_KV_GROUPS)

    scores = []
    for g in groups:
        q4 = _group_queries(qt_ref, g, HEAD_DIM)
        qpad = jnp.concatenate([q4, jnp.zeros_like(q4)], axis=0)
        s = jnp.where(mask_c, _dot(kc_ref[0, :, g * LANES:(g + 1) * LANES], qpad), -jnp.inf)
        m = jnp.max(s, axis=0, keepdims=True)
        m = jnp.where(m == -jnp.inf, 0.0, m)
        e = jnp.exp2(s - m)
        pr = e / jnp.maximum(jnp.sum(e, axis=0, keepdims=True), 1e-30)
        o = _dot(vct_ref[0, g * HEAD_DIM:(g + 1) * HEAD_DIM, :], pr.astype(MXU_DTYPE))
        for h in range(NSA_HPG):
            head = g * NSA_HPG + h
            oct_ref[0, head * HEAD_DIM:(head + 1) * HEAD_DIM, :] = o[:, h * tq:(h + 1) * tq]

        psum = (pr[:, 0:tq] + pr[:, tq:2 * tq]) + (pr[:, 2 * tq:3 * tq] + pr[:, 3 * tq:4 * tq])
        p_hi = psum.astype(MXU_DTYPE)
        p_lo = (psum - p_hi.astype(jnp.float32)).astype(MXU_DTYPE)
        imp = _dot(ovt_ref[...], p_hi) + _dot(ovt_ref[...], p_lo)
        score = jnp.where(forced, FORCE_SCORE, jnp.where(causal, imp, -FORCE_SCORE))
        score_ref[g] = score
        scores.append(score)

    def count(c, cnts):
        out = list(cnts)
        for t in range(RANK_UNROLL):
            r = c * RANK_UNROLL + t
            lower_idx = r < blk
            for g in groups:
                row = score_ref[g, pl.ds(r, 1), :]
                beats = (row > scores[g]) | ((row == scores[g]) & lower_idx)
                out[g] = out[g] + jnp.where(beats, 1.0, 0.0)
        return tuple(out)

    zero = jnp.zeros((nblk, tq), jnp.float32)
    cnts = lax.fori_loop(0, i + 1, count, (zero,) * NSA_KV_GROUPS)
    for g in groups:
        notsel = jnp.where(cnts[g] < float(SLC_TOPK), 0.0, 1.0).astype(qaugt_ref.dtype)
        for h in range(NSA_HPG):
            head = g * NSA_HPG + h
            qaugt_ref[0, head * LANES:head * LANES + HEAD_DIM, :] = qt_ref[0, head * HEAD_DIM:(head + 1) * HEAD_DIM, :]
            qaugt_ref[0, head * LANES + HEAD_DIM:(head + 1) * LANES, :] = notsel


def _nsa_cmp_call(qnt, kc, vct, ovt):
    B, _, S = qnt.shape
    tq = NSA_TQ
    nch = kc.shape[1]
    return pl.pallas_call(
        _nsa_cmp_kernel,
        grid=(B, S // tq),
        in_specs=[pl.BlockSpec((1, NSA_WIDTH, tq), lambda b, i: (b, 0, i)),
                  pl.BlockSpec((1, nch, 2 * LANES), lambda b, i: (b, 0, 0)),
                  pl.BlockSpec((1, NSA_KV_WIDTH, nch), lambda b, i: (b, 0, 0)),
                  pl.BlockSpec(ovt.shape, lambda b, i: (0, 0))],
        out_specs=[pl.BlockSpec((1, NSA_WIDTH, tq), lambda b, i: (b, 0, i)),
                   pl.BlockSpec((1, NSA_HEADS * LANES, tq), lambda b, i: (b, 0, i))],
        out_shape=[jax.ShapeDtypeStruct((B, NSA_WIDTH, S), jnp.float32),
                   jax.ShapeDtypeStruct((B, NSA_HEADS * LANES, S), MXU_DTYPE)],
        scratch_shapes=[pltpu.VMEM((NSA_KV_GROUPS, LANES - HEAD_DIM, tq), jnp.float32)],
        compiler_params=pltpu.CompilerParams(
            dimension_semantics=("arbitrary", "arbitrary"), vmem_limit_bytes=VMEM_LIMIT),
        name="nsa_cmp",
    )(qnt, kc, vct, ovt)


def _nsa_attn_kernel(qaugt_ref, ksa_ref, vst_ref, kwp_ref, vwt_ref, oct_ref, gatet_ref, y_ref):
    i = pl.program_id(1)
    tq = qaugt_ref.shape[2]
    cols4 = NSA_HPG * tq
    wwin = WINDOW + tq
    groups = range(NSA_KV_GROUPS)
    lower = (lax.broadcasted_iota(jnp.int32, (tq, cols4), 0)
             <= lax.broadcasted_iota(jnp.int32, (tq, cols4), 1) % tq)
    qs = [_group_queries(qaugt_ref, g, LANES) for g in groups]

    def pv_from(vt_ref):
        def pv(g, p, start, width):
            return _dot(vt_ref[0, g * V_ROWS:(g + 1) * V_ROWS, pl.ds(start, width)], p)
        return pv

    o_slc = _flash_streams(
        NSA_KV_GROUPS,
        lambda g, start, width: _dot(ksa_ref[0, pl.ds(start, width), g * LANES:(g + 1) * LANES], qs[g]),
        pv_from(vst_ref), lower, i, tq, NSA_WIDE)

    wstart = pl.multiple_of(jnp.maximum(i * tq - WINDOW, 0), tq)
    qpos = i * tq + lax.broadcasted_iota(jnp.int32, (wwin, cols4), 1) % tq
    dist = qpos - (wstart + lax.broadcasted_iota(jnp.int32, (wwin, cols4), 0))
    band = (dist >= 0) & (dist < WINDOW)
    pv_w = pv_from(vwt_ref)
    o_win = []
    for g in groups:
        s = jnp.where(band, _dot(kwp_ref[0, pl.ds(wstart, wwin), g * LANES:(g + 1) * LANES], qs[g]), -jnp.inf)
        p = _numerators(s, jnp.max(s, axis=0, keepdims=True))
        o_win.append(_normalise(pv_w(g, p, wstart, wwin)))

    sig = jax.nn.sigmoid(gatet_ref[0])
    for g in groups:
        for h in range(NSA_HPG):
            head = g * NSA_HPG + h
            cols = slice(h * tq, (h + 1) * tq)
            gate = lambda br: sig[br * NSA_HEADS + head:br * NSA_HEADS + head + 1, :]
            y_ref[0, head * HEAD_DIM:(head + 1) * HEAD_DIM, :] = (
                gate(0) * oct_ref[0, head * HEAD_DIM:(head + 1) * HEAD_DIM, :]
                + gate(1) * o_slc[g][:, cols] + gate(2) * o_win[g][:, cols])


def _nsa_attn_call(qaugt, ksa, vst, kwp, vwt, oct, gatet):
    B, S, _ = ksa.shape
    tq = NSA_TQ
    feat = lambda r: pl.BlockSpec((1, r, tq), lambda b, i: (b, 0, i))
    seq_tok = lambda w: pl.BlockSpec((1, S, w), lambda b, i: (b, 0, 0))
    seq_feat = lambda r: pl.BlockSpec((1, r, S), lambda b, i: (b, 0, 0))
    return pl.pallas_call(
        _nsa_attn_kernel,
        grid=(B, S // tq),
        in_specs=[feat(NSA_HEADS * LANES), seq_tok(2 * LANES), seq_feat(NSA_KV_GROUPS * V_ROWS), seq_tok(2 * LANES),
                  seq_feat(NSA_KV_GROUPS * V_ROWS), feat(NSA_WIDTH), feat(GATE_ROWS)],
        out_specs=feat(NSA_WIDTH),
        out_shape=jax.ShapeDtypeStruct((B, NSA_WIDTH, S), jnp.float32),
        compiler_params=pltpu.CompilerParams(
            dimension_semantics=("arbitrary", "arbitrary"), vmem_limit_bytes=VMEM_LIMIT),
        name="nsa_attn",
    )(qaugt, ksa, vst, kwp, vwt, oct, gatet)


def _rms_cols(yt, g_col):
    return yt * lax.rsqrt(jnp.mean(yt * yt, axis=0, keepdims=True) + NORM_EPS) * g_col


def _out_kernel(x_ref, ymt_ref, ynt_ref, gm_ref, gn_ref, wo_ref, gmlp_ref, wup_ref, wdn_ref, gfin_ref, o_ref):
    mixed_t = jnp.concatenate([_rms_cols(ymt_ref[0], gm_ref[...]), _rms_cols(ynt_ref[0], gn_ref[...])], axis=0)
    h = x_ref[0] + _dot_tn(mixed_t.astype(MXU_DTYPE), wo_ref[...])
    hn = _rms(h, gmlp_ref[...]).astype(MXU_DTYPE)
    acc = h
    for c in range(D_FF // FF_CHUNK):
        a = jnp.maximum(_dot(hn, wup_ref[:, c * FF_CHUNK:(c + 1) * FF_CHUNK]), 0.0)
        acc = acc + _dot((a * a).astype(MXU_DTYPE), wdn_ref[c * FF_CHUNK:(c + 1) * FF_CHUNK, :])
    o_ref[0] = _rms(acc, gfin_ref[...])


def _out_call(x, ymt, ynt, gm, gn, wo, gmlp, wup, wdn, gfin):
    B, S, D = x.shape
    rows = OUT_ROWS
    tok = pl.BlockSpec((1, rows, D), lambda b, i: (b, i, 0))
    feat = lambda r: pl.BlockSpec((1, r, rows), lambda b, i: (b, 0, i))
    full = lambda a: pl.BlockSpec(a.shape, lambda b, i: (0,) * a.ndim)
    return pl.pallas_call(
        _out_kernel,
        grid=(B, S // rows),
        in_specs=[tok, feat(MLA_WIDTH), feat(NSA_WIDTH), full(gm), full(gn), full(wo), full(gmlp),
                  full(wup), full(wdn), full(gfin)],
        out_specs=tok,
        out_shape=jax.ShapeDtypeStruct((B, S, D), jnp.float32),
        compiler_params=pltpu.CompilerParams(
            dimension_semantics=("arbitrary", "arbitrary"), vmem_limit_bytes=VMEM_LIMIT),
        name="out",
    )(x, ymt, ynt, gm, gn, wo, gmlp, wup, wdn, gfin)


def _rope_cs(pos, dim):
    inv_freq = jnp.exp(-math.log(ROPE_THETA) * jnp.arange(0, dim, 2, dtype=jnp.float32) / dim)
    ang = pos.astype(jnp.float32)[:, None] * inv_freq[None, :]
    return jnp.cos(ang), jnp.sin(ang)


def _nsa_rope_tables(pos, width):
    c, s = _rope_cs(pos, NSA_ROPE_DIM)
    n = pos.shape[0]
    rest = HEAD_DIM - NSA_ROPE_DIM
    ch = jnp.concatenate([c, c, jnp.ones((n, rest), jnp.float32)], axis=1)
    sh = jnp.concatenate([-s, s, jnp.zeros((n, rest), jnp.float32)], axis=1)
    reps = width // HEAD_DIM
    return jnp.tile(ch, (1, reps)), jnp.tile(sh, (1, reps))


def _proj_tables(S):
    pos = jnp.arange(S)
    c, s = _rope_cs(pos, MLA_ROPE_DIM)
    z = lambda w: jnp.zeros((S, w), jnp.float32)
    pad = LANES - MLA_NOPE_DIM - MLA_ROPE_DIM
    ck = jnp.concatenate([z(MLA_NOPE_DIM), c, c, z(pad)], axis=1)
    sk = jnp.concatenate([z(MLA_NOPE_DIM), -s, s, z(pad)], axis=1)
    cn, sn = _nsa_rope_tables(pos, LANES)
    onehot = (pos[:, None] // SLC_BLOCK == jnp.arange(LANES - HEAD_DIM)[None, :]).astype(jnp.float32)
    eneg = jnp.concatenate([z(HEAD_DIM), MASK_BIAS * onehot], axis=1)
    cn8, sn8 = _rope_cs(pos, NSA_ROPE_DIM)
    tabt = jnp.concatenate([c.T, s.T, cn8.T, sn8.T], axis=0)
    return jnp.concatenate([ck, sk, cn, sn, eneg], axis=1), tabt


def _compress_tables(nch):
    end = jnp.arange(nch) * CMP_STRIDE + CMP_BLOCK - 1
    c, s = _nsa_rope_tables(end, HEAD_DIM)
    z = jnp.zeros((nch, LANES - HEAD_DIM), jnp.float32)
    return jnp.concatenate([c, z, s, z], axis=1)


def _overlap_t(nch, nblk):
    cs = jnp.arange(nch)[None, :] * CMP_STRIDE
    ss = jnp.arange(nblk)[:, None] * SLC_BLOCK
    ov = jnp.clip(jnp.minimum(cs + CMP_BLOCK, ss + SLC_BLOCK) - jnp.maximum(cs, ss), 0, None)
    ov = ov.astype(jnp.float32) / CMP_BLOCK
    ov = jnp.where(jnp.arange(nch)[None, :] < nch - 1, ov, 0.0)
    return ov.astype(MXU_DTYPE)


def _pad_cols(w, width):
    return jnp.pad(w, ((0, 0), (0, width - w.shape[1])))


def _compress_consts(pe_k, w1_k, b1_k, w2_k, b2_k, pe_v, w1_v, b1_v, w2_v, b2_v):
    half = CMP_BLOCK // 2
    ncol = 2 * NSA_KV_GROUPS
    w1 = jnp.stack([w1_k, w1_k, w1_v, w1_v]).reshape(ncol, 2, half, HEAD_DIM, CMP_HIDDEN)
    eye = jnp.eye(ncol, dtype=w1.dtype)

    def spread(wh):
        return jnp.einsum('cldh,ce->lcdeh', wh, eye).reshape(half * ncol * HEAD_DIM, ncol * CMP_HIDDEN)

    pe = jnp.stack([pe_k, pe_k, pe_v, pe_v]).reshape(ncol, 2, half, HEAD_DIM)
    pe_row = lambda ph: ph.transpose(1, 0, 2).reshape(1, half * ncol * HEAD_DIM)
    b1 = jnp.concatenate([b1_k, b1_k, b1_v, b1_v]).reshape(1, ncol * CMP_HIDDEN)
    return [
        pe_row(pe[:, 0]), pe_row(pe[:, 1]),
        spread(w1[:, 0]).astype(MXU_DTYPE), spread(w1[:, 1]).astype(MXU_DTYPE), b1,
        _pad_cols(w2_k, LANES).astype(MXU_DTYPE), _pad_cols(b2_k[None, :], LANES),
        w2_v.T.astype(MXU_DTYPE), b2_v[:, None],
    ]


def _proj_weights(w_in, w_uq, w_ukv):
    d = w_in.shape[0]
    z = lambda w: jnp.zeros((d, w), w_in.dtype)
    o_kr = MLA_Q_RANK + MLA_KV_RANK
    o_qn = o_kr + MLA_ROPE_DIM
    o_kv = o_qn + NSA_WIDTH
    o_gate = o_kv + 6 * NSA_KV_WIDTH
    kv = lambda t: w_in[:, o_kv + t * NSA_KV_WIDTH:o_kv + (t + 1) * NSA_KV_WIDTH]
    win = jnp.concatenate([
        w_in[:, :o_kr],
        z(MLA_NOPE_DIM), w_in[:, o_kr:o_qn], z(LANES - MLA_NOPE_DIM - MLA_ROPE_DIM),
        kv(0), kv(1), kv(2), kv(4)], axis=1)
    assert win.shape[1] == _U_COLS
    wint = jnp.concatenate([w_in[:, o_qn:o_kv], kv(3), kv(5), w_in[:, o_gate:], z(GATE_ROWS - NSA_GATES)], axis=1).T
    assert wint.shape[0] == _UT_ROWS
    qd = MLA_NOPE_DIM + MLA_ROPE_DIM
    wq = jnp.pad(w_uq.reshape(MLA_Q_RANK, MLA_HEADS, qd), ((0, 0), (0, 0), (0, LANES - qd)))
    wqt = wq.reshape(MLA_Q_RANK, MLA_HEADS * LANES).T
    kvw = w_ukv.reshape(MLA_KV_RANK, MLA_HEADS, MLA_NOPE_DIM + MLA_V_DIM)
    zk = jnp.zeros((MLA_KV_RANK, MLA_HEADS, LANES - MLA_NOPE_DIM), w_ukv.dtype)
    wk = jnp.concatenate([kvw[..., :MLA_NOPE_DIM], zk], axis=-1).reshape(MLA_KV_RANK, MLA_HEADS * LANES)
    wvt = kvw[..., MLA_NOPE_DIM:].reshape(MLA_KV_RANK, MLA_WIDTH).T
    return (win.astype(MXU_DTYPE), wint.astype(MXU_DTYPE), wqt.astype(MXU_DTYPE), wk.astype(MXU_DTYPE),
            wvt.astype(MXU_DTYPE))


def kernel(x, g_mix_norm, w_in, g_cq, w_uq, g_ckv, w_ukv, cmp_pe_k, cmp_w1_k, cmp_b1_k, cmp_w2_k, cmp_b2_k,
           cmp_pe_v, cmp_w1_v, cmp_b1_v, cmp_w2_v, cmp_b2_v, g_out_mla, g_out_nsa, w_o, g_mlp_norm, w_up,
           w_down, g_final):
    B, S, D = x.shape
    nch = S // CMP_STRIDE
    nblk = S // SLC_BLOCK
    assert w_in.shape[0] == 1
    assert D == D_MODEL and S % MLA_TQ == 0 and S % PROJ_ROWS == 0 and S % OUT_ROWS == 0
    assert S % NSA_TQ == 0 and WINDOW % NSA_TQ == 0 and S >= WINDOW + NSA_TQ
    assert SLC_TOPK <= nblk <= LANES - HEAD_DIM
    row = lambda g: g.reshape(1, -1)
    col = lambda g: g.reshape(-1, 1)

    tabs, tabt = _proj_tables(S)
    ctab = _compress_tables(nch)
    ovt = _overlap_t(nch, LANES - HEAD_DIM)

    win, wint, wqt, wk, wvt = _proj_weights(w_in[0], w_uq[0], w_ukv[0])
    qmt, km, vmt, qnt, ksa, vst, kwp, vwt, kcv, gatet = _proj_call(
        x, tabs, tabt, row(g_mix_norm[0]), win, wint, row(g_cq[0]), wqt, row(g_ckv[0]), wk, wvt)
    consts = _compress_consts(cmp_pe_k[0], cmp_w1_k[0], cmp_b1_k[0], cmp_w2_k[0], cmp_b2_k[0],
                              cmp_pe_v[0], cmp_w1_v[0], cmp_b1_v[0], cmp_w2_v[0], cmp_b2_v[0])
    kc, vct = _compress_call(kcv.reshape(B, nch, CMP_STRIDE * 2 * LANES), consts, ctab)
    y_mla_t = _mla_call(qmt, km, vmt)
    oct, qaugt = _nsa_cmp_call(qnt, kc, vct, ovt)
    y_nsa_t = _nsa_attn_call(qaugt, ksa, vst, kwp, vwt, oct, gatet)
    return _out_call(x, y_mla_t, y_nsa_t, col(g_out_mla[0]), col(g_out_nsa[0]), w_o[0].astype(MXU_DTYPE),
                     row(g_mlp_norm[0]), w_up[0].astype(MXU_DTYPE), w_down[0].astype(MXU_DTYPE), row(g_final))
```

```python
import math

import jax
import jax.numpy as jnp
from jax import lax
from jax.experimental import pallas as pl
from jax.experimental.pallas import tpu as pltpu

D_MODEL = 1024
HEAD_DIM = 64
ROPE_THETA = 500000.0
NORM_EPS = 1e-6

MLA_HEADS = 8
MLA_Q_RANK = 256
MLA_KV_RANK = 128
MLA_NOPE_DIM = 64
MLA_ROPE_DIM = 32
MLA_V_DIM = 64
MLA_WIDTH = MLA_HEADS * MLA_V_DIM

NSA_HEADS = 8
NSA_KV_GROUPS = 2
NSA_HPG = NSA_HEADS // NSA_KV_GROUPS
NSA_ROPE_DIM = HEAD_DIM // 4
NSA_WIDTH = NSA_HEADS * HEAD_DIM
NSA_KV_WIDTH = NSA_KV_GROUPS * HEAD_DIM
NSA_GATES = 3 * NSA_HEADS
CMP_BLOCK = 32
CMP_STRIDE = 16
CMP_HIDDEN = 2 * HEAD_DIM
SLC_BLOCK = 64
SLC_TOPK = 16
WINDOW = 512
FORCE_SCORE = 1e9
D_FF = 4 * D_MODEL

LANES = 128
SUBLANES = 8
VMEM_LIMIT = 56 * 1024 * 1024

MXU_DTYPE = jnp.bfloat16
PROJ_ROWS = 512
MLA_TQ = 1024
MLA_WIDE = 2
NSA_TQ = 512
NSA_WIDE = 2
WIN_SUB = 128
RANK_UNROLL = NSA_TQ // SLC_BLOCK
OUT_ROWS = 512
FF_CHUNK = 1024
LOG2E = math.log2(math.e)
MASK_BIAS = -1e9
GATE_ROWS = 32
BF16_ROWS = 16
V_ROWS = HEAD_DIM + BF16_ROWS
EXP_DTYPE = jnp.float32

_U_CQ = 0
_U_CKV = 256
_U_KPE = 384
_U_KCV = 512
_U_KSLC = 768
_U_KWIN = 896
_U_COLS = 1024
_N_TABS = 5
_UT_QN = 0
_UT_VSLC = 512
_UT_VWIN = 640
_UT_GATE = 768
_UT_ROWS = 800
_TT_ROWS = MLA_ROPE_DIM + NSA_ROPE_DIM


def _dot(a, b):
    return jnp.dot(a, b, preferred_element_type=jnp.float32)


def _dot_nt(a, b):
    return lax.dot_general(a, b, (((1,), (1,)), ((), ())), preferred_element_type=jnp.float32)


def _dot_tn(a, b):
    return lax.dot_general(a, b, (((0,), (0,)), ((), ())), preferred_element_type=jnp.float32)


def _rms(x, g):
    return x * lax.rsqrt(jnp.mean(x * x, axis=-1, keepdims=True) + NORM_EPS) * g


def _lane(shape):
    return lax.broadcasted_iota(jnp.int32, shape, len(shape) - 1)


def _swap_pairs(x, lo_end, width):
    n = x.shape[-1]
    lane = _lane(x.shape) % LANES
    return jnp.where(lane < lo_end, pltpu.roll(x, n - width, axis=1), pltpu.roll(x, width, axis=1))


def _nsa_rope(x, cn, sn):
    half = NSA_ROPE_DIM // 2
    lane = _lane(x.shape) % HEAD_DIM
    sw = jnp.where(lane < half, pltpu.roll(x, LANES - half, axis=1), pltpu.roll(x, half, axis=1))
    return x * cn + sw * sn


def _rope_rows(x1, x2, c, s):
    return x1 * c - x2 * s, x2 * c + x1 * s


def _proj_kernel(x_ref, tab_ref, tabt_ref, gmix_ref, win_ref, wint_ref, gcq_ref, wqt_ref, gckv_ref, wk_ref, wvt_ref,
                 qmt_ref, km_ref, vmt_ref, qnt_ref, ksa_ref, vst_ref, kwp_ref, vwt_ref, kcv_ref, gatet_ref):
    x = x_ref[0]
    n = _rms(x, gmix_ref[...]).astype(MXU_DTYPE)
    u = _dot(n, win_ref[...])
    ut = _dot_nt(wint_ref[...], n)

    ck_t, sk_t, cn_t, sn_t, eneg_t = (tab_ref[:, t * LANES:(t + 1) * LANES] for t in range(_N_TABS))
    hm = MLA_ROPE_DIM // 2
    hn = NSA_ROPE_DIM // 2
    cm_r = tabt_ref[0:hm, :]
    sm_r = tabt_ref[hm:2 * hm, :]
    cn_r = tabt_ref[2 * hm:2 * hm + hn, :]
    sn_r = tabt_ref[2 * hm + hn:2 * hm + 2 * hn, :]
    low = _lane((x.shape[0], LANES)) < HEAD_DIM

    mla_scale = (MLA_NOPE_DIM + MLA_ROPE_DIM) ** -0.5 * LOG2E
    cqn = _rms(u[:, _U_CQ:_U_CQ + MLA_Q_RANK], gcq_ref[...]).astype(MXU_DTYPE)
    qmt = _dot_nt(wqt_ref[...], cqn)
    r0 = MLA_NOPE_DIM
    for h in range(MLA_HEADS):
        xh = qmt[h * LANES:(h + 1) * LANES]
        p1, p2 = _rope_rows(xh[r0:r0 + hm], xh[r0 + hm:r0 + 2 * hm], cm_r, sm_r)
        qh = jnp.concatenate([xh[0:r0], p1, p2, xh[r0 + 2 * hm:]], axis=0) * mla_scale
        qmt_ref[0, h * LANES:(h + 1) * LANES, :] = qh.astype(qmt_ref.dtype)

    ckvn = _rms(u[:, _U_CKV:_U_CKV + MLA_KV_RANK], gckv_ref[...]).astype(MXU_DTYPE)
    km = _dot(ckvn, wk_ref[...])
    kpe = u[:, _U_KPE:_U_KPE + LANES]
    kpe = kpe * ck_t + _swap_pairs(kpe, MLA_NOPE_DIM + hm, hm) * sk_t
    for h in range(MLA_HEADS):
        km_ref[0, :, h * LANES:(h + 1) * LANES] = (km[:, h * LANES:(h + 1) * LANES] + kpe).astype(km_ref.dtype)
    ones_tile = jnp.where(lax.broadcasted_iota(jnp.int32, (BF16_ROWS, x.shape[0]), 0) == 0, 1.0, 0.0)

    def put_values(ref, vt):
        for h in range(vt.shape[0] // HEAD_DIM):
            ref[0, h * V_ROWS:h * V_ROWS + HEAD_DIM, :] = vt[h * HEAD_DIM:(h + 1) * HEAD_DIM].astype(ref.dtype)
            ref[0, h * V_ROWS + HEAD_DIM:(h + 1) * V_ROWS, :] = ones_tile.astype(ref.dtype)

    put_values(vmt_ref, _dot_nt(wvt_ref[...], ckvn))

    nsa_scale = HEAD_DIM ** -0.5 * LOG2E
    for h in range(NSA_HEADS):
        xh = ut[_UT_QN + h * HEAD_DIM:_UT_QN + (h + 1) * HEAD_DIM]
        p1, p2 = _rope_rows(xh[0:hn], xh[hn:2 * hn], cn_r, sn_r)
        qh = jnp.concatenate([p1, p2, xh[2 * hn:]], axis=0) * nsa_scale
        qnt_ref[0, h * HEAD_DIM:(h + 1) * HEAD_DIM, :] = qh.astype(qnt_ref.dtype)

    ks = _nsa_rope(u[:, _U_KSLC:_U_KSLC + LANES], cn_t, sn_t)
    ks_r = pltpu.roll(ks, HEAD_DIM, axis=1)
    ksa_ref[0, :, 0:LANES] = jnp.where(low, ks, eneg_t).astype(ksa_ref.dtype)
    ksa_ref[0, :, LANES:2 * LANES] = jnp.where(low, ks_r, eneg_t).astype(ksa_ref.dtype)

    kw = _nsa_rope(u[:, _U_KWIN:_U_KWIN + LANES], cn_t, sn_t)
    kw_r = pltpu.roll(kw, HEAD_DIM, axis=1)
    kwp_ref[0, :, 0:LANES] = jnp.where(low, kw, 0.0).astype(kwp_ref.dtype)
    kwp_ref[0, :, LANES:2 * LANES] = jnp.where(low, kw_r, 0.0).astype(kwp_ref.dtype)

    put_values(vst_ref, ut[_UT_VSLC:_UT_VSLC + NSA_KV_WIDTH])
    put_values(vwt_ref, ut[_UT_VWIN:_UT_VWIN + NSA_KV_WIDTH])
    gatet_ref[0] = ut[_UT_GATE:_UT_GATE + GATE_ROWS]
    kcv_ref[0, 0] = u[:, _U_KCV:_U_KCV + LANES]
    kcv_ref[0, 1] = u[:, _U_KCV + LANES:_U_KCV + 2 * LANES]


def _proj_call(x, tabs, tabt, gmix, win, wint, gcq, wqt, gckv, wk, wvt):
    B, S, D = x.shape
    rows = PROJ_ROWS
    ns = S // rows
    tok = lambda w: pl.BlockSpec((1, rows, w), lambda i, b: (b, i, 0))
    feat = lambda r: pl.BlockSpec((1, r, rows), lambda i, b: (b, 0, i))
    full = lambda a: pl.BlockSpec(a.shape, lambda i, b: (0,) * a.ndim)
    bf = MXU_DTYPE
    outs = [
        (jax.ShapeDtypeStruct((B, MLA_HEADS * LANES, S), bf), feat(MLA_HEADS * LANES)),
        (jax.ShapeDtypeStruct((B, S, MLA_HEADS * LANES), bf), tok(MLA_HEADS * LANES)),
        (jax.ShapeDtypeStruct((B, MLA_HEADS * V_ROWS, S), bf), feat(MLA_HEADS * V_ROWS)),
        (jax.ShapeDtypeStruct((B, NSA_WIDTH, S), bf), feat(NSA_WIDTH)),
        (jax.ShapeDtypeStruct((B, S, 2 * LANES), bf), tok(2 * LANES)),
        (jax.ShapeDtypeStruct((B, NSA_KV_GROUPS * V_ROWS, S), bf), feat(NSA_KV_GROUPS * V_ROWS)),
        (jax.ShapeDtypeStruct((B, S, 2 * LANES), bf), tok(2 * LANES)),
        (jax.ShapeDtypeStruct((B, NSA_KV_GROUPS * V_ROWS, S), bf), feat(NSA_KV_GROUPS * V_ROWS)),
        (jax.ShapeDtypeStruct((B, 2, S, LANES), jnp.float32),
         pl.BlockSpec((1, 2, rows, LANES), lambda i, b: (b, 0, i, 0))),
        (jax.ShapeDtypeStruct((B, GATE_ROWS, S), jnp.float32), feat(GATE_ROWS)),
    ]
    return pl.pallas_call(
        _proj_kernel,
        grid=(ns, B),
        in_specs=[tok(D), pl.BlockSpec((rows, _N_TABS * LANES), lambda i, b: (i, 0)),
                  pl.BlockSpec((_TT_ROWS, rows), lambda i, b: (0, i)),
                  full(gmix), full(win), full(wint), full(gcq), full(wqt), full(gckv), full(wk), full(wvt)],
        out_specs=[o[1] for o in outs],
        out_shape=[o[0] for o in outs],
        compiler_params=pltpu.CompilerParams(
            dimension_semantics=("arbitrary", "arbitrary"), vmem_limit_bytes=VMEM_LIMIT),
        name="proj",
    )(x, tabs, tabt, gmix, win, wint, gcq, wqt, gckv, wk, wvt)


def _gelu_tanh(x):
    return 0.5 * x * (1.0 + jnp.tanh(math.sqrt(2.0 / math.pi) * (x + 0.044715 * (x * x * x))))


def _compress_kernel(x_ref, pea_ref, peb_ref, wa_ref, wb_ref, b1_ref, w2k_ref, b2k_ref, w2vt_ref, b2vt_ref,
                     ctab_ref, kc_ref, vct_ref):
    nch = kc_ref.shape[1]
    a = b = None
    for t in range(CMP_STRIDE):
        for c in range(2):
            cols = slice((2 * t + c) * LANES, (2 * t + c + 1) * LANES)
            xt = x_ref[0, c, pl.ds(t, nch, stride=CMP_STRIDE), :]
            at = _dot((xt + pea_ref[:, cols]).astype(MXU_DTYPE), wa_ref[cols, :])
            bt = _dot((xt + peb_ref[:, cols]).astype(MXU_DTYPE), wb_ref[cols, :])
            a = at if a is None else a + at
            b = bt if b is None else b + bt
    hid = _gelu_tanh(a + pltpu.roll(b, nch - 1, axis=0) + b1_ref[...]).astype(MXU_DTYPE)
    cc = ctab_ref[:, 0:LANES]
    sc = ctab_ref[:, LANES:2 * LANES]
    for g in range(NSA_KV_GROUPS):
        hk = hid[:, g * LANES:(g + 1) * LANES]
        kc = _dot(hk, w2k_ref[...]) + b2k_ref[...]
        kc_ref[0, :, g * LANES:(g + 1) * LANES] = _nsa_rope(kc, cc, sc).astype(kc_ref.dtype)
        hv = hid[:, (NSA_KV_GROUPS + g) * LANES:(NSA_KV_GROUPS + g + 1) * LANES]
        vct_ref[0, g * HEAD_DIM:(g + 1) * HEAD_DIM, :] = (
            _dot_nt(w2vt_ref[...], hv) + b2vt_ref[...]).astype(vct_ref.dtype)


def _compress_call(kcv, consts, ctab):
    B, _, S, _ = kcv.shape
    nch = S // CMP_STRIDE
    full = lambda a: pl.BlockSpec(a.shape, lambda b: (0,) * a.ndim)
    return pl.pallas_call(
        _compress_kernel,
        grid=(B,),
        in_specs=[pl.BlockSpec((1, 2, S, LANES), lambda b: (b, 0, 0, 0))] + [full(c) for c in consts] + [full(ctab)],
        out_specs=[pl.BlockSpec((1, nch, 2 * LANES), lambda b: (b, 0, 0)),
                   pl.BlockSpec((1, NSA_KV_WIDTH, nch), lambda b: (b, 0, 0))],
        out_shape=[jax.ShapeDtypeStruct((B, nch, 2 * LANES), MXU_DTYPE),
                   jax.ShapeDtypeStruct((B, NSA_KV_WIDTH, nch), MXU_DTYPE)],
        compiler_params=pltpu.CompilerParams(
            dimension_semantics=("arbitrary",), vmem_limit_bytes=VMEM_LIMIT),
        name="compress",
    )(kcv, *consts, ctab)


def _numerators(s, m):
    return jnp.exp2((s - m).astype(EXP_DTYPE)).astype(MXU_DTYPE)


def _normalise(acc):
    return acc[0:HEAD_DIM] * (1.0 / acc[HEAD_DIM:HEAD_DIM + 1])


def _flash_streams(n_streams, scores, pv, lower, i, tq, wide):
    start_d = pl.multiple_of(i * tq, tq)
    carry = []
    for st in range(n_streams):
        s = jnp.where(lower, scores(st, start_d, tq), -jnp.inf)
        m = jnp.max(s, axis=0, keepdims=True)
        carry.append((m, pv(st, _numerators(s, m), start_d, tq)))

    def step(start, width, carry):
        out = []
        for st, (m, acc) in enumerate(carry):
            s = scores(st, start, width)
            m_new = jnp.maximum(m, jnp.max(s, axis=0, keepdims=True))
            out.append((m_new, jnp.exp2(m - m_new) * acc + pv(st, _numerators(s, m_new), start, width)))
        return tuple(out)

    wt = wide * tq
    n_wide = i // wide
    carry = lax.fori_loop(0, n_wide, lambda jj, c: step(pl.multiple_of(jj * wt, wt), wt, c), tuple(carry))
    carry = lax.fori_loop(n_wide * wide, i, lambda jj, c: step(pl.multiple_of(jj * tq, tq), tq, c), carry)
    return [_normalise(acc) for (m, acc) in carry]


def _mla_kernel(qt_ref, k_ref, vt_ref, o_ref):
    i = pl.program_id(2)
    tq = qt_ref.shape[2]
    lower = lax.broadcasted_iota(jnp.int32, (tq, tq), 0) <= lax.broadcasted_iota(jnp.int32, (tq, tq), 1)
    qs = [qt_ref[0, hh * LANES:(hh + 1) * LANES, :] for hh in range(2)]

    def scores(hh, start, width):
        return _dot(k_ref[0, pl.ds(start, width), hh * LANES:(hh + 1) * LANES], qs[hh])

    def pv(hh, p, start, width):
        return _dot(vt_ref[0, hh * V_ROWS:(hh + 1) * V_ROWS, pl.ds(start, width)], p)

    outs = _flash_streams(2, scores, pv, lower, i, tq, MLA_WIDE)
    for hh, o in enumerate(outs):
        o_ref[0, hh * MLA_V_DIM:(hh + 1) * MLA_V_DIM, :] = o


def _mla_call(qmt, km, vmt):
    B, S, _ = km.shape
    tq = MLA_TQ
    pairs = MLA_HEADS // 2
    return pl.pallas_call(
        _mla_kernel,
        grid=(B, pairs, S // tq),
        in_specs=[pl.BlockSpec((1, 2 * LANES, tq), lambda b, p, i: (b, p, i)),
                  pl.BlockSpec((1, S, 2 * LANES), lambda b, p, i: (b, 0, p)),
                  pl.BlockSpec((1, 2 * V_ROWS, S), lambda b, p, i: (b, p, 0))],
        out_specs=pl.BlockSpec((1, 2 * MLA_V_DIM, tq), lambda b, p, i: (b, p, i)),
        out_shape=jax.ShapeDtypeStruct((B, MLA_WIDTH, S), jnp.float32),
        compiler_params=pltpu.CompilerParams(
            dimension_semantics=("arbitrary", "arbitrary", "arbitrary"), vmem_limit_bytes=VMEM_LIMIT),
        name="mla",
    )(qmt, km, vmt)


def _group_queries(q_ref, g, rows):
    return jnp.concatenate(
        [q_ref[0, (g * NSA_HPG + h) * rows:(g * NSA_HPG + h + 1) * rows, :] for h in range(NSA_HPG)], axis=1)


def _nsa_cmp_kernel(qt_ref, kc_ref, vct_ref, ovt_ref, oct_ref, qaugt_ref, score_ref):
    i = pl.program_id(1)
    tq = qt_ref.shape[2]
    nch = kc_ref.shape[1]
    nblk = score_ref.shape[1]
    cols4 = NSA_HPG * tq
    q0 = i * tq
    qpos_c = q0 + lax.broadcasted_iota(jnp.int32, (nch, cols4), 1) % tq
    cmp_end = lax.broadcasted_iota(jnp.int32, (nch, cols4), 0) * CMP_STRIDE + (CMP_BLOCK - 1)
    mask_c = cmp_end <= qpos_c

    blk = lax.broadcasted_iota(jnp.int32, (nblk, tq), 0)
    qpos_r = q0 + lax.broadcasted_iota(jnp.int32, (nblk, tq), 1)
    cur = qpos_r // SLC_BLOCK
    forced = (blk == 0) | (blk == cur) | (blk == cur - 1)
    causal = blk * SLC_BLOCK <= qpos_r
    groups = range(NSA_KV_GROUPS)

    scores = []
    for g in groups:
        q4 = _group_queries(qt_ref, g, HEAD_DIM)
        qpad = jnp.concatenate([q4, jnp.zeros_like(q4)], axis=0)
        s = jnp.where(mask_c, _dot(kc_ref[0, :, g * LANES:(g + 1) * LANES], qpad), -jnp.inf)
        m = jnp.max(s, axis=0, keepdims=True)
        m = jnp.where(m == -jnp.inf, 0.0, m)
        e = jnp.exp2(s - m)
        pr = e / jnp.maximum(jnp.sum(e, axis=0, keepdims=True), 1e-30)
        o = _dot(vct_ref[0, g * HEAD_DIM:(g + 1) * HEAD_DIM, :], pr.astype(MXU_DTYPE))
        for h in range(NSA_HPG):
            head = g * NSA_HPG + h
            oct_ref[0, head * HEAD_DIM:(head + 1) * HEAD_DIM, :] = o[:, h * tq:(h + 1) * tq]

        psum = (pr[:, 0:tq] + pr[:, tq:2 * tq]) + (pr[:, 2 * tq:3 * tq] + pr[:, 3 * tq:4 * tq])
        p_hi = psum.astype(MXU_DTYPE)
        p_lo = (psum - p_hi.astype(jnp.float32)).astype(MXU_DTYPE)
        imp = _dot(ovt_ref[...], p_hi) + _dot(ovt_ref[...], p_lo)
        score = jnp.where(forced, FORCE_SCORE, jnp.where(causal, imp, -FORCE_SCORE))
        score_ref[g] = score
        scores.append(score)

    def count(c, cnts):
        out = list(cnts)
        for t in range(RANK_UNROLL):
            r = c * RANK_UNROLL + t
            tie = jnp.where(r < blk, 1.0, 0.0)
            for g in groups:
                row = score_ref[g, pl.ds(r, 1), :]
                out[g] = out[g] + jnp.where(row > scores[g], 1.0, jnp.where(row == scores[g], tie, 0.0))
        return tuple(out)

    zero = jnp.zeros((nblk, tq), jnp.float32)
    cnts = lax.fori_loop(0, i + 1, count, (zero,) * NSA_KV_GROUPS)
    for g in groups:
        notsel = jnp.where(cnts[g] < float(SLC_TOPK), 0.0, 1.0).astype(qaugt_ref.dtype)
        for h in range(NSA_HPG):
            head = g * NSA_HPG + h
            qaugt_ref[0, head * LANES:head * LANES + HEAD_DIM, :] = qt_ref[0, head * HEAD_DIM:(head + 1) * HEAD_DIM, :]
            qaugt_ref[0, head * LANES + HEAD_DIM:(head + 1) * LANES, :] = notsel


def _nsa_cmp_call(qnt, kc, vct, ovt):
    B, _, S = qnt.shape
    tq = NSA_TQ
    nch = kc.shape[1]
    return pl.pallas_call(
        _nsa_cmp_kernel,
        grid=(B, S // tq),
        in_specs=[pl.BlockSpec((1, NSA_WIDTH, tq), lambda b, i: (b, 0, i)),
                  pl.BlockSpec((1, nch, 2 * LANES), lambda b, i: (b, 0, 0)),
                  pl.BlockSpec((1, NSA_KV_WIDTH, nch), lambda b, i: (b, 0, 0)),
                  pl.BlockSpec(ovt.shape, lambda b, i: (0, 0))],
        out_specs=[pl.BlockSpec((1, NSA_WIDTH, tq), lambda b, i: (b, 0, i)),
                   pl.BlockSpec((1, NSA_HEADS * LANES, tq), lambda b, i: (b, 0, i))],
        out_shape=[jax.ShapeDtypeStruct((B, NSA_WIDTH, S), jnp.float32),
                   jax.ShapeDtypeStruct((B, NSA_HEADS * LANES, S), MXU_DTYPE)],
        scratch_shapes=[pltpu.VMEM((NSA_KV_GROUPS, LANES - HEAD_DIM, tq), jnp.float32)],
        compiler_params=pltpu.CompilerParams(
            dimension_semantics=("arbitrary", "arbitrary"), vmem_limit_bytes=VMEM_LIMIT),
        name="nsa_cmp",
    )(qnt, kc, vct, ovt)


def _nsa_attn_kernel(qaugt_ref, ksa_ref, vst_ref, kwp_ref, vwt_ref, oct_ref, gatet_ref, y_ref):
    i = pl.program_id(1)
    tq = qaugt_ref.shape[2]
    cols4 = NSA_HPG * tq
    groups = range(NSA_KV_GROUPS)
    lower = (lax.broadcasted_iota(jnp.int32, (tq, cols4), 0)
             <= lax.broadcasted_iota(jnp.int32, (tq, cols4), 1) % tq)
    qs = [_group_queries(qaugt_ref, g, LANES) for g in groups]

    def pv_from(vt_ref):
        def pv(g, p, start, width):
            return _dot(vt_ref[0, g * V_ROWS:(g + 1) * V_ROWS, pl.ds(start, width)], p)
        return pv

    o_slc = _flash_streams(
        NSA_KV_GROUPS,
        lambda g, start, width: _dot(ksa_ref[0, pl.ds(start, width), g * LANES:(g + 1) * LANES], qs[g]),
        pv_from(vst_ref), lower, i, tq, NSA_WIDE)

    sub = WIN_SUB
    wk = WINDOW + sub
    nsub = tq // sub
    rel = (lax.broadcasted_iota(jnp.int32, (wk, NSA_HPG * sub), 1) % sub
           - lax.broadcasted_iota(jnp.int32, (wk, NSA_HPG * sub), 0))
    pv_w = pv_from(vwt_ref)
    o_win = [[None] * nsub for _ in groups]
    for j in range(nsub):
        q_lo = i * tq + j * sub
        wstart = pl.multiple_of(jnp.maximum(q_lo - WINDOW, 0), sub)
        dist = rel + (q_lo - wstart)
        band = (dist >= 0) & (dist < WINDOW)
        for g in groups:
            q_sub = jnp.concatenate(
                [qs[g][:, h * tq + j * sub:h * tq + (j + 1) * sub] for h in range(NSA_HPG)], axis=1)
            s = jnp.where(band, _dot(kwp_ref[0, pl.ds(wstart, wk), g * LANES:(g + 1) * LANES], q_sub), -jnp.inf)
            p = _numerators(s, jnp.max(s, axis=0, keepdims=True))
            o_win[g][j] = _normalise(pv_w(g, p, wstart, wk))

    sig = jax.nn.sigmoid(gatet_ref[0])
    for g in groups:
        for h in range(NSA_HPG):
            head = g * NSA_HPG + h
            o_w = jnp.concatenate([o_win[g][j][:, h * sub:(h + 1) * sub] for j in range(nsub)], axis=1)
            gate = lambda br: sig[br * NSA_HEADS + head:br * NSA_HEADS + head + 1, :]
            y_ref[0, head * HEAD_DIM:(head + 1) * HEAD_DIM, :] = (
                gate(0) * oct_ref[0, head * HEAD_DIM:(head + 1) * HEAD_DIM, :]
                + gate(1) * o_slc[g][:, h * tq:(h + 1) * tq] + gate(2) * o_w)


def _nsa_attn_call(qaugt, ksa, vst, kwp, vwt, oct, gatet):
    B, S, _ = ksa.shape
    tq = NSA_TQ
    feat = lambda r: pl.BlockSpec((1, r, tq), lambda b, i: (b, 0, i))
    seq_tok = lambda w: pl.BlockSpec((1, S, w), lambda b, i: (b, 0, 0))
    seq_feat = lambda r: pl.BlockSpec((1, r, S), lambda b, i: (b, 0, 0))
    return pl.pallas_call(
        _nsa_attn_kernel,
        grid=(B, S // tq),
        in_specs=[feat(NSA_HEADS * LANES), seq_tok(2 * LANES), seq_feat(NSA_KV_GROUPS * V_ROWS), seq_tok(2 * LANES),
                  seq_feat(NSA_KV_GROUPS * V_ROWS), feat(NSA_WIDTH), feat(GATE_ROWS)],
        out_specs=feat(NSA_WIDTH),
        out_shape=jax.ShapeDtypeStruct((B, NSA_WIDTH, S), jnp.float32),
        compiler_params=pltpu.CompilerParams(
            dimension_semantics=("arbitrary", "arbitrary"), vmem_limit_bytes=VMEM_LIMIT),
        name="nsa_attn",
    )(qaugt, ksa, vst, kwp, vwt, oct, gatet)


def _rms_cols(yt, g_col):
    return yt * lax.rsqrt(jnp.mean(yt * yt, axis=0, keepdims=True) + NORM_EPS) * g_col


def _out_kernel(x_ref, ymt_ref, ynt_ref, gm_ref, gn_ref, wo_ref, gmlp_ref, wup_ref, wdn_ref, gfin_ref, o_ref):
    mixed_t = jnp.concatenate([_rms_cols(ymt_ref[0], gm_ref[...]), _rms_cols(ynt_ref[0], gn_ref[...])], axis=0)
    h = x_ref[0] + _dot_tn(mixed_t.astype(MXU_DTYPE), wo_ref[...])
    hn = _rms(h, gmlp_ref[...]).astype(MXU_DTYPE)
    acc = h
    for c in range(D_FF // FF_CHUNK):
        a = jnp.maximum(_dot(hn, wup_ref[:, c * FF_CHUNK:(c + 1) * FF_CHUNK]), 0.0)
        acc = acc + _dot((a * a).astype(MXU_DTYPE), wdn_ref[c * FF_CHUNK:(c + 1) * FF_CHUNK, :])
    o_ref[0] = _rms(acc, gfin_ref[...])


def _out_call(x, ymt, ynt, gm, gn, wo, gmlp, wup, wdn, gfin):
    B, S, D = x.shape
    rows = OUT_ROWS
    tok = pl.BlockSpec((1, rows, D), lambda b, i: (b, i, 0))
    feat = lambda r: pl.BlockSpec((1, r, rows), lambda b, i: (b, 0, i))
    full = lambda a: pl.BlockSpec(a.shape, lambda b, i: (0,) * a.ndim)
    return pl.pallas_call(
        _out_kernel,
        grid=(B, S // rows),
        in_specs=[tok, feat(MLA_WIDTH), feat(NSA_WIDTH), full(gm), full(gn), full(wo), full(gmlp),
                  full(wup), full(wdn), full(gfin)],
        out_specs=tok,
        out_shape=jax.ShapeDtypeStruct((B, S, D), jnp.float32),
        compiler_params=pltpu.CompilerParams(
            dimension_semantics=("arbitrary", "arbitrary"), vmem_limit_bytes=VMEM_LIMIT),
        name="out",
    )(x, ymt, ynt, gm, gn, wo, gmlp, wup, wdn, gfin)


def _rope_cs(pos, dim):
    inv_freq = jnp.exp(-math.log(ROPE_THETA) * jnp.arange(0, dim, 2, dtype=jnp.float32) / dim)
    ang = pos.astype(jnp.float32)[:, None] * inv_freq[None, :]
    return jnp.cos(ang), jnp.sin(ang)


def _nsa_rope_tables(pos, width):
    c, s = _rope_cs(pos, NSA_ROPE_DIM)
    n = pos.shape[0]
    rest = HEAD_DIM - NSA_ROPE_DIM
    ch = jnp.concatenate([c, c, jnp.ones((n, rest), jnp.float32)], axis=1)
    sh = jnp.concatenate([-s, s, jnp.zeros((n, rest), jnp.float32)], axis=1)
    reps = width // HEAD_DIM
    return jnp.tile(ch, (1, reps)), jnp.tile(sh, (1, reps))


def _proj_tables(S):
    pos = jnp.arange(S)
    c, s = _rope_cs(pos, MLA_ROPE_DIM)
    z = lambda w: jnp.zeros((S, w), jnp.float32)
    pad = LANES - MLA_NOPE_DIM - MLA_ROPE_DIM
    ck = jnp.concatenate([z(MLA_NOPE_DIM), c, c, z(pad)], axis=1)
    sk = jnp.concatenate([z(MLA_NOPE_DIM), -s, s, z(pad)], axis=1)
    cn, sn = _nsa_rope_tables(pos, LANES)
    onehot = (pos[:, None] // SLC_BLOCK == jnp.arange(LANES - HEAD_DIM)[None, :]).astype(jnp.float32)
    eneg = jnp.concatenate([z(HEAD_DIM), MASK_BIAS * onehot], axis=1)
    cn8, sn8 = _rope_cs(pos, NSA_ROPE_DIM)
    tabt = jnp.concatenate([c.T, s.T, cn8.T, sn8.T], axis=0)
    return jnp.concatenate([ck, sk, cn, sn, eneg], axis=1), tabt


def _compress_tables(nch):
    end = jnp.arange(nch) * CMP_STRIDE + CMP_BLOCK - 1
    c, s = _nsa_rope_tables(end, HEAD_DIM)
    z = jnp.zeros((nch, LANES - HEAD_DIM), jnp.float32)
    return jnp.concatenate([c, z, s, z], axis=1)


def _overlap_t(nch, nblk):
    cs = jnp.arange(nch)[None, :] * CMP_STRIDE
    ss = jnp.arange(nblk)[:, None] * SLC_BLOCK
    ov = jnp.clip(jnp.minimum(cs + CMP_BLOCK, ss + SLC_BLOCK) - jnp.maximum(cs, ss), 0, None)
    ov = ov.astype(jnp.float32) / CMP_BLOCK
    ov = jnp.where(jnp.arange(nch)[None, :] < nch - 1, ov, 0.0)
    return ov.astype(MXU_DTYPE)


def _pad_cols(w, width):
    return jnp.pad(w, ((0, 0), (0, width - w.shape[1])))


def _compress_consts(pe_k, w1_k, b1_k, w2_k, b2_k, pe_v, w1_v, b1_v, w2_v, b2_v):
    half = CMP_BLOCK // 2
    ncol = 2 * NSA_KV_GROUPS
    w1 = jnp.stack([w1_k, w1_k, w1_v, w1_v]).reshape(ncol, 2, half, HEAD_DIM, CMP_HIDDEN)
    eye = jnp.eye(ncol, dtype=w1.dtype)

    def spread(wh):
        return jnp.einsum('cldh,ce->lcdeh', wh, eye).reshape(half * ncol * HEAD_DIM, ncol * CMP_HIDDEN)

    pe = jnp.stack([pe_k, pe_k, pe_v, pe_v]).reshape(ncol, 2, half, HEAD_DIM)
    pe_row = lambda ph: ph.transpose(1, 0, 2).reshape(1, half * ncol * HEAD_DIM)
    b1 = jnp.concatenate([b1_k, b1_k, b1_v, b1_v]).reshape(1, ncol * CMP_HIDDEN)
    return [
        pe_row(pe[:, 0]), pe_row(pe[:, 1]),
        spread(w1[:, 0]).astype(MXU_DTYPE), spread(w1[:, 1]).astype(MXU_DTYPE), b1,
        _pad_cols(w2_k, LANES).astype(MXU_DTYPE), _pad_cols(b2_k[None, :], LANES),
        w2_v.T.astype(MXU_DTYPE), b2_v[:, None],
    ]


def _proj_weights(w_in, w_uq, w_ukv):
    d = w_in.shape[0]
    z = lambda w: jnp.zeros((d, w), w_in.dtype)
    o_kr = MLA_Q_RANK + MLA_KV_RANK
    o_qn = o_kr + MLA_ROPE_DIM
    o_kv = o_qn + NSA_WIDTH
    o_gate = o_kv + 6 * NSA_KV_WIDTH
    kv = lambda t: w_in[:, o_kv + t * NSA_KV_WIDTH:o_kv + (t + 1) * NSA_KV_WIDTH]
    win = jnp.concatenate([
        w_in[:, :o_kr],
        z(MLA_NOPE_DIM), w_in[:, o_kr:o_qn], z(LANES - MLA_NOPE_DIM - MLA_ROPE_DIM),
        kv(0), kv(1), kv(2), kv(4)], axis=1)
    assert win.shape[1] == _U_COLS
    wint = jnp.concatenate([w_in[:, o_qn:o_kv], kv(3), kv(5), w_in[:, o_gate:], z(GATE_ROWS - NSA_GATES)], axis=1).T
    assert wint.shape[0] == _UT_ROWS
    qd = MLA_NOPE_DIM + MLA_ROPE_DIM
    wq = jnp.pad(w_uq.reshape(MLA_Q_RANK, MLA_HEADS, qd), ((0, 0), (0, 0), (0, LANES - qd)))
    wqt = wq.reshape(MLA_Q_RANK, MLA_HEADS * LANES).T
    kvw = w_ukv.reshape(MLA_KV_RANK, MLA_HEADS, MLA_NOPE_DIM + MLA_V_DIM)
    zk = jnp.zeros((MLA_KV_RANK, MLA_HEADS, LANES - MLA_NOPE_DIM), w_ukv.dtype)
    wk = jnp.concatenate([kvw[..., :MLA_NOPE_DIM], zk], axis=-1).reshape(MLA_KV_RANK, MLA_HEADS * LANES)
    wvt = kvw[..., MLA_NOPE_DIM:].reshape(MLA_KV_RANK, MLA_WIDTH).T
    return (win.astype(MXU_DTYPE), wint.astype(MXU_DTYPE), wqt.astype(MXU_DTYPE), wk.astype(MXU_DTYPE),
            wvt.astype(MXU_DTYPE))


def kernel(x, g_mix_norm, w_in, g_cq, w_uq, g_ckv, w_ukv, cmp_pe_k, cmp_w1_k, cmp_b1_k, cmp_w2_k, cmp_b2_k,
           cmp_pe_v, cmp_w1_v, cmp_b1_v, cmp_w2_v, cmp_b2_v, g_out_mla, g_out_nsa, w_o, g_mlp_norm, w_up,
           w_down, g_final):
    B, S, D = x.shape
    nch = S // CMP_STRIDE
    nblk = S // SLC_BLOCK
    assert w_in.shape[0] == 1
    assert D == D_MODEL and S % MLA_TQ == 0 and S % PROJ_ROWS == 0 and S % OUT_ROWS == 0
    assert S % NSA_TQ == 0 and NSA_TQ % WIN_SUB == 0 and WINDOW % WIN_SUB == 0 and S >= WINDOW + WIN_SUB
    assert SLC_TOPK <= nblk <= LANES - HEAD_DIM
    row = lambda g: g.reshape(1, -1)
    col = lambda g: g.reshape(-1, 1)

    tabs, tabt = _proj_tables(S)
    ctab = _compress_tables(nch)
    ovt = _overlap_t(nch, LANES - HEAD_DIM)

    win, wint, wqt, wk, wvt = _proj_weights(w_in[0], w_uq[0], w_ukv[0])
    qmt, km, vmt, qnt, ksa, vst, kwp, vwt, kcv, gatet = _proj_call(
        x, tabs, tabt, row(g_mix_norm[0]), win, wint, row(g_cq[0]), wqt, row(g_ckv[0]), wk, wvt)
    consts = _compress_consts(cmp_pe_k[0], cmp_w1_k[0], cmp_b1_k[0], cmp_w2_k[0], cmp_b2_k[0],
                              cmp_pe_v[0], cmp_w1_v[0], cmp_b1_v[0], cmp_w2_v[0], cmp_b2_v[0])
    kc, vct = _compress_call(kcv, consts, ctab)
    y_mla_t = _mla_call(qmt, km, vmt)
    oct, qaugt = _nsa_cmp_call(qnt, kc, vct, ovt)
    y_nsa_t = _nsa_attn_call(qaugt, ksa, vst, kwp, vwt, oct, gatet)
    return _out_call(x, y_mla_t, y_nsa_t, col(g_out_mla[0]), col(g_out_nsa[0]), w_o[0].astype(MXU_DTYPE),
                     row(g_mlp_norm[0]), w_up[0].astype(MXU_DTYPE), w_down[0].astype(MXU_DTYPE), row(g_final))
```

```python
import math

import jax
import jax.numpy as jnp
from jax import lax
from jax.experimental import pallas as pl
from jax.experimental.pallas import tpu as pltpu

D_MODEL = 1024
HEAD_DIM = 64
ROPE_THETA = 500000.0
NORM_EPS = 1e-6

MLA_HEADS = 8
MLA_Q_RANK = 256
MLA_KV_RANK = 128
MLA_NOPE_DIM = 64
MLA_ROPE_DIM = 32
MLA_V_DIM = 64
MLA_WIDTH = MLA_HEADS * MLA_V_DIM

NSA_HEADS = 8
NSA_KV_GROUPS = 2
NSA_HPG = NSA_HEADS // NSA_KV_GROUPS
NSA_ROPE_DIM = HEAD_DIM // 4
NSA_WIDTH = NSA_HEADS * HEAD_DIM
NSA_KV_WIDTH = NSA_KV_GROUPS * HEAD_DIM
NSA_GATES = 3 * NSA_HEADS
CMP_BLOCK = 32
CMP_STRIDE = 16
CMP_HIDDEN = 2 * HEAD_DIM
SLC_BLOCK = 64
SLC_TOPK = 16
WINDOW = 512
FORCE_SCORE = 1e9
D_FF = 4 * D_MODEL

LANES = 128
SUBLANES = 8
VMEM_LIMIT = 56 * 1024 * 1024

MXU_DTYPE = jnp.bfloat16
PROJ_ROWS = 512
MLA_TQ = 1024
MLA_HPS = 4
MLA_WIDE = 2
NSA_TQ = 512
NSA_WIDE = 2
WIN_SUB = 128
RANK_UNROLL = NSA_TQ // SLC_BLOCK
OUT_ROWS = 512
FF_CHUNK = 1024
LOG2E = math.log2(math.e)
MASK_BIAS = -1e30
GATE_ROWS = 32
BF16_ROWS = 16
V_ROWS = HEAD_DIM + BF16_ROWS
EXP_DTYPE = jnp.float32

_U_CQ = 0
_U_CKV = 256
_U_KPE = 384
_U_KCV = 512
_U_KSLC = 768
_U_KWIN = 896
_U_COLS = 1024
_N_TABS = 5
_UT_QN = 0
_UT_VSLC = 512
_UT_VWIN = 640
_UT_GATE = 768
_UT_ROWS = 800
_TT_ROWS = MLA_ROPE_DIM + NSA_ROPE_DIM


def _dot(a, b):
    return jnp.dot(a, b, preferred_element_type=jnp.float32)


def _dot_nt(a, b):
    return lax.dot_general(a, b, (((1,), (1,)), ((), ())), preferred_element_type=jnp.float32)


def _dot_tn(a, b):
    return lax.dot_general(a, b, (((0,), (0,)), ((), ())), preferred_element_type=jnp.float32)


def _rms(x, g):
    return x * lax.rsqrt(jnp.mean(x * x, axis=-1, keepdims=True) + NORM_EPS) * g


def _lane(shape):
    return lax.broadcasted_iota(jnp.int32, shape, len(shape) - 1)


def _swap_pairs(x, lo_end, width):
    n = x.shape[-1]
    lane = _lane(x.shape) % LANES
    return jnp.where(lane < lo_end, pltpu.roll(x, n - width, axis=1), pltpu.roll(x, width, axis=1))


def _nsa_rope(x, cn, sn):
    half = NSA_ROPE_DIM // 2
    lane = _lane(x.shape) % HEAD_DIM
    sw = jnp.where(lane < half, pltpu.roll(x, LANES - half, axis=1), pltpu.roll(x, half, axis=1))
    return x * cn + sw * sn


def _rope_rows(x1, x2, c, s):
    return x1 * c - x2 * s, x2 * c + x1 * s


def _proj_kernel(x_ref, tab_ref, tabt_ref, gmix_ref, win_ref, wint_ref, gcq_ref, wqt_ref, gckv_ref, wk_ref, wvt_ref,
                 qmt_ref, km_ref, vmt_ref, qnt_ref, ksa_ref, vst_ref, kwp_ref, vwt_ref, kcv_ref, gatet_ref):
    x = x_ref[0]
    n = _rms(x, gmix_ref[...]).astype(MXU_DTYPE)
    u = _dot(n, win_ref[...])
    ut = _dot_nt(wint_ref[...], n)

    ck_t, sk_t, cn_t, sn_t, eneg_t = (tab_ref[:, t * LANES:(t + 1) * LANES] for t in range(_N_TABS))
    hm = MLA_ROPE_DIM // 2
    hn = NSA_ROPE_DIM // 2
    cm_r = tabt_ref[0:hm, :]
    sm_r = tabt_ref[hm:2 * hm, :]
    cn_r = tabt_ref[2 * hm:2 * hm + hn, :]
    sn_r = tabt_ref[2 * hm + hn:2 * hm + 2 * hn, :]
    low = _lane((x.shape[0], LANES)) < HEAD_DIM

    mla_scale = (MLA_NOPE_DIM + MLA_ROPE_DIM) ** -0.5 * LOG2E
    cqn = _rms(u[:, _U_CQ:_U_CQ + MLA_Q_RANK], gcq_ref[...]).astype(MXU_DTYPE)
    qmt = _dot_nt(wqt_ref[...], cqn)
    r0 = MLA_NOPE_DIM
    for h in range(MLA_HEADS):
        xh = qmt[h * LANES:(h + 1) * LANES]
        p1, p2 = _rope_rows(xh[r0:r0 + hm], xh[r0 + hm:r0 + 2 * hm], cm_r, sm_r)
        qh = jnp.concatenate([xh[0:r0], p1, p2, xh[r0 + 2 * hm:]], axis=0) * mla_scale
        qmt_ref[0, h * LANES:(h + 1) * LANES, :] = qh.astype(qmt_ref.dtype)

    ckvn = _rms(u[:, _U_CKV:_U_CKV + MLA_KV_RANK], gckv_ref[...]).astype(MXU_DTYPE)
    km = _dot(ckvn, wk_ref[...])
    kpe = u[:, _U_KPE:_U_KPE + LANES]
    kpe = kpe * ck_t + _swap_pairs(kpe, MLA_NOPE_DIM + hm, hm) * sk_t
    for h in range(MLA_HEADS):
        km_ref[0, :, h * LANES:(h + 1) * LANES] = (km[:, h * LANES:(h + 1) * LANES] + kpe).astype(km_ref.dtype)
    ones_tile = jnp.where(lax.broadcasted_iota(jnp.int32, (BF16_ROWS, x.shape[0]), 0) == 0, 1.0, 0.0)

    def put_values(ref, vt):
        for h in range(vt.shape[0] // HEAD_DIM):
            ref[0, h * V_ROWS:h * V_ROWS + HEAD_DIM, :] = vt[h * HEAD_DIM:(h + 1) * HEAD_DIM].astype(ref.dtype)
            ref[0, h * V_ROWS + HEAD_DIM:(h + 1) * V_ROWS, :] = ones_tile.astype(ref.dtype)

    put_values(vmt_ref, _dot_nt(wvt_ref[...], ckvn))

    nsa_scale = HEAD_DIM ** -0.5 * LOG2E
    for h in range(NSA_HEADS):
        xh = ut[_UT_QN + h * HEAD_DIM:_UT_QN + (h + 1) * HEAD_DIM]
        p1, p2 = _rope_rows(xh[0:hn], xh[hn:2 * hn], cn_r, sn_r)
        qh = jnp.concatenate([p1, p2, xh[2 * hn:]], axis=0) * nsa_scale
        qnt_ref[0, h * HEAD_DIM:(h + 1) * HEAD_DIM, :] = qh.astype(qnt_ref.dtype)

    ks = _nsa_rope(u[:, _U_KSLC:_U_KSLC + LANES], cn_t, sn_t)
    ks_r = pltpu.roll(ks, HEAD_DIM, axis=1)
    ksa_ref[0, :, 0:LANES] = jnp.where(low, ks, eneg_t).astype(ksa_ref.dtype)
    ksa_ref[0, :, LANES:2 * LANES] = jnp.where(low, ks_r, eneg_t).astype(ksa_ref.dtype)

    kw = _nsa_rope(u[:, _U_KWIN:_U_KWIN + LANES], cn_t, sn_t)
    kw_r = pltpu.roll(kw, HEAD_DIM, axis=1)
    kwp_ref[0, :, 0:LANES] = jnp.where(low, kw, 0.0).astype(kwp_ref.dtype)
    kwp_ref[0, :, LANES:2 * LANES] = jnp.where(low, kw_r, 0.0).astype(kwp_ref.dtype)

    put_values(vst_ref, ut[_UT_VSLC:_UT_VSLC + NSA_KV_WIDTH])
    put_values(vwt_ref, ut[_UT_VWIN:_UT_VWIN + NSA_KV_WIDTH])
    gatet_ref[0] = ut[_UT_GATE:_UT_GATE + GATE_ROWS]
    kcv_ref[0, 0] = u[:, _U_KCV:_U_KCV + LANES]
    kcv_ref[0, 1] = u[:, _U_KCV + LANES:_U_KCV + 2 * LANES]


def _proj_call(x, tabs, tabt, gmix, win, wint, gcq, wqt, gckv, wk, wvt):
    B, S, D = x.shape
    rows = PROJ_ROWS
    ns = S // rows
    tok = lambda w: pl.BlockSpec((1, rows, w), lambda i, b: (b, i, 0))
    feat = lambda r: pl.BlockSpec((1, r, rows), lambda i, b: (b, 0, i))
    full = lambda a: pl.BlockSpec(a.shape, lambda i, b: (0,) * a.ndim)
    bf = MXU_DTYPE
    outs = [
        (jax.ShapeDtypeStruct((B, MLA_HEADS * LANES, S), bf), feat(MLA_HEADS * LANES)),
        (jax.ShapeDtypeStruct((B, S, MLA_HEADS * LANES), bf), tok(MLA_HEADS * LANES)),
        (jax.ShapeDtypeStruct((B, MLA_HEADS * V_ROWS, S), bf), feat(MLA_HEADS * V_ROWS)),
        (jax.ShapeDtypeStruct((B, NSA_WIDTH, S), bf), feat(NSA_WIDTH)),
        (jax.ShapeDtypeStruct((B, S, 2 * LANES), bf), tok(2 * LANES)),
        (jax.ShapeDtypeStruct((B, NSA_KV_GROUPS * V_ROWS, S), bf), feat(NSA_KV_GROUPS * V_ROWS)),
        (jax.ShapeDtypeStruct((B, S, 2 * LANES), bf), tok(2 * LANES)),
        (jax.ShapeDtypeStruct((B, NSA_KV_GROUPS * V_ROWS, S), bf), feat(NSA_KV_GROUPS * V_ROWS)),
        (jax.ShapeDtypeStruct((B, 2, S, LANES), jnp.float32),
         pl.BlockSpec((1, 2, rows, LANES), lambda i, b: (b, 0, i, 0))),
        (jax.ShapeDtypeStruct((B, GATE_ROWS, S), jnp.float32), feat(GATE_ROWS)),
    ]
    return pl.pallas_call(
        _proj_kernel,
        grid=(ns, B),
        in_specs=[tok(D), pl.BlockSpec((rows, _N_TABS * LANES), lambda i, b: (i, 0)),
                  pl.BlockSpec((_TT_ROWS, rows), lambda i, b: (0, i)),
                  full(gmix), full(win), full(wint), full(gcq), full(wqt), full(gckv), full(wk), full(wvt)],
        out_specs=[o[1] for o in outs],
        out_shape=[o[0] for o in outs],
        compiler_params=pltpu.CompilerParams(
            dimension_semantics=("arbitrary", "arbitrary"), vmem_limit_bytes=VMEM_LIMIT),
        name="proj",
    )(x, tabs, tabt, gmix, win, wint, gcq, wqt, gckv, wk, wvt)


def _gelu_tanh(x):
    return 0.5 * x * (1.0 + jnp.tanh(math.sqrt(2.0 / math.pi) * (x + 0.044715 * (x * x * x))))


def _compress_kernel(x_ref, pea_ref, peb_ref, wa_ref, wb_ref, b1_ref, w2k_ref, b2k_ref, w2vt_ref, b2vt_ref,
                     ctab_ref, kc_ref, vct_ref):
    nch = kc_ref.shape[1]
    a = b = None
    for t in range(CMP_STRIDE):
        for c in range(2):
            cols = slice((2 * t + c) * LANES, (2 * t + c + 1) * LANES)
            xt = x_ref[0, c, pl.ds(t, nch, stride=CMP_STRIDE), :]
            at = _dot((xt + pea_ref[:, cols]).astype(MXU_DTYPE), wa_ref[cols, :])
            bt = _dot((xt + peb_ref[:, cols]).astype(MXU_DTYPE), wb_ref[cols, :])
            a = at if a is None else a + at
            b = bt if b is None else b + bt
    hid = _gelu_tanh(a + pltpu.roll(b, nch - 1, axis=0) + b1_ref[...]).astype(MXU_DTYPE)
    cc = ctab_ref[:, 0:LANES]
    sc = ctab_ref[:, LANES:2 * LANES]
    for g in range(NSA_KV_GROUPS):
        hk = hid[:, g * LANES:(g + 1) * LANES]
        kc = _dot(hk, w2k_ref[...]) + b2k_ref[...]
        kc_ref[0, :, g * LANES:(g + 1) * LANES] = _nsa_rope(kc, cc, sc).astype(kc_ref.dtype)
        hv = hid[:, (NSA_KV_GROUPS + g) * LANES:(NSA_KV_GROUPS + g + 1) * LANES]
        vct_ref[0, g * HEAD_DIM:(g + 1) * HEAD_DIM, :] = (
            _dot_nt(w2vt_ref[...], hv) + b2vt_ref[...]).astype(vct_ref.dtype)


def _compress_call(kcv, consts, ctab):
    B, _, S, _ = kcv.shape
    nch = S // CMP_STRIDE
    full = lambda a: pl.BlockSpec(a.shape, lambda b: (0,) * a.ndim)
    return pl.pallas_call(
        _compress_kernel,
        grid=(B,),
        in_specs=[pl.BlockSpec((1, 2, S, LANES), lambda b: (b, 0, 0, 0))] + [full(c) for c in consts] + [full(ctab)],
        out_specs=[pl.BlockSpec((1, nch, 2 * LANES), lambda b: (b, 0, 0)),
                   pl.BlockSpec((1, NSA_KV_WIDTH, nch), lambda b: (b, 0, 0))],
        out_shape=[jax.ShapeDtypeStruct((B, nch, 2 * LANES), MXU_DTYPE),
                   jax.ShapeDtypeStruct((B, NSA_KV_WIDTH, nch), MXU_DTYPE)],
        compiler_params=pltpu.CompilerParams(
            dimension_semantics=("arbitrary",), vmem_limit_bytes=VMEM_LIMIT),
        name="compress",
    )(kcv, *consts, ctab)


def _numerators(s, m):
    return jnp.exp2((s - m).astype(EXP_DTYPE)).astype(MXU_DTYPE)


def _normalise(acc):
    return acc[0:HEAD_DIM] * (1.0 / acc[HEAD_DIM:HEAD_DIM + 1])


def _flash_streams(n_streams, scores, pv, lower, i, tq, wide):
    start_d = pl.multiple_of(i * tq, tq)
    carry = []
    for st in range(n_streams):
        s = jnp.where(lower, scores(st, start_d, tq), -jnp.inf)
        m = jnp.max(s, axis=0, keepdims=True)
        carry.append((m, pv(st, _numerators(s, m), start_d, tq)))

    def step(start, width, carry):
        out = []
        for st, (m, acc) in enumerate(carry):
            s = scores(st, start, width)
            m_new = jnp.maximum(m, jnp.max(s, axis=0, keepdims=True))
            out.append((m_new, jnp.exp2(m - m_new) * acc + pv(st, _numerators(s, m_new), start, width)))
        return tuple(out)

    wt = wide * tq
    n_wide = i // wide
    carry = lax.fori_loop(0, n_wide, lambda jj, c: step(pl.multiple_of(jj * wt, wt), wt, c), tuple(carry))
    carry = lax.fori_loop(n_wide * wide, i, lambda jj, c: step(pl.multiple_of(jj * tq, tq), tq, c), carry)
    return [_normalise(acc) for (m, acc) in carry]


def _mla_kernel(qt_ref, k_ref, vt_ref, o_ref):
    i = pl.program_id(2)
    tq = qt_ref.shape[2]
    lower = lax.broadcasted_iota(jnp.int32, (tq, tq), 0) <= lax.broadcasted_iota(jnp.int32, (tq, tq), 1)
    qs = [qt_ref[0, hh * LANES:(hh + 1) * LANES, :] for hh in range(MLA_HPS)]

    def scores(hh, start, width):
        return _dot(k_ref[0, pl.ds(start, width), hh * LANES:(hh + 1) * LANES], qs[hh])

    def pv(hh, p, start, width):
        return _dot(vt_ref[0, hh * V_ROWS:(hh + 1) * V_ROWS, pl.ds(start, width)], p)

    outs = _flash_streams(MLA_HPS, scores, pv, lower, i, tq, MLA_WIDE)
    for hh, o in enumerate(outs):
        o_ref[0, hh * MLA_V_DIM:(hh + 1) * MLA_V_DIM, :] = o


def _mla_call(qmt, km, vmt):
    B, S, _ = km.shape
    tq = MLA_TQ
    hps = MLA_HPS
    return pl.pallas_call(
        _mla_kernel,
        grid=(B, MLA_HEADS // hps, S // tq),
        in_specs=[pl.BlockSpec((1, hps * LANES, tq), lambda b, p, i: (b, p, i)),
                  pl.BlockSpec((1, S, hps * LANES), lambda b, p, i: (b, 0, p)),
                  pl.BlockSpec((1, hps * V_ROWS, S), lambda b, p, i: (b, p, 0))],
        out_specs=pl.BlockSpec((1, hps * MLA_V_DIM, tq), lambda b, p, i: (b, p, i)),
        out_shape=jax.ShapeDtypeStruct((B, MLA_WIDTH, S), jnp.float32),
        compiler_params=pltpu.CompilerParams(
            dimension_semantics=("arbitrary", "arbitrary", "arbitrary"), vmem_limit_bytes=VMEM_LIMIT),
        name="mla",
    )(qmt, km, vmt)


def _group_queries(q_ref, g, rows):
    return jnp.concatenate(
        [q_ref[0, (g * NSA_HPG + h) * rows:(g * NSA_HPG + h + 1) * rows, :] for h in range(NSA_HPG)], axis=1)


def _nsa_cmp_kernel(qt_ref, kc_ref, vct_ref, ovt_ref, oct_ref, qaugt_ref, score_ref):
    i = pl.program_id(1)
    tq = qt_ref.shape[2]
    nch = kc_ref.shape[1]
    nblk = score_ref.shape[1]
    cols4 = NSA_HPG * tq
    q0 = i * tq
    qpos_c = q0 + lax.broadcasted_iota(jnp.int32, (nch, cols4), 1) % tq
    cmp_end = lax.broadcasted_iota(jnp.int32, (nch, cols4), 0) * CMP_STRIDE + (CMP_BLOCK - 1)
    mask_c = cmp_end <= qpos_c

    blk = lax.broadcasted_iota(jnp.int32, (nblk, tq), 0)
    qpos_r = q0 + lax.broadcasted_iota(jnp.int32, (nblk, tq), 1)
    cur = qpos_r // SLC_BLOCK
    forced = (blk == 0) | (blk == cur) | (blk == cur - 1)
    causal = blk * SLC_BLOCK <= qpos_r
    groups = range(NSA_KV_GROUPS)

    scores = []
    for g in groups:
        q4 = _group_queries(qt_ref, g, HEAD_DIM)
        qpad = jnp.concatenate([q4, jnp.zeros_like(q4)], axis=0)
        s = jnp.where(mask_c, _dot(kc_ref[0, :, g * LANES:(g + 1) * LANES], qpad), -jnp.inf)
        m = jnp.max(s, axis=0, keepdims=True)
        m = jnp.where(m == -jnp.inf, 0.0, m)
        e = jnp.exp2(s - m)
        pr = e / jnp.maximum(jnp.sum(e, axis=0, keepdims=True), 1e-30)
        o = _dot(vct_ref[0, g * HEAD_DIM:(g + 1) * HEAD_DIM, :], pr.astype(MXU_DTYPE))
        for h in range(NSA_HPG):
            head = g * NSA_HPG + h
            oct_ref[0, head * HEAD_DIM:(head + 1) * HEAD_DIM, :] = o[:, h * tq:(h + 1) * tq]

        psum = (pr[:, 0:tq] + pr[:, tq:2 * tq]) + (pr[:, 2 * tq:3 * tq] + pr[:, 3 * tq:4 * tq])
        p_hi = psum.astype(MXU_DTYPE)
        p_lo = (psum - p_hi.astype(jnp.float32)).astype(MXU_DTYPE)
        imp = _dot(ovt_ref[...], p_hi) + _dot(ovt_ref[...], p_lo)
        score = jnp.where(forced, FORCE_SCORE, jnp.where(causal, imp, -FORCE_SCORE))
        score_ref[g] = score
        scores.append(score)

    def count(c, cnts):
        out = list(cnts)
        for t in range(RANK_UNROLL):
            r = c * RANK_UNROLL + t
            tie = jnp.where(r < blk, 1.0, 0.0)
            for g in groups:
                row = score_ref[g, pl.ds(r, 1), :]
                out[g] = out[g] + jnp.where(row > scores[g], 1.0, jnp.where(row == scores[g], tie, 0.0))
        return tuple(out)

    zero = jnp.zeros((nblk, tq), jnp.float32)
    cnts = lax.fori_loop(0, i + 1, count, (zero,) * NSA_KV_GROUPS)
    for g in groups:
        notsel = jnp.where(cnts[g] < float(SLC_TOPK), 0.0, 1.0).astype(qaugt_ref.dtype)
        for h in range(NSA_HPG):
            head = g * NSA_HPG + h
            qaugt_ref[0, head * LANES:head * LANES + HEAD_DIM, :] = qt_ref[0, head * HEAD_DIM:(head + 1) * HEAD_DIM, :]
            qaugt_ref[0, head * LANES + HEAD_DIM:(head + 1) * LANES, :] = notsel


def _nsa_cmp_call(qnt, kc, vct, ovt):
    B, _, S = qnt.shape
    tq = NSA_TQ
    nch = kc.shape[1]
    return pl.pallas_call(
        _nsa_cmp_kernel,
        grid=(B, S // tq),
        in_specs=[pl.BlockSpec((1, NSA_WIDTH, tq), lambda b, i: (b, 0, i)),
                  pl.BlockSpec((1, nch, 2 * LANES), lambda b, i: (b, 0, 0)),
                  pl.BlockSpec((1, NSA_KV_WIDTH, nch), lambda b, i: (b, 0, 0)),
                  pl.BlockSpec(ovt.shape, lambda b, i: (0, 0))],
        out_specs=[pl.BlockSpec((1, NSA_WIDTH, tq), lambda b, i: (b, 0, i)),
                   pl.BlockSpec((1, NSA_HEADS * LANES, tq), lambda b, i: (b, 0, i))],
        out_shape=[jax.ShapeDtypeStruct((B, NSA_WIDTH, S), jnp.float32),
                   jax.ShapeDtypeStruct((B, NSA_HEADS * LANES, S), MXU_DTYPE)],
        scratch_shapes=[pltpu.VMEM((NSA_KV_GROUPS, LANES - HEAD_DIM, tq), jnp.float32)],
        compiler_params=pltpu.CompilerParams(
            dimension_semantics=("arbitrary", "arbitrary"), vmem_limit_bytes=VMEM_LIMIT),
        name="nsa_cmp",
    )(qnt, kc, vct, ovt)


def _nsa_attn_kernel(qaugt_ref, ksa_ref, vst_ref, kwp_ref, vwt_ref, oct_ref, gatet_ref, y_ref):
    i = pl.program_id(1)
    tq = qaugt_ref.shape[2]
    cols4 = NSA_HPG * tq
    groups = range(NSA_KV_GROUPS)
    lower = (lax.broadcasted_iota(jnp.int32, (tq, cols4), 0)
             <= lax.broadcasted_iota(jnp.int32, (tq, cols4), 1) % tq)
    qs = [_group_queries(qaugt_ref, g, LANES) for g in groups]

    def pv_from(vt_ref):
        def pv(g, p, start, width):
            return _dot(vt_ref[0, g * V_ROWS:(g + 1) * V_ROWS, pl.ds(start, width)], p)
        return pv

    o_slc = _flash_streams(
        NSA_KV_GROUPS,
        lambda g, start, width: _dot(ksa_ref[0, pl.ds(start, width), g * LANES:(g + 1) * LANES], qs[g]),
        pv_from(vst_ref), lower, i, tq, NSA_WIDE)

    sub = WIN_SUB
    wk = WINDOW + sub
    nsub = tq // sub
    rel = (lax.broadcasted_iota(jnp.int32, (wk, NSA_HPG * sub), 1) % sub
           - lax.broadcasted_iota(jnp.int32, (wk, NSA_HPG * sub), 0))
    pv_w = pv_from(vwt_ref)
    o_win = [[None] * nsub for _ in groups]
    for j in range(nsub):
        q_lo = i * tq + j * sub
        wstart = pl.multiple_of(jnp.maximum(q_lo - WINDOW, 0), sub)
        dist = rel + (q_lo - wstart)
        band = (dist >= 0) & (dist < WINDOW)
        for g in groups:
            q_sub = jnp.concatenate(
                [qs[g][:, h * tq + j * sub:h * tq + (j + 1) * sub] for h in range(NSA_HPG)], axis=1)
            s = jnp.where(band, _dot(kwp_ref[0, pl.ds(wstart, wk), g * LANES:(g + 1) * LANES], q_sub), -jnp.inf)
            p = _numerators(s, jnp.max(s, axis=0, keepdims=True))
            o_win[g][j] = _normalise(pv_w(g, p, wstart, wk))

    sig = jax.nn.sigmoid(gatet_ref[0])
    for g in groups:
        for h in range(NSA_HPG):
            head = g * NSA_HPG + h
            o_w = jnp.concatenate([o_win[g][j][:, h * sub:(h + 1) * sub] for j in range(nsub)], axis=1)
            gate = lambda br: sig[br * NSA_HEADS + head:br * NSA_HEADS + head + 1, :]
            y_ref[0, head * HEAD_DIM:(head + 1) * HEAD_DIM, :] = (
                gate(0) * oct_ref[0, head * HEAD_DIM:(head + 1) * HEAD_DIM, :]
                + gate(1) * o_slc[g][:, h * tq:(h + 1) * tq] + gate(2) * o_w)


def _nsa_attn_call(qaugt, ksa, vst, kwp, vwt, oct, gatet):
    B, S, _ = ksa.shape
    tq = NSA_TQ
    feat = lambda r: pl.BlockSpec((1, r, tq), lambda b, i: (b, 0, i))
    seq_tok = lambda w: pl.BlockSpec((1, S, w), lambda b, i: (b, 0, 0))
    seq_feat = lambda r: pl.BlockSpec((1, r, S), lambda b, i: (b, 0, 0))
    return pl.pallas_call(
        _nsa_attn_kernel,
        grid=(B, S // tq),
        in_specs=[feat(NSA_HEADS * LANES), seq_tok(2 * LANES), seq_feat(NSA_KV_GROUPS * V_ROWS), seq_tok(2 * LANES),
                  seq_feat(NSA_KV_GROUPS * V_ROWS), feat(NSA_WIDTH), feat(GATE_ROWS)],
        out_specs=feat(NSA_WIDTH),
        out_shape=jax.ShapeDtypeStruct((B, NSA_WIDTH, S), jnp.float32),
        compiler_params=pltpu.CompilerParams(
            dimension_semantics=("arbitrary", "arbitrary"), vmem_limit_bytes=VMEM_LIMIT),
        name="nsa_attn",
    )(qaugt, ksa, vst, kwp, vwt, oct, gatet)


def _rms_cols(yt, g_col):
    return yt * lax.rsqrt(jnp.mean(yt * yt, axis=0, keepdims=True) + NORM_EPS) * g_col


def _out_kernel(x_ref, ymt_ref, ynt_ref, gm_ref, gn_ref, wo_ref, gmlp_ref, wup_ref, wdn_ref, gfin_ref, o_ref):
    mixed_t = jnp.concatenate([_rms_cols(ymt_ref[0], gm_ref[...]), _rms_cols(ynt_ref[0], gn_ref[...])], axis=0)
    h = x_ref[0] + _dot_tn(mixed_t.astype(MXU_DTYPE), wo_ref[...])
    hn = _rms(h, gmlp_ref[...]).astype(MXU_DTYPE)
    acc = h
    for c in range(D_FF // FF_CHUNK):
        a = jnp.maximum(_dot(hn, wup_ref[:, c * FF_CHUNK:(c + 1) * FF_CHUNK]), 0.0)
        acc = acc + _dot((a * a).astype(MXU_DTYPE), wdn_ref[c * FF_CHUNK:(c + 1) * FF_CHUNK, :])
    o_ref[0] = _rms(acc, gfin_ref[...])


def _out_call(x, ymt, ynt, gm, gn, wo, gmlp, wup, wdn, gfin):
    B, S, D = x.shape
    rows = OUT_ROWS
    tok = pl.BlockSpec((1, rows, D), lambda b, i: (b, i, 0))
    feat = lambda r: pl.BlockSpec((1, r, rows), lambda b, i: (b, 0, i))
    full = lambda a: pl.BlockSpec(a.shape, lambda b, i: (0,) * a.ndim)
    return pl.pallas_call(
        _out_kernel,
        grid=(B, S // rows),
        in_specs=[tok, feat(MLA_WIDTH), feat(NSA_WIDTH), full(gm), full(gn), full(wo), full(gmlp),
                  full(wup), full(wdn), full(gfin)],
        out_specs=tok,
        out_shape=jax.ShapeDtypeStruct((B, S, D), jnp.float32),
        compiler_params=pltpu.CompilerParams(
            dimension_semantics=("arbitrary", "arbitrary"), vmem_limit_bytes=VMEM_LIMIT),
        name="out",
    )(x, ymt, ynt, gm, gn, wo, gmlp, wup, wdn, gfin)


def _rope_cs(pos, dim):
    inv_freq = jnp.exp(-math.log(ROPE_THETA) * jnp.arange(0, dim, 2, dtype=jnp.float32) / dim)
    ang = pos.astype(jnp.float32)[:, None] * inv_freq[None, :]
    return jnp.cos(ang), jnp.sin(ang)


def _nsa_rope_tables(pos, width):
    c, s = _rope_cs(pos, NSA_ROPE_DIM)
    n = pos.shape[0]
    rest = HEAD_DIM - NSA_ROPE_DIM
    ch = jnp.concatenate([c, c, jnp.ones((n, rest), jnp.float32)], axis=1)
    sh = jnp.concatenate([-s, s, jnp.zeros((n, rest), jnp.float32)], axis=1)
    reps = width // HEAD_DIM
    return jnp.tile(ch, (1, reps)), jnp.tile(sh, (1, reps))


def _proj_tables(S):
    pos = jnp.arange(S)
    c, s = _rope_cs(pos, MLA_ROPE_DIM)
    z = lambda w: jnp.zeros((S, w), jnp.float32)
    pad = LANES - MLA_NOPE_DIM - MLA_ROPE_DIM
    ck = jnp.concatenate([z(MLA_NOPE_DIM), c, c, z(pad)], axis=1)
    sk = jnp.concatenate([z(MLA_NOPE_DIM), -s, s, z(pad)], axis=1)
    cn, sn = _nsa_rope_tables(pos, LANES)
    onehot = (pos[:, None] // SLC_BLOCK == jnp.arange(LANES - HEAD_DIM)[None, :]).astype(jnp.float32)
    eneg = jnp.concatenate([z(HEAD_DIM), MASK_BIAS * onehot], axis=1)
    cn8, sn8 = _rope_cs(pos, NSA_ROPE_DIM)
    tabt = jnp.concatenate([c.T, s.T, cn8.T, sn8.T], axis=0)
    return jnp.concatenate([ck, sk, cn, sn, eneg], axis=1), tabt


def _compress_tables(nch):
    end = jnp.arange(nch) * CMP_STRIDE + CMP_BLOCK - 1
    c, s = _nsa_rope_tables(end, HEAD_DIM)
    z = jnp.zeros((nch, LANES - HEAD_DIM), jnp.float32)
    return jnp.concatenate([c, z, s, z], axis=1)


def _overlap_t(nch, nblk):
    cs = jnp.arange(nch)[None, :] * CMP_STRIDE
    ss = jnp.arange(nblk)[:, None] * SLC_BLOCK
    ov = jnp.clip(jnp.minimum(cs + CMP_BLOCK, ss + SLC_BLOCK) - jnp.maximum(cs, ss), 0, None)
    ov = ov.astype(jnp.float32) / CMP_BLOCK
    ov = jnp.where(jnp.arange(nch)[None, :] < nch - 1, ov, 0.0)
    return ov.astype(MXU_DTYPE)


def _pad_cols(w, width):
    return jnp.pad(w, ((0, 0), (0, width - w.shape[1])))


def _compress_consts(pe_k, w1_k, b1_k, w2_k, b2_k, pe_v, w1_v, b1_v, w2_v, b2_v):
    half = CMP_BLOCK // 2
    ncol = 2 * NSA_KV_GROUPS
    w1 = jnp.stack([w1_k, w1_k, w1_v, w1_v]).reshape(ncol, 2, half, HEAD_DIM, CMP_HIDDEN)
    eye = jnp.eye(ncol, dtype=w1.dtype)

    def spread(wh):
        return jnp.einsum('cldh,ce->lcdeh', wh, eye).reshape(half * ncol * HEAD_DIM, ncol * CMP_HIDDEN)

    pe = jnp.stack([pe_k, pe_k, pe_v, pe_v]).reshape(ncol, 2, half, HEAD_DIM)
    pe_row = lambda ph: ph.transpose(1, 0, 2).reshape(1, half * ncol * HEAD_DIM)
    b1 = jnp.concatenate([b1_k, b1_k, b1_v, b1_v]).reshape(1, ncol * CMP_HIDDEN)
    return [
        pe_row(pe[:, 0]), pe_row(pe[:, 1]),
        spread(w1[:, 0]).astype(MXU_DTYPE), spread(w1[:, 1]).astype(MXU_DTYPE), b1,
        _pad_cols(w2_k, LANES).astype(MXU_DTYPE), _pad_cols(b2_k[None, :], LANES),
        w2_v.T.astype(MXU_DTYPE), b2_v[:, None],
    ]


def _proj_weights(w_in, w_uq, w_ukv):
    d = w_in.shape[0]
    z = lambda w: jnp.zeros((d, w), w_in.dtype)
    o_kr = MLA_Q_RANK + MLA_KV_RANK
    o_qn = o_kr + MLA_ROPE_DIM
    o_kv = o_qn + NSA_WIDTH
    o_gate = o_kv + 6 * NSA_KV_WIDTH
    kv = lambda t: w_in[:, o_kv + t * NSA_KV_WIDTH:o_kv + (t + 1) * NSA_KV_WIDTH]
    win = jnp.concatenate([
        w_in[:, :o_kr],
        z(MLA_NOPE_DIM), w_in[:, o_kr:o_qn], z(LANES - MLA_NOPE_DIM - MLA_ROPE_DIM),
        kv(0), kv(1), kv(2), kv(4)], axis=1)
    assert win.shape[1] == _U_COLS
    wint = jnp.concatenate([w_in[:, o_qn:o_kv], kv(3), kv(5), w_in[:, o_gate:], z(GATE_ROWS - NSA_GATES)], axis=1).T
    assert wint.shape[0] == _UT_ROWS
    qd = MLA_NOPE_DIM + MLA_ROPE_DIM
    wq = jnp.pad(w_uq.reshape(MLA_Q_RANK, MLA_HEADS, qd), ((0, 0), (0, 0), (0, LANES - qd)))
    wqt = wq.reshape(MLA_Q_RANK, MLA_HEADS * LANES).T
    kvw = w_ukv.reshape(MLA_KV_RANK, MLA_HEADS, MLA_NOPE_DIM + MLA_V_DIM)
    zk = jnp.zeros((MLA_KV_RANK, MLA_HEADS, LANES - MLA_NOPE_DIM), w_ukv.dtype)
    wk = jnp.concatenate([kvw[..., :MLA_NOPE_DIM], zk], axis=-1).reshape(MLA_KV_RANK, MLA_HEADS * LANES)
    wvt = kvw[..., MLA_NOPE_DIM:].reshape(MLA_KV_RANK, MLA_WIDTH).T
    return (win.astype(MXU_DTYPE), wint.astype(MXU_DTYPE), wqt.astype(MXU_DTYPE), wk.astype(MXU_DTYPE),
            wvt.astype(MXU_DTYPE))


def kernel(x, g_mix_norm, w_in, g_cq, w_uq, g_ckv, w_ukv, cmp_pe_k, cmp_w1_k, cmp_b1_k, cmp_w2_k, cmp_b2_k,
           cmp_pe_v, cmp_w1_v, cmp_b1_v, cmp_w2_v, cmp_b2_v, g_out_mla, g_out_nsa, w_o, g_mlp_norm, w_up,
           w_down, g_final):
    B, S, D = x.shape
    nch = S // CMP_STRIDE
    nblk = S // SLC_BLOCK
    assert w_in.shape[0] == 1
    assert D == D_MODEL and S % MLA_TQ == 0 and S % PROJ_ROWS == 0 and S % OUT_ROWS == 0
    assert S % NSA_TQ == 0 and NSA_TQ % WIN_SUB == 0 and WINDOW % WIN_SUB == 0 and S >= WINDOW + WIN_SUB
    assert SLC_TOPK <= nblk <= LANES - HEAD_DIM
    row = lambda g: g.reshape(1, -1)
    col = lambda g: g.reshape(-1, 1)

    tabs, tabt = _proj_tables(S)
    ctab = _compress_tables(nch)
    ovt = _overlap_t(nch, LANES - HEAD_DIM)

    win, wint, wqt, wk, wvt = _proj_weights(w_in[0], w_uq[0], w_ukv[0])
    qmt, km, vmt, qnt, ksa, vst, kwp, vwt, kcv, gatet = _proj_call(
        x, tabs, tabt, row(g_mix_norm[0]), win, wint, row(g_cq[0]), wqt, row(g_ckv[0]), wk, wvt)
    consts = _compress_consts(cmp_pe_k[0], cmp_w1_k[0], cmp_b1_k[0], cmp_w2_k[0], cmp_b2_k[0],
                              cmp_pe_v[0], cmp_w1_v[0], cmp_b1_v[0], cmp_w2_v[0], cmp_b2_v[0])
    kc, vct = _compress_call(kcv, consts, ctab)
    y_mla_t = _mla_call(qmt, km, vmt)
    oct, qaugt = _nsa_cmp_call(qnt, kc, vct, ovt)
    y_nsa_t = _nsa_attn_call(qaugt, ksa, vst, kwp, vwt, oct, gatet)
    return _out_call(x, y_mla_t, y_nsa_t, col(g_out_mla[0]), col(g_out_nsa[0]), w_o[0].astype(MXU_DTYPE),
                     row(g_mlp_norm[0]), w_up[0].astype(MXU_DTYPE), w_down[0].astype(MXU_DTYPE), row(g_final))
```

```python
import math

import jax
import jax.numpy as jnp
from jax import lax
from jax.experimental import pallas as pl
from jax.experimental.pallas import tpu as pltpu

D_MODEL = 1024
HEAD_DIM = 64
ROPE_THETA = 500000.0
NORM_EPS = 1e-6

MLA_HEADS = 8
MLA_Q_RANK = 256
MLA_KV_RANK = 128
MLA_NOPE_DIM = 64
MLA_ROPE_DIM = 32
MLA_V_DIM = 64
MLA_WIDTH = MLA_HEADS * MLA_V_DIM

NSA_HEADS = 8
NSA_KV_GROUPS = 2
NSA_HPG = NSA_HEADS // NSA_KV_GROUPS
NSA_ROPE_DIM = HEAD_DIM // 4
NSA_WIDTH = NSA_HEADS * HEAD_DIM
NSA_KV_WIDTH = NSA_KV_GROUPS * HEAD_DIM
NSA_GATES = 3 * NSA_HEADS
CMP_BLOCK = 32
CMP_STRIDE = 16
CMP_HIDDEN = 2 * HEAD_DIM
SLC_BLOCK = 64
SLC_TOPK = 16
WINDOW = 512
FORCE_SCORE = 1e9
D_FF = 4 * D_MODEL

LANES = 128
SUBLANES = 8
VMEM_LIMIT = 56 * 1024 * 1024

MXU_DTYPE = jnp.bfloat16
PROJ_ROWS = 512
MLA_TQ = 1024
MLA_HPS = 4
MLA_WIDE = 2
NSA_TQ = 512
NSA_WIDE = 2
WIN_SUB = 128
RANK_UNROLL = NSA_TQ // SLC_BLOCK
OUT_ROWS = 512
FF_CHUNK = 1024
LOG2E = math.log2(math.e)
LAZY_MAX_SLACK = 64.0
MASK_BIAS = -1e30
GATE_ROWS = 32
BF16_ROWS = 16
V_ROWS = HEAD_DIM + BF16_ROWS
EXP_DTYPE = jnp.float32

_U_CQ = 0
_U_CKV = 256
_U_KPE = 384
_U_KCV = 512
_U_KSLC = 768
_U_KWIN = 896
_U_COLS = 1024
_N_TABS = 5
_UT_QN = 0
_UT_VSLC = 512
_UT_VWIN = 640
_UT_GATE = 768
_UT_ROWS = 800
_TT_ROWS = MLA_ROPE_DIM + NSA_ROPE_DIM


def _dot(a, b):
    return jnp.dot(a, b, preferred_element_type=jnp.float32)


def _dot_nt(a, b):
    return lax.dot_general(a, b, (((1,), (1,)), ((), ())), preferred_element_type=jnp.float32)


def _dot_tn(a, b):
    return lax.dot_general(a, b, (((0,), (0,)), ((), ())), preferred_element_type=jnp.float32)


def _rms(x, g):
    return x * lax.rsqrt(jnp.mean(x * x, axis=-1, keepdims=True) + NORM_EPS) * g


def _lane(shape):
    return lax.broadcasted_iota(jnp.int32, shape, len(shape) - 1)


def _swap_pairs(x, lo_end, width):
    n = x.shape[-1]
    lane = _lane(x.shape) % LANES
    return jnp.where(lane < lo_end, pltpu.roll(x, n - width, axis=1), pltpu.roll(x, width, axis=1))


def _nsa_rope(x, cn, sn):
    half = NSA_ROPE_DIM // 2
    lane = _lane(x.shape) % HEAD_DIM
    sw = jnp.where(lane < half, pltpu.roll(x, LANES - half, axis=1), pltpu.roll(x, half, axis=1))
    return x * cn + sw * sn


def _rope_rows(x1, x2, c, s):
    return x1 * c - x2 * s, x2 * c + x1 * s


def _proj_kernel(x_ref, tab_ref, tabt_ref, gmix_ref, win_ref, wint_ref, gcq_ref, wqt_ref, gckv_ref, wk_ref, wvt_ref,
                 qmt_ref, km_ref, vmt_ref, qnt_ref, ksa_ref, vst_ref, kwp_ref, vwt_ref, kcv_ref, gatet_ref):
    x = x_ref[0]
    n = _rms(x, gmix_ref[...]).astype(MXU_DTYPE)
    u = _dot(n, win_ref[...])
    ut = _dot_nt(wint_ref[...], n)

    ck_t, sk_t, cn_t, sn_t, eneg_t = (tab_ref[:, t * LANES:(t + 1) * LANES] for t in range(_N_TABS))
    hm = MLA_ROPE_DIM // 2
    hn = NSA_ROPE_DIM // 2
    cm_r = tabt_ref[0:hm, :]
    sm_r = tabt_ref[hm:2 * hm, :]
    cn_r = tabt_ref[2 * hm:2 * hm + hn, :]
    sn_r = tabt_ref[2 * hm + hn:2 * hm + 2 * hn, :]
    low = _lane((x.shape[0], LANES)) < HEAD_DIM

    mla_scale = (MLA_NOPE_DIM + MLA_ROPE_DIM) ** -0.5 * LOG2E
    cqn = _rms(u[:, _U_CQ:_U_CQ + MLA_Q_RANK], gcq_ref[...]).astype(MXU_DTYPE)
    qmt = _dot_nt(wqt_ref[...], cqn)
    r0 = MLA_NOPE_DIM
    for h in range(MLA_HEADS):
        xh = qmt[h * LANES:(h + 1) * LANES]
        p1, p2 = _rope_rows(xh[r0:r0 + hm], xh[r0 + hm:r0 + 2 * hm], cm_r, sm_r)
        qh = jnp.concatenate([xh[0:r0], p1, p2, xh[r0 + 2 * hm:]], axis=0) * mla_scale
        qmt_ref[0, h * LANES:(h + 1) * LANES, :] = qh.astype(qmt_ref.dtype)

    ckvn = _rms(u[:, _U_CKV:_U_CKV + MLA_KV_RANK], gckv_ref[...]).astype(MXU_DTYPE)
    km = _dot(ckvn, wk_ref[...])
    kpe = u[:, _U_KPE:_U_KPE + LANES]
    kpe = kpe * ck_t + _swap_pairs(kpe, MLA_NOPE_DIM + hm, hm) * sk_t
    for h in range(MLA_HEADS):
        km_ref[0, :, h * LANES:(h + 1) * LANES] = (km[:, h * LANES:(h + 1) * LANES] + kpe).astype(km_ref.dtype)
    ones_tile = jnp.where(lax.broadcasted_iota(jnp.int32, (BF16_ROWS, x.shape[0]), 0) == 0, 1.0, 0.0)

    def put_values(ref, vt):
        for h in range(vt.shape[0] // HEAD_DIM):
            ref[0, h * V_ROWS:h * V_ROWS + HEAD_DIM, :] = vt[h * HEAD_DIM:(h + 1) * HEAD_DIM].astype(ref.dtype)
            ref[0, h * V_ROWS + HEAD_DIM:(h + 1) * V_ROWS, :] = ones_tile.astype(ref.dtype)

    put_values(vmt_ref, _dot_nt(wvt_ref[...], ckvn))

    nsa_scale = HEAD_DIM ** -0.5 * LOG2E
    for h in range(NSA_HEADS):
        xh = ut[_UT_QN + h * HEAD_DIM:_UT_QN + (h + 1) * HEAD_DIM]
        p1, p2 = _rope_rows(xh[0:hn], xh[hn:2 * hn], cn_r, sn_r)
        qh = jnp.concatenate([p1, p2, xh[2 * hn:]], axis=0) * nsa_scale
        qnt_ref[0, h * HEAD_DIM:(h + 1) * HEAD_DIM, :] = qh.astype(qnt_ref.dtype)

    ks = _nsa_rope(u[:, _U_KSLC:_U_KSLC + LANES], cn_t, sn_t)
    ks_r = pltpu.roll(ks, HEAD_DIM, axis=1)
    ksa_ref[0, :, 0:LANES] = jnp.where(low, ks, eneg_t).astype(ksa_ref.dtype)
    ksa_ref[0, :, LANES:2 * LANES] = jnp.where(low, ks_r, eneg_t).astype(ksa_ref.dtype)

    kw = _nsa_rope(u[:, _U_KWIN:_U_KWIN + LANES], cn_t, sn_t)
    kw_r = pltpu.roll(kw, HEAD_DIM, axis=1)
    kwp_ref[0, :, 0:LANES] = jnp.where(low, kw, 0.0).astype(kwp_ref.dtype)
    kwp_ref[0, :, LANES:2 * LANES] = jnp.where(low, kw_r, 0.0).astype(kwp_ref.dtype)

    put_values(vst_ref, ut[_UT_VSLC:_UT_VSLC + NSA_KV_WIDTH])
    put_values(vwt_ref, ut[_UT_VWIN:_UT_VWIN + NSA_KV_WIDTH])
    gatet_ref[0] = ut[_UT_GATE:_UT_GATE + GATE_ROWS]
    kcv_ref[0, 0] = u[:, _U_KCV:_U_KCV + LANES]
    kcv_ref[0, 1] = u[:, _U_KCV + LANES:_U_KCV + 2 * LANES]


def _proj_call(x, tabs, tabt, gmix, win, wint, gcq, wqt, gckv, wk, wvt):
    B, S, D = x.shape
    rows = PROJ_ROWS
    ns = S // rows
    tok = lambda w: pl.BlockSpec((1, rows, w), lambda i, b: (b, i, 0))
    feat = lambda r: pl.BlockSpec((1, r, rows), lambda i, b: (b, 0, i))
    full = lambda a: pl.BlockSpec(a.shape, lambda i, b: (0,) * a.ndim)
    bf = MXU_DTYPE
    outs = [
        (jax.ShapeDtypeStruct((B, MLA_HEADS * LANES, S), bf), feat(MLA_HEADS * LANES)),
        (jax.ShapeDtypeStruct((B, S, MLA_HEADS * LANES), bf), tok(MLA_HEADS * LANES)),
        (jax.ShapeDtypeStruct((B, MLA_HEADS * V_ROWS, S), bf), feat(MLA_HEADS * V_ROWS)),
        (jax.ShapeDtypeStruct((B, NSA_WIDTH, S), bf), feat(NSA_WIDTH)),
        (jax.ShapeDtypeStruct((B, S, 2 * LANES), bf), tok(2 * LANES)),
        (jax.ShapeDtypeStruct((B, NSA_KV_GROUPS * V_ROWS, S), bf), feat(NSA_KV_GROUPS * V_ROWS)),
        (jax.ShapeDtypeStruct((B, S, 2 * LANES), bf), tok(2 * LANES)),
        (jax.ShapeDtypeStruct((B, NSA_KV_GROUPS * V_ROWS, S), bf), feat(NSA_KV_GROUPS * V_ROWS)),
        (jax.ShapeDtypeStruct((B, 2, S, LANES), jnp.float32),
         pl.BlockSpec((1, 2, rows, LANES), lambda i, b: (b, 0, i, 0))),
        (jax.ShapeDtypeStruct((B, GATE_ROWS, S), jnp.float32), feat(GATE_ROWS)),
    ]
    return pl.pallas_call(
        _proj_kernel,
        grid=(ns, B),
        in_specs=[tok(D), pl.BlockSpec((rows, _N_TABS * LANES), lambda i, b: (i, 0)),
                  pl.BlockSpec((_TT_ROWS, rows), lambda i, b: (0, i)),
                  full(gmix), full(win), full(wint), full(gcq), full(wqt), full(gckv), full(wk), full(wvt)],
        out_specs=[o[1] for o in outs],
        out_shape=[o[0] for o in outs],
        compiler_params=pltpu.CompilerParams(
            dimension_semantics=("arbitrary", "arbitrary"), vmem_limit_bytes=VMEM_LIMIT),
        name="proj",
    )(x, tabs, tabt, gmix, win, wint, gcq, wqt, gckv, wk, wvt)


def _gelu_tanh(x):
    return 0.5 * x * (1.0 + jnp.tanh(math.sqrt(2.0 / math.pi) * (x + 0.044715 * (x * x * x))))


def _compress_kernel(x_ref, pea_ref, peb_ref, wa_ref, wb_ref, b1_ref, w2k_ref, b2k_ref, w2vt_ref, b2vt_ref,
                     ctab_ref, kc_ref, vct_ref):
    nch = kc_ref.shape[1]
    a = b = None
    for t in range(CMP_STRIDE):
        for c in range(2):
            cols = slice((2 * t + c) * LANES, (2 * t + c + 1) * LANES)
            xt = x_ref[0, c, pl.ds(t, nch, stride=CMP_STRIDE), :]
            at = _dot((xt + pea_ref[:, cols]).astype(MXU_DTYPE), wa_ref[cols, :])
            bt = _dot((xt + peb_ref[:, cols]).astype(MXU_DTYPE), wb_ref[cols, :])
            a = at if a is None else a + at
            b = bt if b is None else b + bt
    hid = _gelu_tanh(a + pltpu.roll(b, nch - 1, axis=0) + b1_ref[...]).astype(MXU_DTYPE)
    cc = ctab_ref[:, 0:LANES]
    sc = ctab_ref[:, LANES:2 * LANES]
    for g in range(NSA_KV_GROUPS):
        hk = hid[:, g * LANES:(g + 1) * LANES]
        kc = _dot(hk, w2k_ref[...]) + b2k_ref[...]
        kc_ref[0, :, g * LANES:(g + 1) * LANES] = _nsa_rope(kc, cc, sc).astype(kc_ref.dtype)
        hv = hid[:, (NSA_KV_GROUPS + g) * LANES:(NSA_KV_GROUPS + g + 1) * LANES]
        vct_ref[0, g * HEAD_DIM:(g + 1) * HEAD_DIM, :] = (
            _dot_nt(w2vt_ref[...], hv) + b2vt_ref[...]).astype(vct_ref.dtype)


def _compress_call(kcv, consts, ctab):
    B, _, S, _ = kcv.shape
    nch = S // CMP_STRIDE
    full = lambda a: pl.BlockSpec(a.shape, lambda b: (0,) * a.ndim)
    return pl.pallas_call(
        _compress_kernel,
        grid=(B,),
        in_specs=[pl.BlockSpec((1, 2, S, LANES), lambda b: (b, 0, 0, 0))] + [full(c) for c in consts] + [full(ctab)],
        out_specs=[pl.BlockSpec((1, nch, 2 * LANES), lambda b: (b, 0, 0)),
                   pl.BlockSpec((1, NSA_KV_WIDTH, nch), lambda b: (b, 0, 0))],
        out_shape=[jax.ShapeDtypeStruct((B, nch, 2 * LANES), MXU_DTYPE),
                   jax.ShapeDtypeStruct((B, NSA_KV_WIDTH, nch), MXU_DTYPE)],
        compiler_params=pltpu.CompilerParams(
            dimension_semantics=("arbitrary",), vmem_limit_bytes=VMEM_LIMIT),
        name="compress",
    )(kcv, *consts, ctab)


def _numerators(s, m):
    return jnp.exp2((s - m).astype(EXP_DTYPE)).astype(MXU_DTYPE)


def _normalise(acc):
    return acc[0:HEAD_DIM] * (1.0 / acc[HEAD_DIM:HEAD_DIM + 1])


def _flash_streams(n_streams, scores, pv, lower, i, tq, wide):
    start_d = pl.multiple_of(i * tq, tq)
    carry = []
    for st in range(n_streams):
        s = jnp.where(lower, scores(st, start_d, tq), -jnp.inf)
        m = jnp.max(s, axis=0, keepdims=True)
        carry.append((m, pv(st, _numerators(s, m), start_d, tq)))

    def step(start, width, carry):
        fast, tile_max, excess = [], [], None
        for st, (m, acc) in enumerate(carry):
            s = scores(st, start, width)
            tmax = jnp.max(s, axis=0, keepdims=True)
            fast.append((m, acc + pv(st, _numerators(s, m), start, width)))
            tile_max.append(tmax)
            gap = jnp.max(tmax - m)
            excess = gap if excess is None else jnp.maximum(excess, gap)

        def exact(_):
            out = []
            for st, (m, acc) in enumerate(carry):
                m_new = jnp.maximum(m, tile_max[st])
                p = _numerators(scores(st, start, width), m_new)
                out.append((m_new, jnp.exp2(m - m_new) * acc + pv(st, p, start, width)))
            return tuple(out)

        return lax.cond(excess > LAZY_MAX_SLACK, exact, lambda _: tuple(fast), None)

    wt = wide * tq
    n_wide = i // wide
    carry = lax.fori_loop(0, n_wide, lambda jj, c: step(pl.multiple_of(jj * wt, wt), wt, c), tuple(carry))
    carry = lax.fori_loop(n_wide * wide, i, lambda jj, c: step(pl.multiple_of(jj * tq, tq), tq, c), carry)
    return [_normalise(acc) for (m, acc) in carry]


def _mla_kernel(qt_ref, k_ref, vt_ref, o_ref):
    i = pl.program_id(2)
    tq = qt_ref.shape[2]
    lower = lax.broadcasted_iota(jnp.int32, (tq, tq), 0) <= lax.broadcasted_iota(jnp.int32, (tq, tq), 1)
    qs = [qt_ref[0, hh * LANES:(hh + 1) * LANES, :] for hh in range(MLA_HPS)]

    def scores(hh, start, width):
        return _dot(k_ref[0, pl.ds(start, width), hh * LANES:(hh + 1) * LANES], qs[hh])

    def pv(hh, p, start, width):
        return _dot(vt_ref[0, hh * V_ROWS:(hh + 1) * V_ROWS, pl.ds(start, width)], p)

    outs = _flash_streams(MLA_HPS, scores, pv, lower, i, tq, MLA_WIDE)
    for hh, o in enumerate(outs):
        o_ref[0, hh * MLA_V_DIM:(hh + 1) * MLA_V_DIM, :] = o


def _mla_call(qmt, km, vmt):
    B, S, _ = km.shape
    tq = MLA_TQ
    hps = MLA_HPS
    return pl.pallas_call(
        _mla_kernel,
        grid=(B, MLA_HEADS // hps, S // tq),
        in_specs=[pl.BlockSpec((1, hps * LANES, tq), lambda b, p, i: (b, p, i)),
                  pl.BlockSpec((1, S, hps * LANES), lambda b, p, i: (b, 0, p)),
                  pl.BlockSpec((1, hps * V_ROWS, S), lambda b, p, i: (b, p, 0))],
        out_specs=pl.BlockSpec((1, hps * MLA_V_DIM, tq), lambda b, p, i: (b, p, i)),
        out_shape=jax.ShapeDtypeStruct((B, MLA_WIDTH, S), jnp.float32),
        compiler_params=pltpu.CompilerParams(
            dimension_semantics=("arbitrary", "arbitrary", "arbitrary"), vmem_limit_bytes=VMEM_LIMIT),
        name="mla",
    )(qmt, km, vmt)


def _group_queries(q_ref, g, rows):
    return jnp.concatenate(
        [q_ref[0, (g * NSA_HPG + h) * rows:(g * NSA_HPG + h + 1) * rows, :] for h in range(NSA_HPG)], axis=1)


def _nsa_cmp_kernel(qt_ref, kc_ref, vct_ref, ovt_ref, oct_ref, qaugt_ref, score_ref):
    i = pl.program_id(1)
    tq = qt_ref.shape[2]
    nch = kc_ref.shape[1]
    nblk = score_ref.shape[1]
    cols4 = NSA_HPG * tq
    q0 = i * tq
    qpos_c = q0 + lax.broadcasted_iota(jnp.int32, (nch, cols4), 1) % tq
    cmp_end = lax.broadcasted_iota(jnp.int32, (nch, cols4), 0) * CMP_STRIDE + (CMP_BLOCK - 1)
    mask_c = cmp_end <= qpos_c

    blk = lax.broadcasted_iota(jnp.int32, (nblk, tq), 0)
    qpos_r = q0 + lax.broadcasted_iota(jnp.int32, (nblk, tq), 1)
    cur = qpos_r // SLC_BLOCK
    forced = (blk == 0) | (blk == cur) | (blk == cur - 1)
    causal = blk * SLC_BLOCK <= qpos_r
    groups = range(NSA_KV_GROUPS)

    scores = []
    for g in groups:
        q4 = _group_queries(qt_ref, g, HEAD_DIM)
        qpad = jnp.concatenate([q4, jnp.zeros_like(q4)], axis=0)
        s = jnp.where(mask_c, _dot(kc_ref[0, :, g * LANES:(g + 1) * LANES], qpad), -jnp.inf)
        m = jnp.max(s, axis=0, keepdims=True)
        m = jnp.where(m == -jnp.inf, 0.0, m)
        e = jnp.exp2(s - m)
        pr = e / jnp.maximum(jnp.sum(e, axis=0, keepdims=True), 1e-30)
        o = _dot(vct_ref[0, g * HEAD_DIM:(g + 1) * HEAD_DIM, :], pr.astype(MXU_DTYPE))
        for h in range(NSA_HPG):
            head = g * NSA_HPG + h
            oct_ref[0, head * HEAD_DIM:(head + 1) * HEAD_DIM, :] = o[:, h * tq:(h + 1) * tq]

        psum = (pr[:, 0:tq] + pr[:, tq:2 * tq]) + (pr[:, 2 * tq:3 * tq] + pr[:, 3 * tq:4 * tq])
        p_hi = psum.astype(MXU_DTYPE)
        p_lo = (psum - p_hi.astype(jnp.float32)).astype(MXU_DTYPE)
        imp = _dot(ovt_ref[...], p_hi) + _dot(ovt_ref[...], p_lo)
        score = jnp.where(forced, FORCE_SCORE, jnp.where(causal, imp, -FORCE_SCORE))
        score_ref[g] = score
        scores.append(score)

    def count(c, cnts):
        out = list(cnts)
        for t in range(RANK_UNROLL):
            r = c * RANK_UNROLL + t
            tie = jnp.where(r < blk, 1.0, 0.0)
            for g in groups:
                row = score_ref[g, pl.ds(r, 1), :]
                out[g] = out[g] + jnp.where(row > scores[g], 1.0, jnp.where(row == scores[g], tie, 0.0))
        return tuple(out)

    zero = jnp.zeros((nblk, tq), jnp.float32)
    cnts = lax.fori_loop(0, i + 1, count, (zero,) * NSA_KV_GROUPS)
    for g in groups:
        notsel = jnp.where(cnts[g] < float(SLC_TOPK), 0.0, 1.0).astype(qaugt_ref.dtype)
        for h in range(NSA_HPG):
            head = g * NSA_HPG + h
            qaugt_ref[0, head * LANES:head * LANES + HEAD_DIM, :] = qt_ref[0, head * HEAD_DIM:(head + 1) * HEAD_DIM, :]
            qaugt_ref[0, head * LANES + HEAD_DIM:(head + 1) * LANES, :] = notsel


def _nsa_cmp_call(qnt, kc, vct, ovt):
    B, _, S = qnt.shape
    tq = NSA_TQ
    nch = kc.shape[1]
    return pl.pallas_call(
        _nsa_cmp_kernel,
        grid=(B, S // tq),
        in_specs=[pl.BlockSpec((1, NSA_WIDTH, tq), lambda b, i: (b, 0, i)),
                  pl.BlockSpec((1, nch, 2 * LANES), lambda b, i: (b, 0, 0)),
                  pl.BlockSpec((1, NSA_KV_WIDTH, nch), lambda b, i: (b, 0, 0)),
                  pl.BlockSpec(ovt.shape, lambda b, i: (0, 0))],
        out_specs=[pl.BlockSpec((1, NSA_WIDTH, tq), lambda b, i: (b, 0, i)),
                   pl.BlockSpec((1, NSA_HEADS * LANES, tq), lambda b, i: (b, 0, i))],
        out_shape=[jax.ShapeDtypeStruct((B, NSA_WIDTH, S), jnp.float32),
                   jax.ShapeDtypeStruct((B, NSA_HEADS * LANES, S), MXU_DTYPE)],
        scratch_shapes=[pltpu.VMEM((NSA_KV_GROUPS, LANES - HEAD_DIM, tq), jnp.float32)],
        compiler_params=pltpu.CompilerParams(
            dimension_semantics=("arbitrary", "arbitrary"), vmem_limit_bytes=VMEM_LIMIT),
        name="nsa_cmp",
    )(qnt, kc, vct, ovt)


def _nsa_attn_kernel(qaugt_ref, ksa_ref, vst_ref, kwp_ref, vwt_ref, oct_ref, gatet_ref, y_ref):
    i = pl.program_id(1)
    tq = qaugt_ref.shape[2]
    cols4 = NSA_HPG * tq
    groups = range(NSA_KV_GROUPS)
    lower = (lax.broadcasted_iota(jnp.int32, (tq, cols4), 0)
             <= lax.broadcasted_iota(jnp.int32, (tq, cols4), 1) % tq)
    qs = [_group_queries(qaugt_ref, g, LANES) for g in groups]

    def pv_from(vt_ref):
        def pv(g, p, start, width):
            return _dot(vt_ref[0, g * V_ROWS:(g + 1) * V_ROWS, pl.ds(start, width)], p)
        return pv

    o_slc = _flash_streams(
        NSA_KV_GROUPS,
        lambda g, start, width: _dot(ksa_ref[0, pl.ds(start, width), g * LANES:(g + 1) * LANES], qs[g]),
        pv_from(vst_ref), lower, i, tq, NSA_WIDE)

    sub = WIN_SUB
    wk = WINDOW + sub
    nsub = tq // sub
    rel = (lax.broadcasted_iota(jnp.int32, (wk, NSA_HPG * sub), 1) % sub
           - lax.broadcasted_iota(jnp.int32, (wk, NSA_HPG * sub), 0))
    pv_w = pv_from(vwt_ref)
    o_win = [[None] * nsub for _ in groups]
    for j in range(nsub):
        q_lo = i * tq + j * sub
        wstart = pl.multiple_of(jnp.maximum(q_lo - WINDOW, 0), sub)
        dist = rel + (q_lo - wstart)
        band = (dist >= 0) & (dist < WINDOW)
        for g in groups:
            q_sub = jnp.concatenate(
                [qs[g][:, h * tq + j * sub:h * tq + (j + 1) * sub] for h in range(NSA_HPG)], axis=1)
            s = jnp.where(band, _dot(kwp_ref[0, pl.ds(wstart, wk), g * LANES:(g + 1) * LANES], q_sub), -jnp.inf)
            p = _numerators(s, jnp.max(s, axis=0, keepdims=True))
            o_win[g][j] = _normalise(pv_w(g, p, wstart, wk))

    sig = jax.nn.sigmoid(gatet_ref[0])
    for g in groups:
        for h in range(NSA_HPG):
            head = g * NSA_HPG + h
            o_w = jnp.concatenate([o_win[g][j][:, h * sub:(h + 1) * sub] for j in range(nsub)], axis=1)
            gate = lambda br: sig[br * NSA_HEADS + head:br * NSA_HEADS + head + 1, :]
            y_ref[0, head * HEAD_DIM:(head + 1) * HEAD_DIM, :] = (
                gate(0) * oct_ref[0, head * HEAD_DIM:(head + 1) * HEAD_DIM, :]
                + gate(1) * o_slc[g][:, h * tq:(h + 1) * tq] + gate(2) * o_w)


def _nsa_attn_call(qaugt, ksa, vst, kwp, vwt, oct, gatet):
    B, S, _ = ksa.shape
    tq = NSA_TQ
    feat = lambda r: pl.BlockSpec((1, r, tq), lambda b, i: (b, 0, i))
    seq_tok = lambda w: pl.BlockSpec((1, S, w), lambda b, i: (b, 0, 0))
    seq_feat = lambda r: pl.BlockSpec((1, r, S), lambda b, i: (b, 0, 0))
    return pl.pallas_call(
        _nsa_attn_kernel,
        grid=(B, S // tq),
        in_specs=[feat(NSA_HEADS * LANES), seq_tok(2 * LANES), seq_feat(NSA_KV_GROUPS * V_ROWS), seq_tok(2 * LANES),
                  seq_feat(NSA_KV_GROUPS * V_ROWS), feat(NSA_WIDTH), feat(GATE_ROWS)],
        out_specs=feat(NSA_WIDTH),
        out_shape=jax.ShapeDtypeStruct((B, NSA_WIDTH, S), jnp.float32),
        compiler_params=pltpu.CompilerParams(
            dimension_semantics=("arbitrary", "arbitrary"), vmem_limit_bytes=VMEM_LIMIT),
        name="nsa_attn",
    )(qaugt, ksa, vst, kwp, vwt, oct, gatet)


def _rms_cols(yt, g_col):
    return yt * lax.rsqrt(jnp.mean(yt * yt, axis=0, keepdims=True) + NORM_EPS) * g_col


def _out_kernel(x_ref, ymt_ref, ynt_ref, gm_ref, gn_ref, wo_ref, gmlp_ref, wup_ref, wdn_ref, gfin_ref, o_ref):
    mixed_t = jnp.concatenate([_rms_cols(ymt_ref[0], gm_ref[...]), _rms_cols(ynt_ref[0], gn_ref[...])], axis=0)
    h = x_ref[0] + _dot_tn(mixed_t.astype(MXU_DTYPE), wo_ref[...])
    hn = _rms(h, gmlp_ref[...]).astype(MXU_DTYPE)
    acc = h
    for c in range(D_FF // FF_CHUNK):
        a = jnp.maximum(_dot(hn, wup_ref[:, c * FF_CHUNK:(c + 1) * FF_CHUNK]), 0.0)
        acc = acc + _dot((a * a).astype(MXU_DTYPE), wdn_ref[c * FF_CHUNK:(c + 1) * FF_CHUNK, :])
    o_ref[0] = _rms(acc, gfin_ref[...])


def _out_call(x, ymt, ynt, gm, gn, wo, gmlp, wup, wdn, gfin):
    B, S, D = x.shape
    rows = OUT_ROWS
    tok = pl.BlockSpec((1, rows, D), lambda b, i: (b, i, 0))
    feat = lambda r: pl.BlockSpec((1, r, rows), lambda b, i: (b, 0, i))
    full = lambda a: pl.BlockSpec(a.shape, lambda b, i: (0,) * a.ndim)
    return pl.pallas_call(
        _out_kernel,
        grid=(B, S // rows),
        in_specs=[tok, feat(MLA_WIDTH), feat(NSA_WIDTH), full(gm), full(gn), full(wo), full(gmlp),
                  full(wup), full(wdn), full(gfin)],
        out_specs=tok,
        out_shape=jax.ShapeDtypeStruct((B, S, D), jnp.float32),
        compiler_params=pltpu.CompilerParams(
            dimension_semantics=("arbitrary", "arbitrary"), vmem_limit_bytes=VMEM_LIMIT),
        name="out",
    )(x, ymt, ynt, gm, gn, wo, gmlp, wup, wdn, gfin)


def _rope_cs(pos, dim):
    inv_freq = jnp.exp(-math.log(ROPE_THETA) * jnp.arange(0, dim, 2, dtype=jnp.float32) / dim)
    ang = pos.astype(jnp.float32)[:, None] * inv_freq[None, :]
    return jnp.cos(ang), jnp.sin(ang)


def _nsa_rope_tables(pos, width):
    c, s = _rope_cs(pos, NSA_ROPE_DIM)
    n = pos.shape[0]
    rest = HEAD_DIM - NSA_ROPE_DIM
    ch = jnp.concatenate([c, c, jnp.ones((n, rest), jnp.float32)], axis=1)
    sh = jnp.concatenate([-s, s, jnp.zeros((n, rest), jnp.float32)], axis=1)
    reps = width // HEAD_DIM
    return jnp.tile(ch, (1, reps)), jnp.tile(sh, (1, reps))


def _proj_tables(S):
    pos = jnp.arange(S)
    c, s = _rope_cs(pos, MLA_ROPE_DIM)
    z = lambda w: jnp.zeros((S, w), jnp.float32)
    pad = LANES - MLA_NOPE_DIM - MLA_ROPE_DIM
    ck = jnp.concatenate([z(MLA_NOPE_DIM), c, c, z(pad)], axis=1)
    sk = jnp.concatenate([z(MLA_NOPE_DIM), -s, s, z(pad)], axis=1)
    cn, sn = _nsa_rope_tables(pos, LANES)
    onehot = (pos[:, None] // SLC_BLOCK == jnp.arange(LANES - HEAD_DIM)[None, :]).astype(jnp.float32)
    eneg = jnp.concatenate([z(HEAD_DIM), MASK_BIAS * onehot], axis=1)
    cn8, sn8 = _rope_cs(pos, NSA_ROPE_DIM)
    tabt = jnp.concatenate([c.T, s.T, cn8.T, sn8.T], axis=0)
    return jnp.concatenate([ck, sk, cn, sn, eneg], axis=1), tabt


def _compress_tables(nch):
    end = jnp.arange(nch) * CMP_STRIDE + CMP_BLOCK - 1
    c, s = _nsa_rope_tables(end, HEAD_DIM)
    z = jnp.zeros((nch, LANES - HEAD_DIM), jnp.float32)
    return jnp.concatenate([c, z, s, z], axis=1)


def _overlap_t(nch, nblk):
    cs = jnp.arange(nch)[None, :] * CMP_STRIDE
    ss = jnp.arange(nblk)[:, None] * SLC_BLOCK
    ov = jnp.clip(jnp.minimum(cs + CMP_BLOCK, ss + SLC_BLOCK) - jnp.maximum(cs, ss), 0, None)
    ov = ov.astype(jnp.float32) / CMP_BLOCK
    ov = jnp.where(jnp.arange(nch)[None, :] < nch - 1, ov, 0.0)
    return ov.astype(MXU_DTYPE)


def _pad_cols(w, width):
    return jnp.pad(w, ((0, 0), (0, width - w.shape[1])))


def _compress_consts(pe_k, w1_k, b1_k, w2_k, b2_k, pe_v, w1_v, b1_v, w2_v, b2_v):
    half = CMP_BLOCK // 2
    ncol = 2 * NSA_KV_GROUPS
    w1 = jnp.stack([w1_k, w1_k, w1_v, w1_v]).reshape(ncol, 2, half, HEAD_DIM, CMP_HIDDEN)
    eye = jnp.eye(ncol, dtype=w1.dtype)

    def spread(wh):
        return jnp.einsum('cldh,ce->lcdeh', wh, eye).reshape(half * ncol * HEAD_DIM, ncol * CMP_HIDDEN)

    pe = jnp.stack([pe_k, pe_k, pe_v, pe_v]).reshape(ncol, 2, half, HEAD_DIM)
    pe_row = lambda ph: ph.transpose(1, 0, 2).reshape(1, half * ncol * HEAD_DIM)
    b1 = jnp.concatenate([b1_k, b1_k, b1_v, b1_v]).reshape(1, ncol * CMP_HIDDEN)
    return [
        pe_row(pe[:, 0]), pe_row(pe[:, 1]),
        spread(w1[:, 0]).astype(MXU_DTYPE), spread(w1[:, 1]).astype(MXU_DTYPE), b1,
        _pad_cols(w2_k, LANES).astype(MXU_DTYPE), _pad_cols(b2_k[None, :], LANES),
        w2_v.T.astype(MXU_DTYPE), b2_v[:, None],
    ]


def _proj_weights(w_in, w_uq, w_ukv):
    d = w_in.shape[0]
    z = lambda w: jnp.zeros((d, w), w_in.dtype)
    o_kr = MLA_Q_RANK + MLA_KV_RANK
    o_qn = o_kr + MLA_ROPE_DIM
    o_kv = o_qn + NSA_WIDTH
    o_gate = o_kv + 6 * NSA_KV_WIDTH
    kv = lambda t: w_in[:, o_kv + t * NSA_KV_WIDTH:o_kv + (t + 1) * NSA_KV_WIDTH]
    win = jnp.concatenate([
        w_in[:, :o_kr],
        z(MLA_NOPE_DIM), w_in[:, o_kr:o_qn], z(LANES - MLA_NOPE_DIM - MLA_ROPE_DIM),
        kv(0), kv(1), kv(2), kv(4)], axis=1)
    assert win.shape[1] == _U_COLS
    wint = jnp.concatenate([w_in[:, o_qn:o_kv], kv(3), kv(5), w_in[:, o_gate:], z(GATE_ROWS - NSA_GATES)], axis=1).T
    assert wint.shape[0] == _UT_ROWS
    qd = MLA_NOPE_DIM + MLA_ROPE_DIM
    wq = jnp.pad(w_uq.reshape(MLA_Q_RANK, MLA_HEADS, qd), ((0, 0), (0, 0), (0, LANES - qd)))
    wqt = wq.reshape(MLA_Q_RANK, MLA_HEADS * LANES).T
    kvw = w_ukv.reshape(MLA_KV_RANK, MLA_HEADS, MLA_NOPE_DIM + MLA_V_DIM)
    zk = jnp.zeros((MLA_KV_RANK, MLA_HEADS, LANES - MLA_NOPE_DIM), w_ukv.dtype)
    wk = jnp.concatenate([kvw[..., :MLA_NOPE_DIM], zk], axis=-1).reshape(MLA_KV_RANK, MLA_HEADS * LANES)
    wvt = kvw[..., MLA_NOPE_DIM:].reshape(MLA_KV_RANK, MLA_WIDTH).T
    return (win.astype(MXU_DTYPE), wint.astype(MXU_DTYPE), wqt.astype(MXU_DTYPE), wk.astype(MXU_DTYPE),
            wvt.astype(MXU_DTYPE))


def kernel(x, g_mix_norm, w_in, g_cq, w_uq, g_ckv, w_ukv, cmp_pe_k, cmp_w1_k, cmp_b1_k, cmp_w2_k, cmp_b2_k,
           cmp_pe_v, cmp_w1_v, cmp_b1_v, cmp_w2_v, cmp_b2_v, g_out_mla, g_out_nsa, w_o, g_mlp_norm, w_up,
           w_down, g_final):
    B, S, D = x.shape
    nch = S // CMP_STRIDE
    nblk = S // SLC_BLOCK
    assert w_in.shape[0] == 1
    assert D == D_MODEL and S % MLA_TQ == 0 and S % PROJ_ROWS == 0 and S % OUT_ROWS == 0
    assert S % NSA_TQ == 0 and NSA_TQ % WIN_SUB == 0 and WINDOW % WIN_SUB == 0 and S >= WINDOW + WIN_SUB
    assert SLC_TOPK <= nblk <= LANES - HEAD_DIM
    row = lambda g: g.reshape(1, -1)
    col = lambda g: g.reshape(-1, 1)

    tabs, tabt = _proj_tables(S)
    ctab = _compress_tables(nch)
    ovt = _overlap_t(nch, LANES - HEAD_DIM)

    win, wint, wqt, wk, wvt = _proj_weights(w_in[0], w_uq[0], w_ukv[0])
    qmt, km, vmt, qnt, ksa, vst, kwp, vwt, kcv, gatet = _proj_call(
        x, tabs, tabt, row(g_mix_norm[0]), win, wint, row(g_cq[0]), wqt, row(g_ckv[0]), wk, wvt)
    consts = _compress_consts(cmp_pe_k[0], cmp_w1_k[0], cmp_b1_k[0], cmp_w2_k[0], cmp_b2_k[0],
                              cmp_pe_v[0], cmp_w1_v[0], cmp_b1_v[0], cmp_w2_v[0], cmp_b2_v[0])
    kc, vct = _compress_call(kcv, consts, ctab)
    y_mla_t = _mla_call(qmt, km, vmt)
    oct, qaugt = _nsa_cmp_call(qnt, kc, vct, ovt)
    y_nsa_t = _nsa_attn_call(qaugt, ksa, vst, kwp, vwt, oct, gatet)
    return _out_call(x, y_mla_t, y_nsa_t, col(g_out_mla[0]), col(g_out_nsa[0]), w_o[0].astype(MXU_DTYPE),
                     row(g_mlp_norm[0]), w_up[0].astype(MXU_DTYPE), w_down[0].astype(MXU_DTYPE), row(g_final))
```

```python
import math

import jax
import jax.numpy as jnp
from jax import lax
from jax.experimental import pallas as pl
from jax.experimental.pallas import tpu as pltpu

D_MODEL = 1024
HEAD_DIM = 64
ROPE_THETA = 500000.0
NORM_EPS = 1e-6

MLA_HEADS = 8
MLA_Q_RANK = 256
MLA_KV_RANK = 128
MLA_NOPE_DIM = 64
MLA_ROPE_DIM = 32
MLA_V_DIM = 64
MLA_WIDTH = MLA_HEADS * MLA_V_DIM

NSA_HEADS = 8
NSA_KV_GROUPS = 2
NSA_HPG = NSA_HEADS // NSA_KV_GROUPS
NSA_ROPE_DIM = HEAD_DIM // 4
NSA_WIDTH = NSA_HEADS * HEAD_DIM
NSA_KV_WIDTH = NSA_KV_GROUPS * HEAD_DIM
NSA_GATES = 3 * NSA_HEADS
CMP_BLOCK = 32
CMP_STRIDE = 16
CMP_HIDDEN = 2 * HEAD_DIM
SLC_BLOCK = 64
SLC_TOPK = 16
WINDOW = 512
FORCE_SCORE = 1e9
D_FF = 4 * D_MODEL

LANES = 128
SUBLANES = 8
VMEM_LIMIT = 56 * 1024 * 1024

MXU_DTYPE = jnp.bfloat16
PROJ_ROWS = 512
MLA_TQ = 1024
MLA_HPS = 4
MLA_WIDE = 2
NSA_TQ = 512
NSA_WIDE = 2
WIN_SUB = 128
RANK_UNROLL = NSA_TQ // SLC_BLOCK
OUT_ROWS = 512
FF_CHUNK = 1024
LOG2E = math.log2(math.e)
LAZY_MAX_SLACK = 64.0
MASK_BIAS = -1e30
GATE_ROWS = 32
BF16_ROWS = 16
V_ROWS = HEAD_DIM + BF16_ROWS
EXP_DTYPE = jnp.float32

_U_CQ = 0
_U_CKV = 256
_U_KPE = 384
_U_KCV = 512
_U_KSLC = 768
_U_KWIN = 896
_U_COLS = 1024
_N_TABS = 5
_UT_QN = 0
_UT_VSLC = 512
_UT_VWIN = 640
_UT_GATE = 768
_UT_ROWS = 800
_TT_ROWS = MLA_ROPE_DIM + NSA_ROPE_DIM


def _dot(a, b):
    return jnp.dot(a, b, preferred_element_type=jnp.float32)


def _dot_nt(a, b):
    return lax.dot_general(a, b, (((1,), (1,)), ((), ())), preferred_element_type=jnp.float32)


def _dot_tn(a, b):
    return lax.dot_general(a, b, (((0,), (0,)), ((), ())), preferred_element_type=jnp.float32)


def _rms(x, g):
    return x * lax.rsqrt(jnp.mean(x * x, axis=-1, keepdims=True) + NORM_EPS) * g


def _lane(shape):
    return lax.broadcasted_iota(jnp.int32, shape, len(shape) - 1)


def _swap_pairs(x, lo_end, width):
    n = x.shape[-1]
    lane = _lane(x.shape) % LANES
    return jnp.where(lane < lo_end, pltpu.roll(x, n - width, axis=1), pltpu.roll(x, width, axis=1))


def _nsa_rope(x, cn, sn):
    half = NSA_ROPE_DIM // 2
    lane = _lane(x.shape) % HEAD_DIM
    sw = jnp.where(lane < half, pltpu.roll(x, LANES - half, axis=1), pltpu.roll(x, half, axis=1))
    return x * cn + sw * sn


def _rope_rows(x1, x2, c, s):
    return x1 * c - x2 * s, x2 * c + x1 * s


def _proj_kernel(x_ref, tab_ref, tabt_ref, gmix_ref, win_ref, wint_ref, gcq_ref, wqt_ref, gckv_ref, wk_ref, wvt_ref,
                 qmt_ref, km_ref, vmt_ref, qnt_ref, ksa_ref, vst_ref, kwp_ref, vwt_ref, kcv_ref, gatet_ref):
    x = x_ref[0]
    n = _rms(x, gmix_ref[...]).astype(MXU_DTYPE)
    u = _dot(n, win_ref[...])
    ut = _dot_nt(wint_ref[...], n)

    ck_t, sk_t, cn_t, sn_t, eneg_t = (tab_ref[:, t * LANES:(t + 1) * LANES] for t in range(_N_TABS))
    hm = MLA_ROPE_DIM // 2
    hn = NSA_ROPE_DIM // 2
    cm_r = tabt_ref[0:hm, :]
    sm_r = tabt_ref[hm:2 * hm, :]
    cn_r = tabt_ref[2 * hm:2 * hm + hn, :]
    sn_r = tabt_ref[2 * hm + hn:2 * hm + 2 * hn, :]
    low = _lane((x.shape[0], LANES)) < HEAD_DIM

    mla_scale = (MLA_NOPE_DIM + MLA_ROPE_DIM) ** -0.5 * LOG2E
    cqn = _rms(u[:, _U_CQ:_U_CQ + MLA_Q_RANK], gcq_ref[...]).astype(MXU_DTYPE)
    qmt = _dot_nt(wqt_ref[...], cqn)
    r0 = MLA_NOPE_DIM
    for h in range(MLA_HEADS):
        xh = qmt[h * LANES:(h + 1) * LANES]
        p1, p2 = _rope_rows(xh[r0:r0 + hm], xh[r0 + hm:r0 + 2 * hm], cm_r, sm_r)
        qh = jnp.concatenate([xh[0:r0], p1, p2, xh[r0 + 2 * hm:]], axis=0) * mla_scale
        qmt_ref[0, h * LANES:(h + 1) * LANES, :] = qh.astype(qmt_ref.dtype)

    ckvn = _rms(u[:, _U_CKV:_U_CKV + MLA_KV_RANK], gckv_ref[...]).astype(MXU_DTYPE)
    km = _dot(ckvn, wk_ref[...])
    kpe = u[:, _U_KPE:_U_KPE + LANES]
    kpe = kpe * ck_t + _swap_pairs(kpe, MLA_NOPE_DIM + hm, hm) * sk_t
    for h in range(MLA_HEADS):
        km_ref[0, :, h * LANES:(h + 1) * LANES] = (km[:, h * LANES:(h + 1) * LANES] + kpe).astype(km_ref.dtype)
    ones_tile = jnp.where(lax.broadcasted_iota(jnp.int32, (BF16_ROWS, x.shape[0]), 0) == 0, 1.0, 0.0)

    def put_values(ref, vt):
        for h in range(vt.shape[0] // HEAD_DIM):
            ref[0, h * V_ROWS:h * V_ROWS + HEAD_DIM, :] = vt[h * HEAD_DIM:(h + 1) * HEAD_DIM].astype(ref.dtype)
            ref[0, h * V_ROWS + HEAD_DIM:(h + 1) * V_ROWS, :] = ones_tile.astype(ref.dtype)

    put_values(vmt_ref, _dot_nt(wvt_ref[...], ckvn))

    nsa_scale = HEAD_DIM ** -0.5 * LOG2E
    for h in range(NSA_HEADS):
        xh = ut[_UT_QN + h * HEAD_DIM:_UT_QN + (h + 1) * HEAD_DIM]
        p1, p2 = _rope_rows(xh[0:hn], xh[hn:2 * hn], cn_r, sn_r)
        qh = jnp.concatenate([p1, p2, xh[2 * hn:]], axis=0) * nsa_scale
        qnt_ref[0, h * HEAD_DIM:(h + 1) * HEAD_DIM, :] = qh.astype(qnt_ref.dtype)

    ks = _nsa_rope(u[:, _U_KSLC:_U_KSLC + LANES], cn_t, sn_t)
    ks_r = pltpu.roll(ks, HEAD_DIM, axis=1)
    ksa_ref[0, :, 0:LANES] = jnp.where(low, ks, eneg_t).astype(ksa_ref.dtype)
    ksa_ref[0, :, LANES:2 * LANES] = jnp.where(low, ks_r, eneg_t).astype(ksa_ref.dtype)

    kw = _nsa_rope(u[:, _U_KWIN:_U_KWIN + LANES], cn_t, sn_t)
    kw_r = pltpu.roll(kw, HEAD_DIM, axis=1)
    kwp_ref[0, :, 0:LANES] = jnp.where(low, kw, 0.0).astype(kwp_ref.dtype)
    kwp_ref[0, :, LANES:2 * LANES] = jnp.where(low, kw_r, 0.0).astype(kwp_ref.dtype)

    put_values(vst_ref, ut[_UT_VSLC:_UT_VSLC + NSA_KV_WIDTH])
    put_values(vwt_ref, ut[_UT_VWIN:_UT_VWIN + NSA_KV_WIDTH])
    gatet_ref[0] = ut[_UT_GATE:_UT_GATE + GATE_ROWS]
    kcv_ref[0, 0] = u[:, _U_KCV:_U_KCV + LANES]
    kcv_ref[0, 1] = u[:, _U_KCV + LANES:_U_KCV + 2 * LANES]


def _proj_call(x, tabs, tabt, gmix, win, wint, gcq, wqt, gckv, wk, wvt):
    B, S, D = x.shape
    rows = PROJ_ROWS
    ns = S // rows
    tok = lambda w: pl.BlockSpec((1, rows, w), lambda i, b: (b, i, 0))
    feat = lambda r: pl.BlockSpec((1, r, rows), lambda i, b: (b, 0, i))
    full = lambda a: pl.BlockSpec(a.shape, lambda i, b: (0,) * a.ndim)
    bf = MXU_DTYPE
    outs = [
        (jax.ShapeDtypeStruct((B, MLA_HEADS * LANES, S), bf), feat(MLA_HEADS * LANES)),
        (jax.ShapeDtypeStruct((B, S, MLA_HEADS * LANES), bf), tok(MLA_HEADS * LANES)),
        (jax.ShapeDtypeStruct((B, MLA_HEADS * V_ROWS, S), bf), feat(MLA_HEADS * V_ROWS)),
        (jax.ShapeDtypeStruct((B, NSA_WIDTH, S), bf), feat(NSA_WIDTH)),
        (jax.ShapeDtypeStruct((B, S, 2 * LANES), bf), tok(2 * LANES)),
        (jax.ShapeDtypeStruct((B, NSA_KV_GROUPS * V_ROWS, S), bf), feat(NSA_KV_GROUPS * V_ROWS)),
        (jax.ShapeDtypeStruct((B, S, 2 * LANES), bf), tok(2 * LANES)),
        (jax.ShapeDtypeStruct((B, NSA_KV_GROUPS * V_ROWS, S), bf), feat(NSA_KV_GROUPS * V_ROWS)),
        (jax.ShapeDtypeStruct((B, 2, S, LANES), jnp.float32),
         pl.BlockSpec((1, 2, rows, LANES), lambda i, b: (b, 0, i, 0))),
        (jax.ShapeDtypeStruct((B, GATE_ROWS, S), jnp.float32), feat(GATE_ROWS)),
    ]
    return pl.pallas_call(
        _proj_kernel,
        grid=(ns, B),
        in_specs=[tok(D), pl.BlockSpec((rows, _N_TABS * LANES), lambda i, b: (i, 0)),
                  pl.BlockSpec((_TT_ROWS, rows), lambda i, b: (0, i)),
                  full(gmix), full(win), full(wint), full(gcq), full(wqt), full(gckv), full(wk), full(wvt)],
        out_specs=[o[1] for o in outs],
        out_shape=[o[0] for o in outs],
        compiler_params=pltpu.CompilerParams(
            dimension_semantics=("arbitrary", "arbitrary"), vmem_limit_bytes=VMEM_LIMIT),
        name="proj",
    )(x, tabs, tabt, gmix, win, wint, gcq, wqt, gckv, wk, wvt)


def _gelu_tanh(x):
    return 0.5 * x * (1.0 + jnp.tanh(math.sqrt(2.0 / math.pi) * (x + 0.044715 * (x * x * x))))


def _compress_kernel(x_ref, pea_ref, peb_ref, wa_ref, wb_ref, b1_ref, w2k_ref, b2k_ref, w2vt_ref, b2vt_ref,
                     ctab_ref, kc_ref, vct_ref):
    nch = kc_ref.shape[1]
    a = b = None
    for t in range(CMP_STRIDE):
        for c in range(2):
            cols = slice((2 * t + c) * LANES, (2 * t + c + 1) * LANES)
            xt = x_ref[0, c, pl.ds(t, nch, stride=CMP_STRIDE), :]
            at = _dot((xt + pea_ref[:, cols]).astype(MXU_DTYPE), wa_ref[cols, :])
            bt = _dot((xt + peb_ref[:, cols]).astype(MXU_DTYPE), wb_ref[cols, :])
            a = at if a is None else a + at
            b = bt if b is None else b + bt
    hid = _gelu_tanh(a + pltpu.roll(b, nch - 1, axis=0) + b1_ref[...]).astype(MXU_DTYPE)
    cc = ctab_ref[:, 0:LANES]
    sc = ctab_ref[:, LANES:2 * LANES]
    for g in range(NSA_KV_GROUPS):
        hk = hid[:, g * LANES:(g + 1) * LANES]
        kc = _dot(hk, w2k_ref[...]) + b2k_ref[...]
        kc_ref[0, :, g * LANES:(g + 1) * LANES] = _nsa_rope(kc, cc, sc).astype(kc_ref.dtype)
        hv = hid[:, (NSA_KV_GROUPS + g) * LANES:(NSA_KV_GROUPS + g + 1) * LANES]
        vct_ref[0, g * HEAD_DIM:(g + 1) * HEAD_DIM, :] = (
            _dot_nt(w2vt_ref[...], hv) + b2vt_ref[...]).astype(vct_ref.dtype)


def _compress_call(kcv, consts, ctab):
    B, _, S, _ = kcv.shape
    nch = S // CMP_STRIDE
    full = lambda a: pl.BlockSpec(a.shape, lambda b: (0,) * a.ndim)
    return pl.pallas_call(
        _compress_kernel,
        grid=(B,),
        in_specs=[pl.BlockSpec((1, 2, S, LANES), lambda b: (b, 0, 0, 0))] + [full(c) for c in consts] + [full(ctab)],
        out_specs=[pl.BlockSpec((1, nch, 2 * LANES), lambda b: (b, 0, 0)),
                   pl.BlockSpec((1, NSA_KV_WIDTH, nch), lambda b: (b, 0, 0))],
        out_shape=[jax.ShapeDtypeStruct((B, nch, 2 * LANES), MXU_DTYPE),
                   jax.ShapeDtypeStruct((B, NSA_KV_WIDTH, nch), MXU_DTYPE)],
        compiler_params=pltpu.CompilerParams(
            dimension_semantics=("arbitrary",), vmem_limit_bytes=VMEM_LIMIT),
        name="compress",
    )(kcv, *consts, ctab)


def _numerators(s, m):
    return jnp.exp2((s - m).astype(EXP_DTYPE)).astype(MXU_DTYPE)


def _normalise(acc):
    return acc[0:HEAD_DIM] * (1.0 / acc[HEAD_DIM:HEAD_DIM + 1])


def _own_scores(qt, k_tile):
    n = k_tile.shape[0]
    kt = k_tile.astype(jnp.float32).T
    return jnp.concatenate(
        [jnp.sum(qt[:, h * n:(h + 1) * n].astype(jnp.float32) * kt, axis=0, keepdims=True)
         for h in range(qt.shape[1] // n)], axis=1)


def _fast_start(shifts):
    return tuple((jnp.zeros((V_ROWS, m.shape[1]), jnp.float32), jnp.full(m.shape, -jnp.inf, jnp.float32))
                 for m in shifts)


def _fast_tile(carry, shifts, scores, pv, start, width, mask=None):
    out = []
    for st, (acc, top) in enumerate(carry):
        s = scores(st, start, width)
        if mask is not None:
            s = jnp.where(mask, s, -jnp.inf)
        out.append((acc + pv(st, _numerators(s, shifts[st]), start, width),
                    jnp.maximum(top, jnp.max(s, axis=0, keepdims=True))))
    return tuple(out)


def _fast_excess(carry, shifts):
    gaps = [jnp.max(top - m) for (acc, top), m in zip(carry, shifts)]
    excess = gaps[0]
    for g in gaps[1:]:
        excess = jnp.maximum(excess, g)
    return excess


def _tile_starts(i, tq, wide):
    return wide * tq, i // wide


def _fast_streams(shifts, scores, pv, lower, i, tq, wide):
    carry = _fast_tile(_fast_start(shifts), shifts, scores, pv, pl.multiple_of(i * tq, tq), tq, lower)
    wt, n_wide = _tile_starts(i, tq, wide)
    carry = lax.fori_loop(
        0, n_wide, lambda jj, c: _fast_tile(c, shifts, scores, pv, pl.multiple_of(jj * wt, wt), wt), carry)
    return lax.fori_loop(
        n_wide * wide, i, lambda jj, c: _fast_tile(c, shifts, scores, pv, pl.multiple_of(jj * tq, tq), tq), carry)


def _exact_streams(n_streams, scores, pv, lower, i, tq, wide):
    start_d = pl.multiple_of(i * tq, tq)
    carry = []
    for st in range(n_streams):
        s = jnp.where(lower, scores(st, start_d, tq), -jnp.inf)
        m = jnp.max(s, axis=0, keepdims=True)
        carry.append((m, pv(st, _numerators(s, m), start_d, tq)))

    def step(start, width, carry):
        out = []
        for st, (m, acc) in enumerate(carry):
            s = scores(st, start, width)
            m_new = jnp.maximum(m, jnp.max(s, axis=0, keepdims=True))
            out.append((m_new, jnp.exp2(m - m_new) * acc + pv(st, _numerators(s, m_new), start, width)))
        return tuple(out)

    wt, n_wide = _tile_starts(i, tq, wide)
    carry = lax.fori_loop(0, n_wide, lambda jj, c: step(pl.multiple_of(jj * wt, wt), wt, c), tuple(carry))
    carry = lax.fori_loop(n_wide * wide, i, lambda jj, c: step(pl.multiple_of(jj * tq, tq), tq, c), carry)
    return tuple(_normalise(acc) for (m, acc) in carry)


def _mla_kernel(qt_ref, k_ref, vt_ref, o_ref):
    i = pl.program_id(2)
    tq = qt_ref.shape[2]
    lower = lax.broadcasted_iota(jnp.int32, (tq, tq), 0) <= lax.broadcasted_iota(jnp.int32, (tq, tq), 1)
    qs = [qt_ref[0, hh * LANES:(hh + 1) * LANES, :] for hh in range(MLA_HPS)]

    def scores(hh, start, width):
        return _dot(k_ref[0, pl.ds(start, width), hh * LANES:(hh + 1) * LANES], qs[hh])

    def pv(hh, p, start, width):
        return _dot(vt_ref[0, hh * V_ROWS:(hh + 1) * V_ROWS, pl.ds(start, width)], p)

    own_rows = pl.ds(pl.multiple_of(i * tq, tq), tq)
    shifts = [_own_scores(qs[hh], k_ref[0, own_rows, hh * LANES:(hh + 1) * LANES]) for hh in range(MLA_HPS)]
    fast = _fast_streams(shifts, scores, pv, lower, i, tq, MLA_WIDE)
    outs = lax.cond(_fast_excess(fast, shifts) > LAZY_MAX_SLACK,
                    lambda _: _exact_streams(MLA_HPS, scores, pv, lower, i, tq, MLA_WIDE),
                    lambda _: tuple(_normalise(acc) for acc, top in fast), None)
    for hh, o in enumerate(outs):
        o_ref[0, hh * MLA_V_DIM:(hh + 1) * MLA_V_DIM, :] = o


def _mla_call(qmt, km, vmt):
    B, S, _ = km.shape
    tq = MLA_TQ
    hps = MLA_HPS
    return pl.pallas_call(
        _mla_kernel,
        grid=(B, MLA_HEADS // hps, S // tq),
        in_specs=[pl.BlockSpec((1, hps * LANES, tq), lambda b, p, i: (b, p, i)),
                  pl.BlockSpec((1, S, hps * LANES), lambda b, p, i: (b, 0, p)),
                  pl.BlockSpec((1, hps * V_ROWS, S), lambda b, p, i: (b, p, 0))],
        out_specs=pl.BlockSpec((1, hps * MLA_V_DIM, tq), lambda b, p, i: (b, p, i)),
        out_shape=jax.ShapeDtypeStruct((B, MLA_WIDTH, S), jnp.float32),
        compiler_params=pltpu.CompilerParams(
            dimension_semantics=("arbitrary", "arbitrary", "arbitrary"), vmem_limit_bytes=VMEM_LIMIT),
        name="mla",
    )(qmt, km, vmt)


def _group_queries(q_ref, g, rows):
    return jnp.concatenate(
        [q_ref[0, (g * NSA_HPG + h) * rows:(g * NSA_HPG + h + 1) * rows, :] for h in range(NSA_HPG)], axis=1)


def _nsa_cmp_kernel(qt_ref, kc_ref, vct_ref, ovt_ref, oct_ref, qaugt_ref, score_ref):
    i = pl.program_id(1)
    tq = qt_ref.shape[2]
    nch = kc_ref.shape[1]
    nblk = score_ref.shape[1]
    cols4 = NSA_HPG * tq
    q0 = i * tq
    qpos_c = q0 + lax.broadcasted_iota(jnp.int32, (nch, cols4), 1) % tq
    cmp_end = lax.broadcasted_iota(jnp.int32, (nch, cols4), 0) * CMP_STRIDE + (CMP_BLOCK - 1)
    mask_c = cmp_end <= qpos_c

    blk = lax.broadcasted_iota(jnp.int32, (nblk, tq), 0)
    qpos_r = q0 + lax.broadcasted_iota(jnp.int32, (nblk, tq), 1)
    cur = qpos_r // SLC_BLOCK
    forced = (blk == 0) | (blk == cur) | (blk == cur - 1)
    causal = blk * SLC_BLOCK <= qpos_r
    groups = range(NSA_KV_GROUPS)

    scores = []
    for g in groups:
        q4 = _group_queries(qt_ref, g, HEAD_DIM)
        qpad = jnp.concatenate([q4, jnp.zeros_like(q4)], axis=0)
        s = jnp.where(mask_c, _dot(kc_ref[0, :, g * LANES:(g + 1) * LANES], qpad), -jnp.inf)
        m = jnp.max(s, axis=0, keepdims=True)
        m = jnp.where(m == -jnp.inf, 0.0, m)
        e = jnp.exp2(s - m)
        pr = e / jnp.maximum(jnp.sum(e, axis=0, keepdims=True), 1e-30)
        o = _dot(vct_ref[0, g * HEAD_DIM:(g + 1) * HEAD_DIM, :], pr.astype(MXU_DTYPE))
        for h in range(NSA_HPG):
            head = g * NSA_HPG + h
            oct_ref[0, head * HEAD_DIM:(head + 1) * HEAD_DIM, :] = o[:, h * tq:(h + 1) * tq]

        psum = (pr[:, 0:tq] + pr[:, tq:2 * tq]) + (pr[:, 2 * tq:3 * tq] + pr[:, 3 * tq:4 * tq])
        p_hi = psum.astype(MXU_DTYPE)
        p_lo = (psum - p_hi.astype(jnp.float32)).astype(MXU_DTYPE)
        imp = _dot(ovt_ref[...], p_hi) + _dot(ovt_ref[...], p_lo)
        score = jnp.where(forced, FORCE_SCORE, jnp.where(causal, imp, -FORCE_SCORE))
        score_ref[g] = score
        scores.append(score)

    def count(c, cnts):
        out = list(cnts)
        for t in range(RANK_UNROLL):
            r = c * RANK_UNROLL + t
            tie = jnp.where(r < blk, 1.0, 0.0)
            for g in groups:
                row = score_ref[g, pl.ds(r, 1), :]
                out[g] = out[g] + jnp.where(row > scores[g], 1.0, jnp.where(row == scores[g], tie, 0.0))
        return tuple(out)

    zero = jnp.zeros((nblk, tq), jnp.float32)
    cnts = lax.fori_loop(0, i + 1, count, (zero,) * NSA_KV_GROUPS)
    for g in groups:
        notsel = jnp.where(cnts[g] < float(SLC_TOPK), 0.0, 1.0).astype(qaugt_ref.dtype)
        for h in range(NSA_HPG):
            head = g * NSA_HPG + h
            qaugt_ref[0, head * LANES:head * LANES + HEAD_DIM, :] = qt_ref[0, head * HEAD_DIM:(head + 1) * HEAD_DIM, :]
            qaugt_ref[0, head * LANES + HEAD_DIM:(head + 1) * LANES, :] = notsel


def _nsa_cmp_call(qnt, kc, vct, ovt):
    B, _, S = qnt.shape
    tq = NSA_TQ
    nch = kc.shape[1]
    return pl.pallas_call(
        _nsa_cmp_kernel,
        grid=(B, S // tq),
        in_specs=[pl.BlockSpec((1, NSA_WIDTH, tq), lambda b, i: (b, 0, i)),
                  pl.BlockSpec((1, nch, 2 * LANES), lambda b, i: (b, 0, 0)),
                  pl.BlockSpec((1, NSA_KV_WIDTH, nch), lambda b, i: (b, 0, 0)),
                  pl.BlockSpec(ovt.shape, lambda b, i: (0, 0))],
        out_specs=[pl.BlockSpec((1, NSA_WIDTH, tq), lambda b, i: (b, 0, i)),
                   pl.BlockSpec((1, NSA_HEADS * LANES, tq), lambda b, i: (b, 0, i))],
        out_shape=[jax.ShapeDtypeStruct((B, NSA_WIDTH, S), jnp.float32),
                   jax.ShapeDtypeStruct((B, NSA_HEADS * LANES, S), MXU_DTYPE)],
        scratch_shapes=[pltpu.VMEM((NSA_KV_GROUPS, LANES - HEAD_DIM, tq), jnp.float32)],
        compiler_params=pltpu.CompilerParams(
            dimension_semantics=("arbitrary", "arbitrary"), vmem_limit_bytes=VMEM_LIMIT),
        name="nsa_cmp",
    )(qnt, kc, vct, ovt)


def _nsa_attn_kernel(qaugt_ref, ksa_ref, vst_ref, kwp_ref, vwt_ref, oct_ref, gatet_ref, y_ref):
    i = pl.program_id(1)
    tq = qaugt_ref.shape[2]
    cols4 = NSA_HPG * tq
    groups = range(NSA_KV_GROUPS)
    lower = (lax.broadcasted_iota(jnp.int32, (tq, cols4), 0)
             <= lax.broadcasted_iota(jnp.int32, (tq, cols4), 1) % tq)
    qs = [_group_queries(qaugt_ref, g, LANES) for g in groups]

    def pv_from(vt_ref):
        def pv(g, p, start, width):
            return _dot(vt_ref[0, g * V_ROWS:(g + 1) * V_ROWS, pl.ds(start, width)], p)
        return pv

    pv_s = pv_from(vst_ref)
    pv_w = pv_from(vwt_ref)

    def s_scores(g, start, width):
        return _dot(ksa_ref[0, pl.ds(start, width), g * LANES:(g + 1) * LANES], qs[g])

    sub = WIN_SUB
    wk = WINDOW + sub
    nsub = tq // sub
    rel = (lax.broadcasted_iota(jnp.int32, (wk, NSA_HPG * sub), 1) % sub
           - lax.broadcasted_iota(jnp.int32, (wk, NSA_HPG * sub), 0))

    def window_tile(j):
        q_lo = i * tq + j * sub
        wstart = pl.multiple_of(jnp.maximum(q_lo - WINDOW, 0), sub)
        dist = rel + (q_lo - wstart)
        q_subs = [jnp.concatenate(
            [qs[g][:, h * tq + j * sub:h * tq + (j + 1) * sub] for h in range(NSA_HPG)], axis=1) for g in groups]

        def w_scores(g, start, width):
            return _dot(kwp_ref[0, pl.ds(start, width), g * LANES:(g + 1) * LANES], q_subs[g])

        return q_lo, wstart, (dist >= 0) & (dist < WINDOW), q_subs, w_scores

    own_rows = pl.ds(pl.multiple_of(i * tq, tq), tq)
    s_shifts = [_own_scores(qs[g], ksa_ref[0, own_rows, g * LANES:(g + 1) * LANES]) for g in groups]
    fast_s = _fast_streams(s_shifts, s_scores, pv_s, lower, i, tq, NSA_WIDE)
    excess = _fast_excess(fast_s, s_shifts)
    fast_w = []
    for j in range(nsub):
        q_lo, wstart, band, q_subs, w_scores = window_tile(j)
        sub_rows = pl.ds(pl.multiple_of(q_lo, sub), sub)
        w_shifts = [_own_scores(q_subs[g], kwp_ref[0, sub_rows, g * LANES:(g + 1) * LANES]) for g in groups]
        carry = _fast_tile(_fast_start(w_shifts), w_shifts, w_scores, pv_w, wstart, wk, band)
        excess = jnp.maximum(excess, _fast_excess(carry, w_shifts))
        fast_w.append(tuple(_normalise(acc) for acc, top in carry))

    def exact(_):
        o_s = _exact_streams(NSA_KV_GROUPS, s_scores, pv_s, lower, i, tq, NSA_WIDE)
        o_w = []
        for j in range(nsub):
            q_lo, wstart, band, q_subs, w_scores = window_tile(j)
            outs = []
            for g in groups:
                s = jnp.where(band, w_scores(g, wstart, wk), -jnp.inf)
                p = _numerators(s, jnp.max(s, axis=0, keepdims=True))
                outs.append(_normalise(pv_w(g, p, wstart, wk)))
            o_w.append(tuple(outs))
        return o_s, tuple(o_w)

    o_slc, o_wins = lax.cond(
        excess > LAZY_MAX_SLACK, exact,
        lambda _: (tuple(_normalise(acc) for acc, top in fast_s), tuple(fast_w)), None)
    o_win = [[o_wins[j][g] for j in range(nsub)] for g in groups]

    sig = jax.nn.sigmoid(gatet_ref[0])
    for g in groups:
        for h in range(NSA_HPG):
            head = g * NSA_HPG + h
            o_w = jnp.concatenate([o_win[g][j][:, h * sub:(h + 1) * sub] for j in range(nsub)], axis=1)
            gate = lambda br: sig[br * NSA_HEADS + head:br * NSA_HEADS + head + 1, :]
            y_ref[0, head * HEAD_DIM:(head + 1) * HEAD_DIM, :] = (
                gate(0) * oct_ref[0, head * HEAD_DIM:(head + 1) * HEAD_DIM, :]
                + gate(1) * o_slc[g][:, h * tq:(h + 1) * tq] + gate(2) * o_w)


def _nsa_attn_call(qaugt, ksa, vst, kwp, vwt, oct, gatet):
    B, S, _ = ksa.shape
    tq = NSA_TQ
    feat = lambda r: pl.BlockSpec((1, r, tq), lambda b, i: (b, 0, i))
    seq_tok = lambda w: pl.BlockSpec((1, S, w), lambda b, i: (b, 0, 0))
    seq_feat = lambda r: pl.BlockSpec((1, r, S), lambda b, i: (b, 0, 0))
    return pl.pallas_call(
        _nsa_attn_kernel,
        grid=(B, S // tq),
        in_specs=[feat(NSA_HEADS * LANES), seq_tok(2 * LANES), seq_feat(NSA_KV_GROUPS * V_ROWS), seq_tok(2 * LANES),
                  seq_feat(NSA_KV_GROUPS * V_ROWS), feat(NSA_WIDTH), feat(GATE_ROWS)],
        out_specs=feat(NSA_WIDTH),
        out_shape=jax.ShapeDtypeStruct((B, NSA_WIDTH, S), jnp.float32),
        compiler_params=pltpu.CompilerParams(
            dimension_semantics=("arbitrary", "arbitrary"), vmem_limit_bytes=VMEM_LIMIT),
        name="nsa_attn",
    )(qaugt, ksa, vst, kwp, vwt, oct, gatet)


def _rms_cols(yt, g_col):
    return yt * lax.rsqrt(jnp.mean(yt * yt, axis=0, keepdims=True) + NORM_EPS) * g_col


def _out_kernel(x_ref, ymt_ref, ynt_ref, gm_ref, gn_ref, wo_ref, gmlp_ref, wup_ref, wdn_ref, gfin_ref, o_ref):
    mixed_t = jnp.concatenate([_rms_cols(ymt_ref[0], gm_ref[...]), _rms_cols(ynt_ref[0], gn_ref[...])], axis=0)
    h = x_ref[0] + _dot_tn(mixed_t.astype(MXU_DTYPE), wo_ref[...])
    hn = _rms(h, gmlp_ref[...]).astype(MXU_DTYPE)
    acc = h
    for c in range(D_FF // FF_CHUNK):
        a = jnp.maximum(_dot(hn, wup_ref[:, c * FF_CHUNK:(c + 1) * FF_CHUNK]), 0.0)
        acc = acc + _dot((a * a).astype(MXU_DTYPE), wdn_ref[c * FF_CHUNK:(c + 1) * FF_CHUNK, :])
    o_ref[0] = _rms(acc, gfin_ref[...])


def _out_call(x, ymt, ynt, gm, gn, wo, gmlp, wup, wdn, gfin):
    B, S, D = x.shape
    rows = OUT_ROWS
    tok = pl.BlockSpec((1, rows, D), lambda b, i: (b, i, 0))
    feat = lambda r: pl.BlockSpec((1, r, rows), lambda b, i: (b, 0, i))
    full = lambda a: pl.BlockSpec(a.shape, lambda b, i: (0,) * a.ndim)
    return pl.pallas_call(
        _out_kernel,
        grid=(B, S // rows),
        in_specs=[tok, feat(MLA_WIDTH), feat(NSA_WIDTH), full(gm), full(gn), full(wo), full(gmlp),
                  full(wup), full(wdn), full(gfin)],
        out_specs=tok,
        out_shape=jax.ShapeDtypeStruct((B, S, D), jnp.float32),
        compiler_params=pltpu.CompilerParams(
            dimension_semantics=("arbitrary", "arbitrary"), vmem_limit_bytes=VMEM_LIMIT),
        name="out",
    )(x, ymt, ynt, gm, gn, wo, gmlp, wup, wdn, gfin)


def _rope_cs(pos, dim):
    inv_freq = jnp.exp(-math.log(ROPE_THETA) * jnp.arange(0, dim, 2, dtype=jnp.float32) / dim)
    ang = pos.astype(jnp.float32)[:, None] * inv_freq[None, :]
    return jnp.cos(ang), jnp.sin(ang)


def _nsa_rope_tables(pos, width):
    c, s = _rope_cs(pos, NSA_ROPE_DIM)
    n = pos.shape[0]
    rest = HEAD_DIM - NSA_ROPE_DIM
    ch = jnp.concatenate([c, c, jnp.ones((n, rest), jnp.float32)], axis=1)
    sh = jnp.concatenate([-s, s, jnp.zeros((n, rest), jnp.float32)], axis=1)
    reps = width // HEAD_DIM
    return jnp.tile(ch, (1, reps)), jnp.tile(sh, (1, reps))


def _proj_tables(S):
    pos = jnp.arange(S)
    c, s = _rope_cs(pos, MLA_ROPE_DIM)
    z = lambda w: jnp.zeros((S, w), jnp.float32)
    pad = LANES - MLA_NOPE_DIM - MLA_ROPE_DIM
    ck = jnp.concatenate([z(MLA_NOPE_DIM), c, c, z(pad)], axis=1)
    sk = jnp.concatenate([z(MLA_NOPE_DIM), -s, s, z(pad)], axis=1)
    cn, sn = _nsa_rope_tables(pos, LANES)
    onehot = (pos[:, None] // SLC_BLOCK == jnp.arange(LANES - HEAD_DIM)[None, :]).astype(jnp.float32)
    eneg = jnp.concatenate([z(HEAD_DIM), MASK_BIAS * onehot], axis=1)
    cn8, sn8 = _rope_cs(pos, NSA_ROPE_DIM)
    tabt = jnp.concatenate([c.T, s.T, cn8.T, sn8.T], axis=0)
    return jnp.concatenate([ck, sk, cn, sn, eneg], axis=1), tabt


def _compress_tables(nch):
    end = jnp.arange(nch) * CMP_STRIDE + CMP_BLOCK - 1
    c, s = _nsa_rope_tables(end, HEAD_DIM)
    z = jnp.zeros((nch, LANES - HEAD_DIM), jnp.float32)
    return jnp.concatenate([c, z, s, z], axis=1)


def _overlap_t(nch, nblk):
    cs = jnp.arange(nch)[None, :] * CMP_STRIDE
    ss = jnp.arange(nblk)[:, None] * SLC_BLOCK
    ov = jnp.clip(jnp.minimum(cs + CMP_BLOCK, ss + SLC_BLOCK) - jnp.maximum(cs, ss), 0, None)
    ov = ov.astype(jnp.float32) / CMP_BLOCK
    ov = jnp.where(jnp.arange(nch)[None, :] < nch - 1, ov, 0.0)
    return ov.astype(MXU_DTYPE)


def _pad_cols(w, width):
    return jnp.pad(w, ((0, 0), (0, width - w.shape[1])))


def _compress_consts(pe_k, w1_k, b1_k, w2_k, b2_k, pe_v, w1_v, b1_v, w2_v, b2_v):
    half = CMP_BLOCK // 2
    ncol = 2 * NSA_KV_GROUPS
    w1 = jnp.stack([w1_k, w1_k, w1_v, w1_v]).reshape(ncol, 2, half, HEAD_DIM, CMP_HIDDEN)
    eye = jnp.eye(ncol, dtype=w1.dtype)

    def spread(wh):
        return jnp.einsum('cldh,ce->lcdeh', wh, eye).reshape(half * ncol * HEAD_DIM, ncol * CMP_HIDDEN)

    pe = jnp.stack([pe_k, pe_k, pe_v, pe_v]).reshape(ncol, 2, half, HEAD_DIM)
    pe_row = lambda ph: ph.transpose(1, 0, 2).reshape(1, half * ncol * HEAD_DIM)
    b1 = jnp.concatenate([b1_k, b1_k, b1_v, b1_v]).reshape(1, ncol * CMP_HIDDEN)
    return [
        pe_row(pe[:, 0]), pe_row(pe[:, 1]),
        spread(w1[:, 0]).astype(MXU_DTYPE), spread(w1[:, 1]).astype(MXU_DTYPE), b1,
        _pad_cols(w2_k, LANES).astype(MXU_DTYPE), _pad_cols(b2_k[None, :], LANES),
        w2_v.T.astype(MXU_DTYPE), b2_v[:, None],
    ]


def _proj_weights(w_in, w_uq, w_ukv):
    d = w_in.shape[0]
    z = lambda w: jnp.zeros((d, w), w_in.dtype)
    o_kr = MLA_Q_RANK + MLA_KV_RANK
    o_qn = o_kr + MLA_ROPE_DIM
    o_kv = o_qn + NSA_WIDTH
    o_gate = o_kv + 6 * NSA_KV_WIDTH
    kv = lambda t: w_in[:, o_kv + t * NSA_KV_WIDTH:o_kv + (t + 1) * NSA_KV_WIDTH]
    win = jnp.concatenate([
        w_in[:, :o_kr],
        z(MLA_NOPE_DIM), w_in[:, o_kr:o_qn], z(LANES - MLA_NOPE_DIM - MLA_ROPE_DIM),
        kv(0), kv(1), kv(2), kv(4)], axis=1)
    assert win.shape[1] == _U_COLS
    wint = jnp.concatenate([w_in[:, o_qn:o_kv], kv(3), kv(5), w_in[:, o_gate:], z(GATE_ROWS - NSA_GATES)], axis=1).T
    assert wint.shape[0] == _UT_ROWS
    qd = MLA_NOPE_DIM + MLA_ROPE_DIM
    wq = jnp.pad(w_uq.reshape(MLA_Q_RANK, MLA_HEADS, qd), ((0, 0), (0, 0), (0, LANES - qd)))
    wqt = wq.reshape(MLA_Q_RANK, MLA_HEADS * LANES).T
    kvw = w_ukv.reshape(MLA_KV_RANK, MLA_HEADS, MLA_NOPE_DIM + MLA_V_DIM)
    zk = jnp.zeros((MLA_KV_RANK, MLA_HEADS, LANES - MLA_NOPE_DIM), w_ukv.dtype)
    wk = jnp.concatenate([kvw[..., :MLA_NOPE_DIM], zk], axis=-1).reshape(MLA_KV_RANK, MLA_HEADS * LANES)
    wvt = kvw[..., MLA_NOPE_DIM:].reshape(MLA_KV_RANK, MLA_WIDTH).T
    return (win.astype(MXU_DTYPE), wint.astype(MXU_DTYPE), wqt.astype(MXU_DTYPE), wk.astype(MXU_DTYPE),
            wvt.astype(MXU_DTYPE))


def kernel(x, g_mix_norm, w_in, g_cq, w_uq, g_ckv, w_ukv, cmp_pe_k, cmp_w1_k, cmp_b1_k, cmp_w2_k, cmp_b2_k,
           cmp_pe_v, cmp_w1_v, cmp_b1_v, cmp_w2_v, cmp_b2_v, g_out_mla, g_out_nsa, w_o, g_mlp_norm, w_up,
           w_down, g_final):
    B, S, D = x.shape
    nch = S // CMP_STRIDE
    nblk = S // SLC_BLOCK
    assert w_in.shape[0] == 1
    assert D == D_MODEL and S % MLA_TQ == 0 and S % PROJ_ROWS == 0 and S % OUT_ROWS == 0
    assert S % NSA_TQ == 0 and NSA_TQ % WIN_SUB == 0 and WINDOW % WIN_SUB == 0 and S >= WINDOW + WIN_SUB
    assert SLC_TOPK <= nblk <= LANES - HEAD_DIM
    row = lambda g: g.reshape(1, -1)
    col = lambda g: g.reshape(-1, 1)

    tabs, tabt = _proj_tables(S)
    ctab = _compress_tables(nch)
    ovt = _overlap_t(nch, LANES - HEAD_DIM)

    win, wint, wqt, wk, wvt = _proj_weights(w_in[0], w_uq[0], w_ukv[0])
    qmt, km, vmt, qnt, ksa, vst, kwp, vwt, kcv, gatet = _proj_call(
        x, tabs, tabt, row(g_mix_norm[0]), win, wint, row(g_cq[0]), wqt, row(g_ckv[0]), wk, wvt)
    consts = _compress_consts(cmp_pe_k[0], cmp_w1_k[0], cmp_b1_k[0], cmp_w2_k[0], cmp_b2_k[0],
                              cmp_pe_v[0], cmp_w1_v[0], cmp_b1_v[0], cmp_w2_v[0], cmp_b2_v[0])
    kc, vct = _compress_call(kcv, consts, ctab)
    y_mla_t = _mla_call(qmt, km, vmt)
    oct, qaugt = _nsa_cmp_call(qnt, kc, vct, ovt)
    y_nsa_t = _nsa_attn_call(qaugt, ksa, vst, kwp, vwt, oct, gatet)
    return _out_call(x, y_mla_t, y_nsa_t, col(g_out_mla[0]), col(g_out_nsa[0]), w_o[0].astype(MXU_DTYPE),
                     row(g_mlp_norm[0]), w_up[0].astype(MXU_DTYPE), w_down[0].astype(MXU_DTYPE), row(g_final))
```

```python
import math

import jax
import jax.numpy as jnp
from jax import lax
from jax.experimental import pallas as pl
from jax.experimental.pallas import tpu as pltpu

D_MODEL = 1024
HEAD_DIM = 64
ROPE_THETA = 500000.0
NORM_EPS = 1e-6

MLA_HEADS = 8
MLA_Q_RANK = 256
MLA_KV_RANK = 128
MLA_NOPE_DIM = 64
MLA_ROPE_DIM = 32
MLA_V_DIM = 64
MLA_WIDTH = MLA_HEADS * MLA_V_DIM

NSA_HEADS = 8
NSA_KV_GROUPS = 2
NSA_HPG = NSA_HEADS // NSA_KV_GROUPS
NSA_ROPE_DIM = HEAD_DIM // 4
NSA_WIDTH = NSA_HEADS * HEAD_DIM
NSA_KV_WIDTH = NSA_KV_GROUPS * HEAD_DIM
NSA_GATES = 3 * NSA_HEADS
CMP_BLOCK = 32
CMP_STRIDE = 16
CMP_HIDDEN = 2 * HEAD_DIM
SLC_BLOCK = 64
SLC_TOPK = 16
WINDOW = 512
FORCE_SCORE = 1e9
D_FF = 4 * D_MODEL

LANES = 128
SUBLANES = 8
VMEM_LIMIT = 56 * 1024 * 1024

MXU_DTYPE = jnp.bfloat16
PROJ_ROWS = 1024
MLA_TQ = 1024
MLA_HPS = 4
MLA_WIDE = 2
NSA_TQ = 512
NSA_WIDE = 4
WIN_SUB = 128
RANK_UNROLL = NSA_TQ // SLC_BLOCK
OUT_ROWS = 1024
FF_CHUNK = 1024
LOG2E = math.log2(math.e)
LAZY_MAX_SLACK = 64.0
MASK_BIAS = -1e30
GATE_ROWS = 32
BF16_ROWS = 16
V_ROWS = HEAD_DIM + BF16_ROWS
EXP_DTYPE = jnp.float32

_U_CQ = 0
_U_CKV = 256
_U_KPE = 384
_U_KCV = 512
_U_KSLC = 768
_U_KWIN = 896
_U_COLS = 1024
_N_TABS = 5
_UT_QN = 0
_UT_VSLC = 512
_UT_VWIN = 640
_UT_GATE = 768
_UT_ROWS = 800
_TT_ROWS = MLA_ROPE_DIM + NSA_ROPE_DIM


def _dot(a, b):
    return jnp.dot(a, b, preferred_element_type=jnp.float32)


def _dot_nt(a, b):
    return lax.dot_general(a, b, (((1,), (1,)), ((), ())), preferred_element_type=jnp.float32)


def _dot_tn(a, b):
    return lax.dot_general(a, b, (((0,), (0,)), ((), ())), preferred_element_type=jnp.float32)


def _rms(x, g):
    return x * lax.rsqrt(jnp.mean(x * x, axis=-1, keepdims=True) + NORM_EPS) * g


def _lane(shape):
    return lax.broadcasted_iota(jnp.int32, shape, len(shape) - 1)


def _swap_pairs(x, lo_end, width):
    n = x.shape[-1]
    lane = _lane(x.shape) % LANES
    return jnp.where(lane < lo_end, pltpu.roll(x, n - width, axis=1), pltpu.roll(x, width, axis=1))


def _nsa_rope(x, cn, sn):
    half = NSA_ROPE_DIM // 2
    lane = _lane(x.shape) % HEAD_DIM
    sw = jnp.where(lane < half, pltpu.roll(x, LANES - half, axis=1), pltpu.roll(x, half, axis=1))
    return x * cn + sw * sn


def _rope_rows(x1, x2, c, s):
    return x1 * c - x2 * s, x2 * c + x1 * s


def _proj_kernel(x_ref, tab_ref, tabt_ref, gmix_ref, win_ref, wint_ref, gcq_ref, wqt_ref, gckv_ref, wk_ref, wvt_ref,
                 qmt_ref, km_ref, vmt_ref, qnt_ref, ksa_ref, vst_ref, kwp_ref, vwt_ref, kcv_ref, gatet_ref):
    x = x_ref[0]
    n = _rms(x, gmix_ref[...]).astype(MXU_DTYPE)
    u = _dot(n, win_ref[...])
    ut = _dot_nt(wint_ref[...], n)

    ck_t, sk_t, cn_t, sn_t, eneg_t = (tab_ref[:, t * LANES:(t + 1) * LANES] for t in range(_N_TABS))
    hm = MLA_ROPE_DIM // 2
    hn = NSA_ROPE_DIM // 2
    cm_r = tabt_ref[0:hm, :]
    sm_r = tabt_ref[hm:2 * hm, :]
    cn_r = tabt_ref[2 * hm:2 * hm + hn, :]
    sn_r = tabt_ref[2 * hm + hn:2 * hm + 2 * hn, :]
    low = _lane((x.shape[0], LANES)) < HEAD_DIM

    mla_scale = (MLA_NOPE_DIM + MLA_ROPE_DIM) ** -0.5 * LOG2E
    cqn = _rms(u[:, _U_CQ:_U_CQ + MLA_Q_RANK], gcq_ref[...]).astype(MXU_DTYPE)
    qmt = _dot_nt(wqt_ref[...], cqn)
    r0 = MLA_NOPE_DIM
    for h in range(MLA_HEADS):
        xh = qmt[h * LANES:(h + 1) * LANES]
        p1, p2 = _rope_rows(xh[r0:r0 + hm], xh[r0 + hm:r0 + 2 * hm], cm_r, sm_r)
        qh = jnp.concatenate([xh[0:r0], p1, p2, xh[r0 + 2 * hm:]], axis=0) * mla_scale
        qmt_ref[0, h * LANES:(h + 1) * LANES, :] = qh.astype(qmt_ref.dtype)

    ckvn = _rms(u[:, _U_CKV:_U_CKV + MLA_KV_RANK], gckv_ref[...]).astype(MXU_DTYPE)
    km = _dot(ckvn, wk_ref[...])
    kpe = u[:, _U_KPE:_U_KPE + LANES]
    kpe = kpe * ck_t + _swap_pairs(kpe, MLA_NOPE_DIM + hm, hm) * sk_t
    for h in range(MLA_HEADS):
        km_ref[0, :, h * LANES:(h + 1) * LANES] = (km[:, h * LANES:(h + 1) * LANES] + kpe).astype(km_ref.dtype)
    ones_tile = jnp.where(lax.broadcasted_iota(jnp.int32, (BF16_ROWS, x.shape[0]), 0) == 0, 1.0, 0.0)

    def put_values(ref, vt):
        for h in range(vt.shape[0] // HEAD_DIM):
            ref[0, h * V_ROWS:h * V_ROWS + HEAD_DIM, :] = vt[h * HEAD_DIM:(h + 1) * HEAD_DIM].astype(ref.dtype)
            ref[0, h * V_ROWS + HEAD_DIM:(h + 1) * V_ROWS, :] = ones_tile.astype(ref.dtype)

    put_values(vmt_ref, _dot_nt(wvt_ref[...], ckvn))

    nsa_scale = HEAD_DIM ** -0.5 * LOG2E
    for h in range(NSA_HEADS):
        xh = ut[_UT_QN + h * HEAD_DIM:_UT_QN + (h + 1) * HEAD_DIM]
        p1, p2 = _rope_rows(xh[0:hn], xh[hn:2 * hn], cn_r, sn_r)
        qh = jnp.concatenate([p1, p2, xh[2 * hn:]], axis=0) * nsa_scale
        qnt_ref[0, h * HEAD_DIM:(h + 1) * HEAD_DIM, :] = qh.astype(qnt_ref.dtype)

    ks = _nsa_rope(u[:, _U_KSLC:_U_KSLC + LANES], cn_t, sn_t)
    ks_r = pltpu.roll(ks, HEAD_DIM, axis=1)
    ksa_ref[0, :, 0:LANES] = jnp.where(low, ks, eneg_t).astype(ksa_ref.dtype)
    ksa_ref[0, :, LANES:2 * LANES] = jnp.where(low, ks_r, eneg_t).astype(ksa_ref.dtype)

    kw = _nsa_rope(u[:, _U_KWIN:_U_KWIN + LANES], cn_t, sn_t)
    kw_r = pltpu.roll(kw, HEAD_DIM, axis=1)
    kwp_ref[0, :, 0:LANES] = jnp.where(low, kw, 0.0).astype(kwp_ref.dtype)
    kwp_ref[0, :, LANES:2 * LANES] = jnp.where(low, kw_r, 0.0).astype(kwp_ref.dtype)

    put_values(vst_ref, ut[_UT_VSLC:_UT_VSLC + NSA_KV_WIDTH])
    put_values(vwt_ref, ut[_UT_VWIN:_UT_VWIN + NSA_KV_WIDTH])
    gatet_ref[0] = ut[_UT_GATE:_UT_GATE + GATE_ROWS]
    kcv_ref[0, 0] = u[:, _U_KCV:_U_KCV + LANES]
    kcv_ref[0, 1] = u[:, _U_KCV + LANES:_U_KCV + 2 * LANES]


def _proj_call(x, tabs, tabt, gmix, win, wint, gcq, wqt, gckv, wk, wvt):
    B, S, D = x.shape
    rows = PROJ_ROWS
    ns = S // rows
    tok = lambda w: pl.BlockSpec((1, rows, w), lambda i, b: (b, i, 0))
    feat = lambda r: pl.BlockSpec((1, r, rows), lambda i, b: (b, 0, i))
    full = lambda a: pl.BlockSpec(a.shape, lambda i, b: (0,) * a.ndim)
    bf = MXU_DTYPE
    outs = [
        (jax.ShapeDtypeStruct((B, MLA_HEADS * LANES, S), bf), feat(MLA_HEADS * LANES)),
        (jax.ShapeDtypeStruct((B, S, MLA_HEADS * LANES), bf), tok(MLA_HEADS * LANES)),
        (jax.ShapeDtypeStruct((B, MLA_HEADS * V_ROWS, S), bf), feat(MLA_HEADS * V_ROWS)),
        (jax.ShapeDtypeStruct((B, NSA_WIDTH, S), bf), feat(NSA_WIDTH)),
        (jax.ShapeDtypeStruct((B, S, 2 * LANES), bf), tok(2 * LANES)),
        (jax.ShapeDtypeStruct((B, NSA_KV_GROUPS * V_ROWS, S), bf), feat(NSA_KV_GROUPS * V_ROWS)),
        (jax.ShapeDtypeStruct((B, S, 2 * LANES), bf), tok(2 * LANES)),
        (jax.ShapeDtypeStruct((B, NSA_KV_GROUPS * V_ROWS, S), bf), feat(NSA_KV_GROUPS * V_ROWS)),
        (jax.ShapeDtypeStruct((B, 2, S, LANES), jnp.float32),
         pl.BlockSpec((1, 2, rows, LANES), lambda i, b: (b, 0, i, 0))),
        (jax.ShapeDtypeStruct((B, GATE_ROWS, S), jnp.float32), feat(GATE_ROWS)),
    ]
    return pl.pallas_call(
        _proj_kernel,
        grid=(ns, B),
        in_specs=[tok(D), pl.BlockSpec((rows, _N_TABS * LANES), lambda i, b: (i, 0)),
                  pl.BlockSpec((_TT_ROWS, rows), lambda i, b: (0, i)),
                  full(gmix), full(win), full(wint), full(gcq), full(wqt), full(gckv), full(wk), full(wvt)],
        out_specs=[o[1] for o in outs],
        out_shape=[o[0] for o in outs],
        compiler_params=pltpu.CompilerParams(
            dimension_semantics=("arbitrary", "arbitrary"), vmem_limit_bytes=VMEM_LIMIT),
        name="proj",
    )(x, tabs, tabt, gmix, win, wint, gcq, wqt, gckv, wk, wvt)


def _gelu_tanh(x):
    return 0.5 * x * (1.0 + jnp.tanh(math.sqrt(2.0 / math.pi) * (x + 0.044715 * (x * x * x))))


def _compress_kernel(x_ref, pea_ref, peb_ref, wa_ref, wb_ref, b1_ref, w2k_ref, b2k_ref, w2vt_ref, b2vt_ref,
                     ctab_ref, kc_ref, vct_ref):
    nch = kc_ref.shape[1]
    a = b = None
    for t in range(CMP_STRIDE):
        for c in range(2):
            cols = slice((2 * t + c) * LANES, (2 * t + c + 1) * LANES)
            xt = x_ref[0, c, pl.ds(t, nch, stride=CMP_STRIDE), :]
            at = _dot((xt + pea_ref[:, cols]).astype(MXU_DTYPE), wa_ref[cols, :])
            bt = _dot((xt + peb_ref[:, cols]).astype(MXU_DTYPE), wb_ref[cols, :])
            a = at if a is None else a + at
            b = bt if b is None else b + bt
    hid = _gelu_tanh(a + pltpu.roll(b, nch - 1, axis=0) + b1_ref[...]).astype(MXU_DTYPE)
    cc = ctab_ref[:, 0:LANES]
    sc = ctab_ref[:, LANES:2 * LANES]
    for g in range(NSA_KV_GROUPS):
        hk = hid[:, g * LANES:(g + 1) * LANES]
        kc = _dot(hk, w2k_ref[...]) + b2k_ref[...]
        kc_ref[0, :, g * LANES:(g + 1) * LANES] = _nsa_rope(kc, cc, sc).astype(kc_ref.dtype)
        hv = hid[:, (NSA_KV_GROUPS + g) * LANES:(NSA_KV_GROUPS + g + 1) * LANES]
        vct_ref[0, g * HEAD_DIM:(g + 1) * HEAD_DIM, :] = (
            _dot_nt(w2vt_ref[...], hv) + b2vt_ref[...]).astype(vct_ref.dtype)


def _compress_call(kcv, consts, ctab):
    B, _, S, _ = kcv.shape
    nch = S // CMP_STRIDE
    full = lambda a: pl.BlockSpec(a.shape, lambda b: (0,) * a.ndim)
    return pl.pallas_call(
        _compress_kernel,
        grid=(B,),
        in_specs=[pl.BlockSpec((1, 2, S, LANES), lambda b: (b, 0, 0, 0))] + [full(c) for c in consts] + [full(ctab)],
        out_specs=[pl.BlockSpec((1, nch, 2 * LANES), lambda b: (b, 0, 0)),
                   pl.BlockSpec((1, NSA_KV_WIDTH, nch), lambda b: (b, 0, 0))],
        out_shape=[jax.ShapeDtypeStruct((B, nch, 2 * LANES), MXU_DTYPE),
                   jax.ShapeDtypeStruct((B, NSA_KV_WIDTH, nch), MXU_DTYPE)],
        compiler_params=pltpu.CompilerParams(
            dimension_semantics=("arbitrary",), vmem_limit_bytes=VMEM_LIMIT),
        name="compress",
    )(kcv, *consts, ctab)


def _numerators(s, m):
    return jnp.exp2((s - m).astype(EXP_DTYPE)).astype(MXU_DTYPE)


def _normalise(acc):
    return acc[0:HEAD_DIM] * (1.0 / acc[HEAD_DIM:HEAD_DIM + 1])


def _own_scores(qt, k_tile):
    n = k_tile.shape[0]
    kt = k_tile.astype(jnp.float32).T
    return jnp.concatenate(
        [jnp.sum(qt[:, h * n:(h + 1) * n].astype(jnp.float32) * kt, axis=0, keepdims=True)
         for h in range(qt.shape[1] // n)], axis=1)


def _fast_start(shifts):
    return tuple((jnp.zeros((V_ROWS, m.shape[1]), jnp.float32), jnp.full(m.shape, -jnp.inf, jnp.float32))
                 for m in shifts)


def _fast_tile(carry, shifts, scores, pv, start, width, mask=None):
    out = []
    for st, (acc, top) in enumerate(carry):
        s = scores(st, start, width)
        if mask is not None:
            s = jnp.where(mask, s, -jnp.inf)
        out.append((acc + pv(st, _numerators(s, shifts[st]), start, width),
                    jnp.maximum(top, jnp.max(s, axis=0, keepdims=True))))
    return tuple(out)


def _fast_excess(carry, shifts):
    gaps = [jnp.max(top - m) for (acc, top), m in zip(carry, shifts)]
    excess = gaps[0]
    for g in gaps[1:]:
        excess = jnp.maximum(excess, g)
    return excess


def _tile_starts(i, tq, wide):
    return wide * tq, i // wide


def _fast_streams(shifts, scores, pv, lower, i, tq, wide):
    carry = _fast_tile(_fast_start(shifts), shifts, scores, pv, pl.multiple_of(i * tq, tq), tq, lower)
    wt, n_wide = _tile_starts(i, tq, wide)
    carry = lax.fori_loop(
        0, n_wide, lambda jj, c: _fast_tile(c, shifts, scores, pv, pl.multiple_of(jj * wt, wt), wt), carry)
    return lax.fori_loop(
        n_wide * wide, i, lambda jj, c: _fast_tile(c, shifts, scores, pv, pl.multiple_of(jj * tq, tq), tq), carry)


def _exact_streams(n_streams, scores, pv, lower, i, tq, wide):
    start_d = pl.multiple_of(i * tq, tq)
    carry = []
    for st in range(n_streams):
        s = jnp.where(lower, scores(st, start_d, tq), -jnp.inf)
        m = jnp.max(s, axis=0, keepdims=True)
        carry.append((m, pv(st, _numerators(s, m), start_d, tq)))

    def step(start, width, carry):
        out = []
        for st, (m, acc) in enumerate(carry):
            s = scores(st, start, width)
            m_new = jnp.maximum(m, jnp.max(s, axis=0, keepdims=True))
            out.append((m_new, jnp.exp2(m - m_new) * acc + pv(st, _numerators(s, m_new), start, width)))
        return tuple(out)

    wt, n_wide = _tile_starts(i, tq, wide)
    carry = lax.fori_loop(0, n_wide, lambda jj, c: step(pl.multiple_of(jj * wt, wt), wt, c), tuple(carry))
    carry = lax.fori_loop(n_wide * wide, i, lambda jj, c: step(pl.multiple_of(jj * tq, tq), tq, c), carry)
    return tuple(_normalise(acc) for (m, acc) in carry)


def _mla_kernel(qt_ref, k_ref, vt_ref, o_ref):
    i = pl.program_id(2)
    tq = qt_ref.shape[2]
    lower = lax.broadcasted_iota(jnp.int32, (tq, tq), 0) <= lax.broadcasted_iota(jnp.int32, (tq, tq), 1)
    qs = [qt_ref[0, hh * LANES:(hh + 1) * LANES, :] for hh in range(MLA_HPS)]

    def scores(hh, start, width):
        return _dot(k_ref[0, pl.ds(start, width), hh * LANES:(hh + 1) * LANES], qs[hh])

    def pv(hh, p, start, width):
        return _dot(vt_ref[0, hh * V_ROWS:(hh + 1) * V_ROWS, pl.ds(start, width)], p)

    own_rows = pl.ds(pl.multiple_of(i * tq, tq), tq)
    shifts = [_own_scores(qs[hh], k_ref[0, own_rows, hh * LANES:(hh + 1) * LANES]) for hh in range(MLA_HPS)]
    fast = _fast_streams(shifts, scores, pv, lower, i, tq, MLA_WIDE)
    outs = lax.cond(_fast_excess(fast, shifts) > LAZY_MAX_SLACK,
                    lambda _: _exact_streams(MLA_HPS, scores, pv, lower, i, tq, MLA_WIDE),
                    lambda _: tuple(_normalise(acc) for acc, top in fast), None)
    for hh, o in enumerate(outs):
        o_ref[0, hh * MLA_V_DIM:(hh + 1) * MLA_V_DIM, :] = o


def _mla_call(qmt, km, vmt):
    B, S, _ = km.shape
    tq = MLA_TQ
    hps = MLA_HPS
    return pl.pallas_call(
        _mla_kernel,
        grid=(B, MLA_HEADS // hps, S // tq),
        in_specs=[pl.BlockSpec((1, hps * LANES, tq), lambda b, p, i: (b, p, i)),
                  pl.BlockSpec((1, S, hps * LANES), lambda b, p, i: (b, 0, p)),
                  pl.BlockSpec((1, hps * V_ROWS, S), lambda b, p, i: (b, p, 0))],
        out_specs=pl.BlockSpec((1, hps * MLA_V_DIM, tq), lambda b, p, i: (b, p, i)),
        out_shape=jax.ShapeDtypeStruct((B, MLA_WIDTH, S), jnp.float32),
        compiler_params=pltpu.CompilerParams(
            dimension_semantics=("arbitrary", "arbitrary", "arbitrary"), vmem_limit_bytes=VMEM_LIMIT),
        name="mla",
    )(qmt, km, vmt)


def _group_queries(q_ref, g, rows):
    return jnp.concatenate(
        [q_ref[0, (g * NSA_HPG + h) * rows:(g * NSA_HPG + h + 1) * rows, :] for h in range(NSA_HPG)], axis=1)


def _nsa_cmp_kernel(qt_ref, kc_ref, vct_ref, ovt_ref, oct_ref, qaugt_ref, score_ref):
    i = pl.program_id(1)
    tq = qt_ref.shape[2]
    nch = kc_ref.shape[1]
    nblk = score_ref.shape[1]
    cols4 = NSA_HPG * tq
    q0 = i * tq
    qpos_c = q0 + lax.broadcasted_iota(jnp.int32, (nch, cols4), 1) % tq
    cmp_end = lax.broadcasted_iota(jnp.int32, (nch, cols4), 0) * CMP_STRIDE + (CMP_BLOCK - 1)
    mask_c = cmp_end <= qpos_c

    blk = lax.broadcasted_iota(jnp.int32, (nblk, tq), 0)
    qpos_r = q0 + lax.broadcasted_iota(jnp.int32, (nblk, tq), 1)
    cur = qpos_r // SLC_BLOCK
    forced = (blk == 0) | (blk == cur) | (blk == cur - 1)
    causal = blk * SLC_BLOCK <= qpos_r
    groups = range(NSA_KV_GROUPS)

    scores = []
    for g in groups:
        q4 = _group_queries(qt_ref, g, HEAD_DIM)
        qpad = jnp.concatenate([q4, jnp.zeros_like(q4)], axis=0)
        s = jnp.where(mask_c, _dot(kc_ref[0, :, g * LANES:(g + 1) * LANES], qpad), -jnp.inf)
        m = jnp.max(s, axis=0, keepdims=True)
        m = jnp.where(m == -jnp.inf, 0.0, m)
        e = jnp.exp2(s - m)
        pr = e / jnp.maximum(jnp.sum(e, axis=0, keepdims=True), 1e-30)
        o = _dot(vct_ref[0, g * HEAD_DIM:(g + 1) * HEAD_DIM, :], pr.astype(MXU_DTYPE))
        for h in range(NSA_HPG):
            head = g * NSA_HPG + h
            oct_ref[0, head * HEAD_DIM:(head + 1) * HEAD_DIM, :] = o[:, h * tq:(h + 1) * tq]

        psum = (pr[:, 0:tq] + pr[:, tq:2 * tq]) + (pr[:, 2 * tq:3 * tq] + pr[:, 3 * tq:4 * tq])
        p_hi = psum.astype(MXU_DTYPE)
        p_lo = (psum - p_hi.astype(jnp.float32)).astype(MXU_DTYPE)
        imp = _dot(ovt_ref[...], p_hi) + _dot(ovt_ref[...], p_lo)
        score = jnp.where(forced, FORCE_SCORE, jnp.where(causal, imp, -FORCE_SCORE))
        score_ref[g] = score
        scores.append(score)

    def count(c, cnts):
        out = list(cnts)
        for t in range(RANK_UNROLL):
            r = c * RANK_UNROLL + t
            tie = jnp.where(r < blk, 1.0, 0.0)
            for g in groups:
                row = score_ref[g, pl.ds(r, 1), :]
                out[g] = out[g] + jnp.where(row > scores[g], 1.0, jnp.where(row == scores[g], tie, 0.0))
        return tuple(out)

    zero = jnp.zeros((nblk, tq), jnp.float32)
    cnts = lax.fori_loop(0, i + 1, count, (zero,) * NSA_KV_GROUPS)
    for g in groups:
        notsel = jnp.where(cnts[g] < float(SLC_TOPK), 0.0, 1.0).astype(qaugt_ref.dtype)
        for h in range(NSA_HPG):
            head = g * NSA_HPG + h
            qaugt_ref[0, head * LANES:head * LANES + HEAD_DIM, :] = qt_ref[0, head * HEAD_DIM:(head + 1) * HEAD_DIM, :]
            qaugt_ref[0, head * LANES + HEAD_DIM:(head + 1) * LANES, :] = notsel


def _nsa_cmp_call(qnt, kc, vct, ovt):
    B, _, S = qnt.shape
    tq = NSA_TQ
    nch = kc.shape[1]
    return pl.pallas_call(
        _nsa_cmp_kernel,
        grid=(B, S // tq),
        in_specs=[pl.BlockSpec((1, NSA_WIDTH, tq), lambda b, i: (b, 0, i)),
                  pl.BlockSpec((1, nch, 2 * LANES), lambda b, i: (b, 0, 0)),
                  pl.BlockSpec((1, NSA_KV_WIDTH, nch), lambda b, i: (b, 0, 0)),
                  pl.BlockSpec(ovt.shape, lambda b, i: (0, 0))],
        out_specs=[pl.BlockSpec((1, NSA_WIDTH, tq), lambda b, i: (b, 0, i)),
                   pl.BlockSpec((1, NSA_HEADS * LANES, tq), lambda b, i: (b, 0, i))],
        out_shape=[jax.ShapeDtypeStruct((B, NSA_WIDTH, S), jnp.float32),
                   jax.ShapeDtypeStruct((B, NSA_HEADS * LANES, S), MXU_DTYPE)],
        scratch_shapes=[pltpu.VMEM((NSA_KV_GROUPS, LANES - HEAD_DIM, tq), jnp.float32)],
        compiler_params=pltpu.CompilerParams(
            dimension_semantics=("arbitrary", "arbitrary"), vmem_limit_bytes=VMEM_LIMIT),
        name="nsa_cmp",
    )(qnt, kc, vct, ovt)


def _nsa_attn_kernel(qaugt_ref, ksa_ref, vst_ref, kwp_ref, vwt_ref, oct_ref, gatet_ref, y_ref):
    i = pl.program_id(1)
    tq = qaugt_ref.shape[2]
    cols4 = NSA_HPG * tq
    groups = range(NSA_KV_GROUPS)
    lower = (lax.broadcasted_iota(jnp.int32, (tq, cols4), 0)
             <= lax.broadcasted_iota(jnp.int32, (tq, cols4), 1) % tq)
    qs = [_group_queries(qaugt_ref, g, LANES) for g in groups]

    def pv_from(vt_ref):
        def pv(g, p, start, width):
            return _dot(vt_ref[0, g * V_ROWS:(g + 1) * V_ROWS, pl.ds(start, width)], p)
        return pv

    pv_s = pv_from(vst_ref)
    pv_w = pv_from(vwt_ref)

    def s_scores(g, start, width):
        return _dot(ksa_ref[0, pl.ds(start, width), g * LANES:(g + 1) * LANES], qs[g])

    sub = WIN_SUB
    wk = WINDOW + sub
    nsub = tq // sub
    rel = (lax.broadcasted_iota(jnp.int32, (wk, NSA_HPG * sub), 1) % sub
           - lax.broadcasted_iota(jnp.int32, (wk, NSA_HPG * sub), 0))

    def window_tile(j):
        q_lo = i * tq + j * sub
        wstart = pl.multiple_of(jnp.maximum(q_lo - WINDOW, 0), sub)
        dist = rel + (q_lo - wstart)
        q_subs = [jnp.concatenate(
            [qs[g][:, h * tq + j * sub:h * tq + (j + 1) * sub] for h in range(NSA_HPG)], axis=1) for g in groups]

        def w_scores(g, start, width):
            return _dot(kwp_ref[0, pl.ds(start, width), g * LANES:(g + 1) * LANES], q_subs[g])

        return q_lo, wstart, (dist >= 0) & (dist < WINDOW), q_subs, w_scores

    own_rows = pl.ds(pl.multiple_of(i * tq, tq), tq)
    s_shifts = [_own_scores(qs[g], ksa_ref[0, own_rows, g * LANES:(g + 1) * LANES]) for g in groups]
    fast_s = _fast_streams(s_shifts, s_scores, pv_s, lower, i, tq, NSA_WIDE)
    excess = _fast_excess(fast_s, s_shifts)
    fast_w = []
    for j in range(nsub):
        q_lo, wstart, band, q_subs, w_scores = window_tile(j)
        sub_rows = pl.ds(pl.multiple_of(q_lo, sub), sub)
        w_shifts = [_own_scores(q_subs[g], kwp_ref[0, sub_rows, g * LANES:(g + 1) * LANES]) for g in groups]
        carry = _fast_tile(_fast_start(w_shifts), w_shifts, w_scores, pv_w, wstart, wk, band)
        excess = jnp.maximum(excess, _fast_excess(carry, w_shifts))
        fast_w.append(tuple(_normalise(acc) for acc, top in carry))

    def exact(_):
        o_s = _exact_streams(NSA_KV_GROUPS, s_scores, pv_s, lower, i, tq, NSA_WIDE)
        o_w = []
        for j in range(nsub):
            q_lo, wstart, band, q_subs, w_scores = window_tile(j)
            outs = []
            for g in groups:
                s = jnp.where(band, w_scores(g, wstart, wk), -jnp.inf)
                p = _numerators(s, jnp.max(s, axis=0, keepdims=True))
                outs.append(_normalise(pv_w(g, p, wstart, wk)))
            o_w.append(tuple(outs))
        return o_s, tuple(o_w)

    o_slc, o_wins = lax.cond(
        excess > LAZY_MAX_SLACK, exact,
        lambda _: (tuple(_normalise(acc) for acc, top in fast_s), tuple(fast_w)), None)
    o_win = [[o_wins[j][g] for j in range(nsub)] for g in groups]

    sig = jax.nn.sigmoid(gatet_ref[0])
    for g in groups:
        for h in range(NSA_HPG):
            head = g * NSA_HPG + h
            o_w = jnp.concatenate([o_win[g][j][:, h * sub:(h + 1) * sub] for j in range(nsub)], axis=1)
            gate = lambda br: sig[br * NSA_HEADS + head:br * NSA_HEADS + head + 1, :]
            y_ref[0, head * HEAD_DIM:(head + 1) * HEAD_DIM, :] = (
                gate(0) * oct_ref[0, head * HEAD_DIM:(head + 1) * HEAD_DIM, :]
                + gate(1) * o_slc[g][:, h * tq:(h + 1) * tq] + gate(2) * o_w)


def _nsa_attn_call(qaugt, ksa, vst, kwp, vwt, oct, gatet):
    B, S, _ = ksa.shape
    tq = NSA_TQ
    feat = lambda r: pl.BlockSpec((1, r, tq), lambda b, i: (b, 0, i))
    seq_tok = lambda w: pl.BlockSpec((1, S, w), lambda b, i: (b, 0, 0))
    seq_feat = lambda r: pl.BlockSpec((1, r, S), lambda b, i: (b, 0, 0))
    return pl.pallas_call(
        _nsa_attn_kernel,
        grid=(B, S // tq),
        in_specs=[feat(NSA_HEADS * LANES), seq_tok(2 * LANES), seq_feat(NSA_KV_GROUPS * V_ROWS), seq_tok(2 * LANES),
                  seq_feat(NSA_KV_GROUPS * V_ROWS), feat(NSA_WIDTH), feat(GATE_ROWS)],
        out_specs=feat(NSA_WIDTH),
        out_shape=jax.ShapeDtypeStruct((B, NSA_WIDTH, S), jnp.float32),
        compiler_params=pltpu.CompilerParams(
            dimension_semantics=("arbitrary", "arbitrary"), vmem_limit_bytes=VMEM_LIMIT),
        name="nsa_attn",
    )(qaugt, ksa, vst, kwp, vwt, oct, gatet)


def _rms_cols(yt, g_col):
    return yt * lax.rsqrt(jnp.mean(yt * yt, axis=0, keepdims=True) + NORM_EPS) * g_col


def _out_kernel(x_ref, ymt_ref, ynt_ref, gm_ref, gn_ref, wo_ref, gmlp_ref, wup_ref, wdn_ref, gfin_ref, o_ref):
    mixed_t = jnp.concatenate([_rms_cols(ymt_ref[0], gm_ref[...]), _rms_cols(ynt_ref[0], gn_ref[...])], axis=0)
    h = x_ref[0] + _dot_tn(mixed_t.astype(MXU_DTYPE), wo_ref[...])
    hn = _rms(h, gmlp_ref[...]).astype(MXU_DTYPE)
    acc = h
    for c in range(D_FF // FF_CHUNK):
        a = jnp.maximum(_dot(hn, wup_ref[:, c * FF_CHUNK:(c + 1) * FF_CHUNK]), 0.0)
        acc = acc + _dot((a * a).astype(MXU_DTYPE), wdn_ref[c * FF_CHUNK:(c + 1) * FF_CHUNK, :])
    o_ref[0] = _rms(acc, gfin_ref[...])


def _out_call(x, ymt, ynt, gm, gn, wo, gmlp, wup, wdn, gfin):
    B, S, D = x.shape
    rows = OUT_ROWS
    tok = pl.BlockSpec((1, rows, D), lambda b, i: (b, i, 0))
    feat = lambda r: pl.BlockSpec((1, r, rows), lambda b, i: (b, 0, i))
    full = lambda a: pl.BlockSpec(a.shape, lambda b, i: (0,) * a.ndim)
    return pl.pallas_call(
        _out_kernel,
        grid=(B, S // rows),
        in_specs=[tok, feat(MLA_WIDTH), feat(NSA_WIDTH), full(gm), full(gn), full(wo), full(gmlp),
                  full(wup), full(wdn), full(gfin)],
        out_specs=tok,
        out_shape=jax.ShapeDtypeStruct((B, S, D), jnp.float32),
        compiler_params=pltpu.CompilerParams(
            dimension_semantics=("arbitrary", "arbitrary"), vmem_limit_bytes=VMEM_LIMIT),
        name="out",
    )(x, ymt, ynt, gm, gn, wo, gmlp, wup, wdn, gfin)


def _rope_cs(pos, dim):
    inv_freq = jnp.exp(-math.log(ROPE_THETA) * jnp.arange(0, dim, 2, dtype=jnp.float32) / dim)
    ang = pos.astype(jnp.float32)[:, None] * inv_freq[None, :]
    return jnp.cos(ang), jnp.sin(ang)


def _nsa_rope_tables(pos, width):
    c, s = _rope_cs(pos, NSA_ROPE_DIM)
    n = pos.shape[0]
    rest = HEAD_DIM - NSA_ROPE_DIM
    ch = jnp.concatenate([c, c, jnp.ones((n, rest), jnp.float32)], axis=1)
    sh = jnp.concatenate([-s, s, jnp.zeros((n, rest), jnp.float32)], axis=1)
    reps = width // HEAD_DIM
    return jnp.tile(ch, (1, reps)), jnp.tile(sh, (1, reps))


def _proj_tables(S):
    pos = jnp.arange(S)
    c, s = _rope_cs(pos, MLA_ROPE_DIM)
    z = lambda w: jnp.zeros((S, w), jnp.float32)
    pad = LANES - MLA_NOPE_DIM - MLA_ROPE_DIM
    ck = jnp.concatenate([z(MLA_NOPE_DIM), c, c, z(pad)], axis=1)
    sk = jnp.concatenate([z(MLA_NOPE_DIM), -s, s, z(pad)], axis=1)
    cn, sn = _nsa_rope_tables(pos, LANES)
    onehot = (pos[:, None] // SLC_BLOCK == jnp.arange(LANES - HEAD_DIM)[None, :]).astype(jnp.float32)
    eneg = jnp.concatenate([z(HEAD_DIM), MASK_BIAS * onehot], axis=1)
    cn8, sn8 = _rope_cs(pos, NSA_ROPE_DIM)
    tabt = jnp.concatenate([c.T, s.T, cn8.T, sn8.T], axis=0)
    return jnp.concatenate([ck, sk, cn, sn, eneg], axis=1), tabt


def _compress_tables(nch):
    end = jnp.arange(nch) * CMP_STRIDE + CMP_BLOCK - 1
    c, s = _nsa_rope_tables(end, HEAD_DIM)
    z = jnp.zeros((nch, LANES - HEAD_DIM), jnp.float32)
    return jnp.concatenate([c, z, s, z], axis=1)


def _overlap_t(nch, nblk):
    cs = jnp.arange(nch)[None, :] * CMP_STRIDE
    ss = jnp.arange(nblk)[:, None] * SLC_BLOCK
    ov = jnp.clip(jnp.minimum(cs + CMP_BLOCK, ss + SLC_BLOCK) - jnp.maximum(cs, ss), 0, None)
    ov = ov.astype(jnp.float32) / CMP_BLOCK
    ov = jnp.where(jnp.arange(nch)[None, :] < nch - 1, ov, 0.0)
    return ov.astype(MXU_DTYPE)


def _pad_cols(w, width):
    return jnp.pad(w, ((0, 0), (0, width - w.shape[1])))


def _compress_consts(pe_k, w1_k, b1_k, w2_k, b2_k, pe_v, w1_v, b1_v, w2_v, b2_v):
    half = CMP_BLOCK // 2
    ncol = 2 * NSA_KV_GROUPS
    w1 = jnp.stack([w1_k, w1_k, w1_v, w1_v]).reshape(ncol, 2, half, HEAD_DIM, CMP_HIDDEN)
    eye = jnp.eye(ncol, dtype=w1.dtype)

    def spread(wh):
        return jnp.einsum('cldh,ce->lcdeh', wh, eye).reshape(half * ncol * HEAD_DIM, ncol * CMP_HIDDEN)

    pe = jnp.stack([pe_k, pe_k, pe_v, pe_v]).reshape(ncol, 2, half, HEAD_DIM)
    pe_row = lambda ph: ph.transpose(1, 0, 2).reshape(1, half * ncol * HEAD_DIM)
    b1 = jnp.concatenate([b1_k, b1_k, b1_v, b1_v]).reshape(1, ncol * CMP_HIDDEN)
    return [
        pe_row(pe[:, 0]), pe_row(pe[:, 1]),
        spread(w1[:, 0]).astype(MXU_DTYPE), spread(w1[:, 1]).astype(MXU_DTYPE), b1,
        _pad_cols(w2_k, LANES).astype(MXU_DTYPE), _pad_cols(b2_k[None, :], LANES),
        w2_v.T.astype(MXU_DTYPE), b2_v[:, None],
    ]


def _proj_weights(w_in, w_uq, w_ukv):
    d = w_in.shape[0]
    z = lambda w: jnp.zeros((d, w), w_in.dtype)
    o_kr = MLA_Q_RANK + MLA_KV_RANK
    o_qn = o_kr + MLA_ROPE_DIM
    o_kv = o_qn + NSA_WIDTH
    o_gate = o_kv + 6 * NSA_KV_WIDTH
    kv = lambda t: w_in[:, o_kv + t * NSA_KV_WIDTH:o_kv + (t + 1) * NSA_KV_WIDTH]
    win = jnp.concatenate([
        w_in[:, :o_kr],
        z(MLA_NOPE_DIM), w_in[:, o_kr:o_qn], z(LANES - MLA_NOPE_DIM - MLA_ROPE_DIM),
        kv(0), kv(1), kv(2), kv(4)], axis=1)
    assert win.shape[1] == _U_COLS
    wint = jnp.concatenate([w_in[:, o_qn:o_kv], kv(3), kv(5), w_in[:, o_gate:], z(GATE_ROWS - NSA_GATES)], axis=1).T
    assert wint.shape[0] == _UT_ROWS
    qd = MLA_NOPE_DIM + MLA_ROPE_DIM
    wq = jnp.pad(w_uq.reshape(MLA_Q_RANK, MLA_HEADS, qd), ((0, 0), (0, 0), (0, LANES - qd)))
    wqt = wq.reshape(MLA_Q_RANK, MLA_HEADS * LANES).T
    kvw = w_ukv.reshape(MLA_KV_RANK, MLA_HEADS, MLA_NOPE_DIM + MLA_V_DIM)
    zk = jnp.zeros((MLA_KV_RANK, MLA_HEADS, LANES - MLA_NOPE_DIM), w_ukv.dtype)
    wk = jnp.concatenate([kvw[..., :MLA_NOPE_DIM], zk], axis=-1).reshape(MLA_KV_RANK, MLA_HEADS * LANES)
    wvt = kvw[..., MLA_NOPE_DIM:].reshape(MLA_KV_RANK, MLA_WIDTH).T
    return (win.astype(MXU_DTYPE), wint.astype(MXU_DTYPE), wqt.astype(MXU_DTYPE), wk.astype(MXU_DTYPE),
            wvt.astype(MXU_DTYPE))


def kernel(x, g_mix_norm, w_in, g_cq, w_uq, g_ckv, w_ukv, cmp_pe_k, cmp_w1_k, cmp_b1_k, cmp_w2_k, cmp_b2_k,
           cmp_pe_v, cmp_w1_v, cmp_b1_v, cmp_w2_v, cmp_b2_v, g_out_mla, g_out_nsa, w_o, g_mlp_norm, w_up,
           w_down, g_final):
    B, S, D = x.shape
    nch = S // CMP_STRIDE
    nblk = S // SLC_BLOCK
    assert w_in.shape[0] == 1
    assert D == D_MODEL and S % MLA_TQ == 0 and S % PROJ_ROWS == 0 and S % OUT_ROWS == 0
    assert S % NSA_TQ == 0 and NSA_TQ % WIN_SUB == 0 and WINDOW % WIN_SUB == 0 and S >= WINDOW + WIN_SUB
    assert SLC_TOPK <= nblk <= LANES - HEAD_DIM
    row = lambda g: g.reshape(1, -1)
    col = lambda g: g.reshape(-1, 1)

    tabs, tabt = _proj_tables(S)
    ctab = _compress_tables(nch)
    ovt = _overlap_t(nch, LANES - HEAD_DIM)

    win, wint, wqt, wk, wvt = _proj_weights(w_in[0], w_uq[0], w_ukv[0])
    qmt, km, vmt, qnt, ksa, vst, kwp, vwt, kcv, gatet = _proj_call(
        x, tabs, tabt, row(g_mix_norm[0]), win, wint, row(g_cq[0]), wqt, row(g_ckv[0]), wk, wvt)
    consts = _compress_consts(cmp_pe_k[0], cmp_w1_k[0], cmp_b1_k[0], cmp_w2_k[0], cmp_b2_k[0],
                              cmp_pe_v[0], cmp_w1_v[0], cmp_b1_v[0], cmp_w2_v[0], cmp_b2_v[0])
    kc, vct = _compress_call(kcv, consts, ctab)
    y_mla_t = _mla_call(qmt, km, vmt)
    oct, qaugt = _nsa_cmp_call(qnt, kc, vct, ovt)
    y_nsa_t = _nsa_attn_call(qaugt, ksa, vst, kwp, vwt, oct, gatet)
    return _out_call(x, y_mla_t, y_nsa_t, col(g_out_mla[0]), col(g_out_nsa[0]), w_o[0].astype(MXU_DTYPE),
                     row(g_mlp_norm[0]), w_up[0].astype(MXU_DTYPE), w_down[0].astype(MXU_DTYPE), row(g_final))
```

```python
import math

import jax
import jax.numpy as jnp
from jax import lax
from jax.experimental import pallas as pl
from jax.experimental.pallas import tpu as pltpu

D_MODEL = 1024
HEAD_DIM = 64
ROPE_THETA = 500000.0
NORM_EPS = 1e-6

MLA_HEADS = 8
MLA_Q_RANK = 256
MLA_KV_RANK = 128
MLA_NOPE_DIM = 64
MLA_ROPE_DIM = 32
MLA_V_DIM = 64
MLA_WIDTH = MLA_HEADS * MLA_V_DIM

NSA_HEADS = 8
NSA_KV_GROUPS = 2
NSA_HPG = NSA_HEADS // NSA_KV_GROUPS
NSA_ROPE_DIM = HEAD_DIM // 4
NSA_WIDTH = NSA_HEADS * HEAD_DIM
NSA_KV_WIDTH = NSA_KV_GROUPS * HEAD_DIM
NSA_GATES = 3 * NSA_HEADS
CMP_BLOCK = 32
CMP_STRIDE = 16
CMP_HIDDEN = 2 * HEAD_DIM
SLC_BLOCK = 64
SLC_TOPK = 16
WINDOW = 512
FORCE_SCORE = 1e9
D_FF = 4 * D_MODEL

LANES = 128
SUBLANES = 8
VMEM_LIMIT = 56 * 1024 * 1024

MXU_DTYPE = jnp.bfloat16
PROJ_ROWS = 1024
MLA_TQ = 1024
MLA_HPS = 4
MLA_WIDE = 2
NSA_TQ = 512
NSA_WIDE = 4
WIN_SUB = 128
RANK_UNROLL = NSA_TQ // SLC_BLOCK
OUT_ROWS = 1024
FF_CHUNK = 1024
LOG2E = math.log2(math.e)
LAZY_MAX_SLACK = 64.0
MASK_BIAS = -1e30
GATE_ROWS = 32
BF16_ROWS = 16
V_ROWS = HEAD_DIM + BF16_ROWS
EXP_DTYPE = jnp.float32
Y_DTYPE = jnp.bfloat16

_U_CQ = 0
_U_CKV = 256
_U_KPE = 384
_U_KCV = 512
_U_KSLC = 768
_U_KWIN = 896
_U_COLS = 1024
_N_TABS = 5
_UT_QN = 0
_UT_VSLC = 512
_UT_VWIN = 640
_UT_GATE = 768
_UT_ROWS = 800
_TT_ROWS = MLA_ROPE_DIM + NSA_ROPE_DIM


def _dot(a, b):
    return jnp.dot(a, b, preferred_element_type=jnp.float32)


def _dot_nt(a, b):
    return lax.dot_general(a, b, (((1,), (1,)), ((), ())), preferred_element_type=jnp.float32)


def _dot_tn(a, b):
    return lax.dot_general(a, b, (((0,), (0,)), ((), ())), preferred_element_type=jnp.float32)


def _rms(x, g):
    return x * lax.rsqrt(jnp.mean(x * x, axis=-1, keepdims=True) + NORM_EPS) * g


def _lane(shape):
    return lax.broadcasted_iota(jnp.int32, shape, len(shape) - 1)


def _swap_pairs(x, lo_end, width):
    n = x.shape[-1]
    lane = _lane(x.shape) % LANES
    return jnp.where(lane < lo_end, pltpu.roll(x, n - width, axis=1), pltpu.roll(x, width, axis=1))


def _nsa_rope(x, cn, sn):
    half = NSA_ROPE_DIM // 2
    lane = _lane(x.shape) % HEAD_DIM
    sw = jnp.where(lane < half, pltpu.roll(x, LANES - half, axis=1), pltpu.roll(x, half, axis=1))
    return x * cn + sw * sn


def _rope_rows(x1, x2, c, s):
    return x1 * c - x2 * s, x2 * c + x1 * s


def _proj_kernel(x_ref, tab_ref, tabt_ref, gmix_ref, win_ref, wint_ref, gcq_ref, wqt_ref, gckv_ref, wk_ref, wvt_ref,
                 qmt_ref, km_ref, vmt_ref, qnt_ref, ksa_ref, vst_ref, kwp_ref, vwt_ref, kcv_ref, gatet_ref):
    x = x_ref[0]
    n = _rms(x, gmix_ref[...]).astype(MXU_DTYPE)
    u = _dot(n, win_ref[...])
    ut = _dot_nt(wint_ref[...], n)

    ck_t, sk_t, cn_t, sn_t, eneg_t = (tab_ref[:, t * LANES:(t + 1) * LANES] for t in range(_N_TABS))
    hm = MLA_ROPE_DIM // 2
    hn = NSA_ROPE_DIM // 2
    cm_r = tabt_ref[0:hm, :]
    sm_r = tabt_ref[hm:2 * hm, :]
    cn_r = tabt_ref[2 * hm:2 * hm + hn, :]
    sn_r = tabt_ref[2 * hm + hn:2 * hm + 2 * hn, :]
    low = _lane((x.shape[0], LANES)) < HEAD_DIM

    mla_scale = (MLA_NOPE_DIM + MLA_ROPE_DIM) ** -0.5 * LOG2E
    cqn = _rms(u[:, _U_CQ:_U_CQ + MLA_Q_RANK], gcq_ref[...]).astype(MXU_DTYPE)
    qmt = _dot_nt(wqt_ref[...], cqn)
    r0 = MLA_NOPE_DIM
    for h in range(MLA_HEADS):
        xh = qmt[h * LANES:(h + 1) * LANES]
        p1, p2 = _rope_rows(xh[r0:r0 + hm], xh[r0 + hm:r0 + 2 * hm], cm_r, sm_r)
        qh = jnp.concatenate([xh[0:r0], p1, p2, xh[r0 + 2 * hm:]], axis=0) * mla_scale
        qmt_ref[0, h * LANES:(h + 1) * LANES, :] = qh.astype(qmt_ref.dtype)

    ckvn = _rms(u[:, _U_CKV:_U_CKV + MLA_KV_RANK], gckv_ref[...]).astype(MXU_DTYPE)
    km = _dot(ckvn, wk_ref[...])
    kpe = u[:, _U_KPE:_U_KPE + LANES]
    kpe = kpe * ck_t + _swap_pairs(kpe, MLA_NOPE_DIM + hm, hm) * sk_t
    for h in range(MLA_HEADS):
        km_ref[0, :, h * LANES:(h + 1) * LANES] = (km[:, h * LANES:(h + 1) * LANES] + kpe).astype(km_ref.dtype)
    ones_tile = jnp.where(lax.broadcasted_iota(jnp.int32, (BF16_ROWS, x.shape[0]), 0) == 0, 1.0, 0.0)

    def put_values(ref, vt):
        for h in range(vt.shape[0] // HEAD_DIM):
            ref[0, h * V_ROWS:h * V_ROWS + HEAD_DIM, :] = vt[h * HEAD_DIM:(h + 1) * HEAD_DIM].astype(ref.dtype)
            ref[0, h * V_ROWS + HEAD_DIM:(h + 1) * V_ROWS, :] = ones_tile.astype(ref.dtype)

    put_values(vmt_ref, _dot_nt(wvt_ref[...], ckvn))

    nsa_scale = HEAD_DIM ** -0.5 * LOG2E
    for h in range(NSA_HEADS):
        xh = ut[_UT_QN + h * HEAD_DIM:_UT_QN + (h + 1) * HEAD_DIM]
        p1, p2 = _rope_rows(xh[0:hn], xh[hn:2 * hn], cn_r, sn_r)
        qh = jnp.concatenate([p1, p2, xh[2 * hn:]], axis=0) * nsa_scale
        qnt_ref[0, h * HEAD_DIM:(h + 1) * HEAD_DIM, :] = qh.astype(qnt_ref.dtype)

    ks = _nsa_rope(u[:, _U_KSLC:_U_KSLC + LANES], cn_t, sn_t)
    ks_r = pltpu.roll(ks, HEAD_DIM, axis=1)
    ksa_ref[0, :, 0:LANES] = jnp.where(low, ks, eneg_t).astype(ksa_ref.dtype)
    ksa_ref[0, :, LANES:2 * LANES] = jnp.where(low, ks_r, eneg_t).astype(ksa_ref.dtype)

    kw = _nsa_rope(u[:, _U_KWIN:_U_KWIN + LANES], cn_t, sn_t)
    kw_r = pltpu.roll(kw, HEAD_DIM, axis=1)
    kwp_ref[0, :, 0:LANES] = jnp.where(low, kw, 0.0).astype(kwp_ref.dtype)
    kwp_ref[0, :, LANES:2 * LANES] = jnp.where(low, kw_r, 0.0).astype(kwp_ref.dtype)

    put_values(vst_ref, ut[_UT_VSLC:_UT_VSLC + NSA_KV_WIDTH])
    put_values(vwt_ref, ut[_UT_VWIN:_UT_VWIN + NSA_KV_WIDTH])
    gatet_ref[0] = ut[_UT_GATE:_UT_GATE + GATE_ROWS]
    kcv_ref[0, 0] = u[:, _U_KCV:_U_KCV + LANES]
    kcv_ref[0, 1] = u[:, _U_KCV + LANES:_U_KCV + 2 * LANES]


def _proj_call(x, tabs, tabt, gmix, win, wint, gcq, wqt, gckv, wk, wvt):
    B, S, D = x.shape
    rows = PROJ_ROWS
    ns = S // rows
    tok = lambda w: pl.BlockSpec((1, rows, w), lambda i, b: (b, i, 0))
    feat = lambda r: pl.BlockSpec((1, r, rows), lambda i, b: (b, 0, i))
    full = lambda a: pl.BlockSpec(a.shape, lambda i, b: (0,) * a.ndim)
    bf = MXU_DTYPE
    outs = [
        (jax.ShapeDtypeStruct((B, MLA_HEADS * LANES, S), bf), feat(MLA_HEADS * LANES)),
        (jax.ShapeDtypeStruct((B, S, MLA_HEADS * LANES), bf), tok(MLA_HEADS * LANES)),
        (jax.ShapeDtypeStruct((B, MLA_HEADS * V_ROWS, S), bf), feat(MLA_HEADS * V_ROWS)),
        (jax.ShapeDtypeStruct((B, NSA_WIDTH, S), bf), feat(NSA_WIDTH)),
        (jax.ShapeDtypeStruct((B, S, 2 * LANES), bf), tok(2 * LANES)),
        (jax.ShapeDtypeStruct((B, NSA_KV_GROUPS * V_ROWS, S), bf), feat(NSA_KV_GROUPS * V_ROWS)),
        (jax.ShapeDtypeStruct((B, S, 2 * LANES), bf), tok(2 * LANES)),
        (jax.ShapeDtypeStruct((B, NSA_KV_GROUPS * V_ROWS, S), bf), feat(NSA_KV_GROUPS * V_ROWS)),
        (jax.ShapeDtypeStruct((B, 2, S, LANES), jnp.float32),
         pl.BlockSpec((1, 2, rows, LANES), lambda i, b: (b, 0, i, 0))),
        (jax.ShapeDtypeStruct((B, GATE_ROWS, S), jnp.float32), feat(GATE_ROWS)),
    ]
    return pl.pallas_call(
        _proj_kernel,
        grid=(ns, B),
        in_specs=[tok(D), pl.BlockSpec((rows, _N_TABS * LANES), lambda i, b: (i, 0)),
                  pl.BlockSpec((_TT_ROWS, rows), lambda i, b: (0, i)),
                  full(gmix), full(win), full(wint), full(gcq), full(wqt), full(gckv), full(wk), full(wvt)],
        out_specs=[o[1] for o in outs],
        out_shape=[o[0] for o in outs],
        compiler_params=pltpu.CompilerParams(
            dimension_semantics=("arbitrary", "arbitrary"), vmem_limit_bytes=VMEM_LIMIT),
        name="proj",
    )(x, tabs, tabt, gmix, win, wint, gcq, wqt, gckv, wk, wvt)


def _gelu_tanh(x):
    return 0.5 * x * (1.0 + jnp.tanh(math.sqrt(2.0 / math.pi) * (x + 0.044715 * (x * x * x))))


def _compress_kernel(x_ref, pea_ref, peb_ref, wa_ref, wb_ref, b1_ref, w2k_ref, b2k_ref, w2vt_ref, b2vt_ref,
                     ctab_ref, kc_ref, vct_ref):
    nch = kc_ref.shape[1]
    a = b = None
    for t in range(CMP_STRIDE):
        for c in range(2):
            cols = slice((2 * t + c) * LANES, (2 * t + c + 1) * LANES)
            xt = x_ref[0, c, pl.ds(t, nch, stride=CMP_STRIDE), :]
            at = _dot((xt + pea_ref[:, cols]).astype(MXU_DTYPE), wa_ref[cols, :])
            bt = _dot((xt + peb_ref[:, cols]).astype(MXU_DTYPE), wb_ref[cols, :])
            a = at if a is None else a + at
            b = bt if b is None else b + bt
    hid = _gelu_tanh(a + pltpu.roll(b, nch - 1, axis=0) + b1_ref[...]).astype(MXU_DTYPE)
    cc = ctab_ref[:, 0:LANES]
    sc = ctab_ref[:, LANES:2 * LANES]
    for g in range(NSA_KV_GROUPS):
        hk = hid[:, g * LANES:(g + 1) * LANES]
        kc = _dot(hk, w2k_ref[...]) + b2k_ref[...]
        kc_ref[0, :, g * LANES:(g + 1) * LANES] = _nsa_rope(kc, cc, sc).astype(kc_ref.dtype)
        hv = hid[:, (NSA_KV_GROUPS + g) * LANES:(NSA_KV_GROUPS + g + 1) * LANES]
        vct_ref[0, g * HEAD_DIM:(g + 1) * HEAD_DIM, :] = (
            _dot_nt(w2vt_ref[...], hv) + b2vt_ref[...]).astype(vct_ref.dtype)


def _compress_call(kcv, consts, ctab):
    B, _, S, _ = kcv.shape
    nch = S // CMP_STRIDE
    full = lambda a: pl.BlockSpec(a.shape, lambda b: (0,) * a.ndim)
    return pl.pallas_call(
        _compress_kernel,
        grid=(B,),
        in_specs=[pl.BlockSpec((1, 2, S, LANES), lambda b: (b, 0, 0, 0))] + [full(c) for c in consts] + [full(ctab)],
        out_specs=[pl.BlockSpec((1, nch, 2 * LANES), lambda b: (b, 0, 0)),
                   pl.BlockSpec((1, NSA_KV_WIDTH, nch), lambda b: (b, 0, 0))],
        out_shape=[jax.ShapeDtypeStruct((B, nch, 2 * LANES), MXU_DTYPE),
                   jax.ShapeDtypeStruct((B, NSA_KV_WIDTH, nch), MXU_DTYPE)],
        compiler_params=pltpu.CompilerParams(
            dimension_semantics=("arbitrary",), vmem_limit_bytes=VMEM_LIMIT),
        name="compress",
    )(kcv, *consts, ctab)


def _numerators(s, m):
    return jnp.exp2((s - m).astype(EXP_DTYPE)).astype(MXU_DTYPE)


def _normalise(acc):
    return acc[0:HEAD_DIM] * (1.0 / acc[HEAD_DIM:HEAD_DIM + 1])


def _own_scores(qt, k_tile):
    n = k_tile.shape[0]
    kt = k_tile.astype(jnp.float32).T
    return jnp.concatenate(
        [jnp.sum(qt[:, h * n:(h + 1) * n].astype(jnp.float32) * kt, axis=0, keepdims=True)
         for h in range(qt.shape[1] // n)], axis=1)


def _fast_start(shifts):
    return tuple((jnp.zeros((V_ROWS, m.shape[1]), jnp.float32), jnp.full(m.shape, -jnp.inf, jnp.float32))
                 for m in shifts)


def _fast_tile(carry, shifts, scores, pv, start, width, mask=None):
    out = []
    for st, (acc, top) in enumerate(carry):
        s = scores(st, start, width)
        if mask is not None:
            s = jnp.where(mask, s, -jnp.inf)
        out.append((acc + pv(st, _numerators(s, shifts[st]), start, width),
                    jnp.maximum(top, jnp.max(s, axis=0, keepdims=True))))
    return tuple(out)


def _fast_excess(carry, shifts):
    gaps = [jnp.max(top - m) for (acc, top), m in zip(carry, shifts)]
    excess = gaps[0]
    for g in gaps[1:]:
        excess = jnp.maximum(excess, g)
    return excess


def _tile_starts(i, tq, wide):
    return wide * tq, i // wide


def _fast_streams(shifts, scores, pv, lower, i, tq, wide):
    carry = _fast_tile(_fast_start(shifts), shifts, scores, pv, pl.multiple_of(i * tq, tq), tq, lower)
    wt, n_wide = _tile_starts(i, tq, wide)
    carry = lax.fori_loop(
        0, n_wide, lambda jj, c: _fast_tile(c, shifts, scores, pv, pl.multiple_of(jj * wt, wt), wt), carry)
    return lax.fori_loop(
        n_wide * wide, i, lambda jj, c: _fast_tile(c, shifts, scores, pv, pl.multiple_of(jj * tq, tq), tq), carry)


def _exact_streams(n_streams, scores, pv, lower, i, tq, wide):
    start_d = pl.multiple_of(i * tq, tq)
    carry = []
    for st in range(n_streams):
        s = jnp.where(lower, scores(st, start_d, tq), -jnp.inf)
        m = jnp.max(s, axis=0, keepdims=True)
        carry.append((m, pv(st, _numerators(s, m), start_d, tq)))

    def step(start, width, carry):
        out = []
        for st, (m, acc) in enumerate(carry):
            s = scores(st, start, width)
            m_new = jnp.maximum(m, jnp.max(s, axis=0, keepdims=True))
            out.append((m_new, jnp.exp2(m - m_new) * acc + pv(st, _numerators(s, m_new), start, width)))
        return tuple(out)

    wt, n_wide = _tile_starts(i, tq, wide)
    carry = lax.fori_loop(0, n_wide, lambda jj, c: step(pl.multiple_of(jj * wt, wt), wt, c), tuple(carry))
    carry = lax.fori_loop(n_wide * wide, i, lambda jj, c: step(pl.multiple_of(jj * tq, tq), tq, c), carry)
    return tuple(_normalise(acc) for (m, acc) in carry)


def _mla_kernel(qt_ref, k_ref, vt_ref, o_ref):
    i = pl.program_id(2)
    tq = qt_ref.shape[2]
    lower = lax.broadcasted_iota(jnp.int32, (tq, tq), 0) <= lax.broadcasted_iota(jnp.int32, (tq, tq), 1)
    qs = [qt_ref[0, hh * LANES:(hh + 1) * LANES, :] for hh in range(MLA_HPS)]

    def scores(hh, start, width):
        return _dot(k_ref[0, pl.ds(start, width), hh * LANES:(hh + 1) * LANES], qs[hh])

    def pv(hh, p, start, width):
        return _dot(vt_ref[0, hh * V_ROWS:(hh + 1) * V_ROWS, pl.ds(start, width)], p)

    own_rows = pl.ds(pl.multiple_of(i * tq, tq), tq)
    shifts = [_own_scores(qs[hh], k_ref[0, own_rows, hh * LANES:(hh + 1) * LANES]) for hh in range(MLA_HPS)]
    fast = _fast_streams(shifts, scores, pv, lower, i, tq, MLA_WIDE)
    outs = lax.cond(_fast_excess(fast, shifts) > LAZY_MAX_SLACK,
                    lambda _: _exact_streams(MLA_HPS, scores, pv, lower, i, tq, MLA_WIDE),
                    lambda _: tuple(_normalise(acc) for acc, top in fast), None)
    for hh, o in enumerate(outs):
        o_ref[0, hh * MLA_V_DIM:(hh + 1) * MLA_V_DIM, :] = o.astype(o_ref.dtype)


def _mla_call(qmt, km, vmt):
    B, S, _ = km.shape
    tq = MLA_TQ
    hps = MLA_HPS
    return pl.pallas_call(
        _mla_kernel,
        grid=(B, MLA_HEADS // hps, S // tq),
        in_specs=[pl.BlockSpec((1, hps * LANES, tq), lambda b, p, i: (b, p, i)),
                  pl.BlockSpec((1, S, hps * LANES), lambda b, p, i: (b, 0, p)),
                  pl.BlockSpec((1, hps * V_ROWS, S), lambda b, p, i: (b, p, 0))],
        out_specs=pl.BlockSpec((1, hps * MLA_V_DIM, tq), lambda b, p, i: (b, p, i)),
        out_shape=jax.ShapeDtypeStruct((B, MLA_WIDTH, S), Y_DTYPE),
        compiler_params=pltpu.CompilerParams(
            dimension_semantics=("arbitrary", "arbitrary", "arbitrary"), vmem_limit_bytes=VMEM_LIMIT),
        name="mla",
    )(qmt, km, vmt)


def _group_queries(q_ref, g, rows):
    return jnp.concatenate(
        [q_ref[0, (g * NSA_HPG + h) * rows:(g * NSA_HPG + h + 1) * rows, :] for h in range(NSA_HPG)], axis=1)


def _nsa_cmp_kernel(qt_ref, kc_ref, vct_ref, ovt_ref, oct_ref, qaugt_ref, score_ref):
    i = pl.program_id(1)
    tq = qt_ref.shape[2]
    nch = kc_ref.shape[1]
    nblk = score_ref.shape[1]
    cols4 = NSA_HPG * tq
    q0 = i * tq
    qpos_c = q0 + lax.broadcasted_iota(jnp.int32, (nch, cols4), 1) % tq
    cmp_end = lax.broadcasted_iota(jnp.int32, (nch, cols4), 0) * CMP_STRIDE + (CMP_BLOCK - 1)
    mask_c = cmp_end <= qpos_c

    blk = lax.broadcasted_iota(jnp.int32, (nblk, tq), 0)
    qpos_r = q0 + lax.broadcasted_iota(jnp.int32, (nblk, tq), 1)
    cur = qpos_r // SLC_BLOCK
    forced = (blk == 0) | (blk == cur) | (blk == cur - 1)
    causal = blk * SLC_BLOCK <= qpos_r
    groups = range(NSA_KV_GROUPS)

    scores = []
    for g in groups:
        q4 = _group_queries(qt_ref, g, HEAD_DIM)
        qpad = jnp.concatenate([q4, jnp.zeros_like(q4)], axis=0)
        s = jnp.where(mask_c, _dot(kc_ref[0, :, g * LANES:(g + 1) * LANES], qpad), -jnp.inf)
        m = jnp.max(s, axis=0, keepdims=True)
        m = jnp.where(m == -jnp.inf, 0.0, m)
        e = jnp.exp2(s - m)
        pr = e / jnp.maximum(jnp.sum(e, axis=0, keepdims=True), 1e-30)
        o = _dot(vct_ref[0, g * HEAD_DIM:(g + 1) * HEAD_DIM, :], pr.astype(MXU_DTYPE))
        for h in range(NSA_HPG):
            head = g * NSA_HPG + h
            oct_ref[0, head * HEAD_DIM:(head + 1) * HEAD_DIM, :] = o[:, h * tq:(h + 1) * tq]

        psum = (pr[:, 0:tq] + pr[:, tq:2 * tq]) + (pr[:, 2 * tq:3 * tq] + pr[:, 3 * tq:4 * tq])
        p_hi = psum.astype(MXU_DTYPE)
        p_lo = (psum - p_hi.astype(jnp.float32)).astype(MXU_DTYPE)
        imp = _dot(ovt_ref[...], p_hi) + _dot(ovt_ref[...], p_lo)
        score = jnp.where(forced, FORCE_SCORE, jnp.where(causal, imp, -FORCE_SCORE))
        score_ref[g] = score
        scores.append(score)

    def count(c, cnts):
        out = list(cnts)
        for t in range(RANK_UNROLL):
            r = c * RANK_UNROLL + t
            tie = jnp.where(r < blk, 1.0, 0.0)
            for g in groups:
                row = score_ref[g, pl.ds(r, 1), :]
                out[g] = out[g] + jnp.where(row > scores[g], 1.0, jnp.where(row == scores[g], tie, 0.0))
        return tuple(out)

    zero = jnp.zeros((nblk, tq), jnp.float32)
    cnts = lax.fori_loop(0, i + 1, count, (zero,) * NSA_KV_GROUPS)
    for g in groups:
        notsel = jnp.where(cnts[g] < float(SLC_TOPK), 0.0, 1.0).astype(qaugt_ref.dtype)
        for h in range(NSA_HPG):
            head = g * NSA_HPG + h
            qaugt_ref[0, head * LANES:head * LANES + HEAD_DIM, :] = qt_ref[0, head * HEAD_DIM:(head + 1) * HEAD_DIM, :]
            qaugt_ref[0, head * LANES + HEAD_DIM:(head + 1) * LANES, :] = notsel


def _nsa_cmp_call(qnt, kc, vct, ovt):
    B, _, S = qnt.shape
    tq = NSA_TQ
    nch = kc.shape[1]
    return pl.pallas_call(
        _nsa_cmp_kernel,
        grid=(B, S // tq),
        in_specs=[pl.BlockSpec((1, NSA_WIDTH, tq), lambda b, i: (b, 0, i)),
                  pl.BlockSpec((1, nch, 2 * LANES), lambda b, i: (b, 0, 0)),
                  pl.BlockSpec((1, NSA_KV_WIDTH, nch), lambda b, i: (b, 0, 0)),
                  pl.BlockSpec(ovt.shape, lambda b, i: (0, 0))],
        out_specs=[pl.BlockSpec((1, NSA_WIDTH, tq), lambda b, i: (b, 0, i)),
                   pl.BlockSpec((1, NSA_HEADS * LANES, tq), lambda b, i: (b, 0, i))],
        out_shape=[jax.ShapeDtypeStruct((B, NSA_WIDTH, S), jnp.float32),
                   jax.ShapeDtypeStruct((B, NSA_HEADS * LANES, S), MXU_DTYPE)],
        scratch_shapes=[pltpu.VMEM((NSA_KV_GROUPS, LANES - HEAD_DIM, tq), jnp.float32)],
        compiler_params=pltpu.CompilerParams(
            dimension_semantics=("arbitrary", "arbitrary"), vmem_limit_bytes=VMEM_LIMIT),
        name="nsa_cmp",
    )(qnt, kc, vct, ovt)


def _nsa_attn_kernel(qaugt_ref, ksa_ref, vst_ref, kwp_ref, vwt_ref, oct_ref, gatet_ref, y_ref):
    i = pl.program_id(1)
    tq = qaugt_ref.shape[2]
    cols4 = NSA_HPG * tq
    groups = range(NSA_KV_GROUPS)
    lower = (lax.broadcasted_iota(jnp.int32, (tq, cols4), 0)
             <= lax.broadcasted_iota(jnp.int32, (tq, cols4), 1) % tq)
    qs = [_group_queries(qaugt_ref, g, LANES) for g in groups]

    def pv_from(vt_ref):
        def pv(g, p, start, width):
            return _dot(vt_ref[0, g * V_ROWS:(g + 1) * V_ROWS, pl.ds(start, width)], p)
        return pv

    pv_s = pv_from(vst_ref)
    pv_w = pv_from(vwt_ref)

    def s_scores(g, start, width):
        return _dot(ksa_ref[0, pl.ds(start, width), g * LANES:(g + 1) * LANES], qs[g])

    sub = WIN_SUB
    wk = WINDOW + sub
    nsub = tq // sub
    rel = (lax.broadcasted_iota(jnp.int32, (wk, NSA_HPG * sub), 1) % sub
           - lax.broadcasted_iota(jnp.int32, (wk, NSA_HPG * sub), 0))

    def window_tile(j):
        q_lo = i * tq + j * sub
        wstart = pl.multiple_of(jnp.maximum(q_lo - WINDOW, 0), sub)
        dist = rel + (q_lo - wstart)
        q_subs = [jnp.concatenate(
            [qs[g][:, h * tq + j * sub:h * tq + (j + 1) * sub] for h in range(NSA_HPG)], axis=1) for g in groups]

        def w_scores(g, start, width):
            return _dot(kwp_ref[0, pl.ds(start, width), g * LANES:(g + 1) * LANES], q_subs[g])

        return q_lo, wstart, (dist >= 0) & (dist < WINDOW), q_subs, w_scores

    own_rows = pl.ds(pl.multiple_of(i * tq, tq), tq)
    s_shifts = [_own_scores(qs[g], ksa_ref[0, own_rows, g * LANES:(g + 1) * LANES]) for g in groups]
    fast_s = _fast_streams(s_shifts, s_scores, pv_s, lower, i, tq, NSA_WIDE)
    excess = _fast_excess(fast_s, s_shifts)
    fast_w = []
    for j in range(nsub):
        q_lo, wstart, band, q_subs, w_scores = window_tile(j)
        sub_rows = pl.ds(pl.multiple_of(q_lo, sub), sub)
        w_shifts = [_own_scores(q_subs[g], kwp_ref[0, sub_rows, g * LANES:(g + 1) * LANES]) for g in groups]
        carry = _fast_tile(_fast_start(w_shifts), w_shifts, w_scores, pv_w, wstart, wk, band)
        excess = jnp.maximum(excess, _fast_excess(carry, w_shifts))
        fast_w.append(tuple(_normalise(acc) for acc, top in carry))

    def exact(_):
        o_s = _exact_streams(NSA_KV_GROUPS, s_scores, pv_s, lower, i, tq, NSA_WIDE)
        o_w = []
        for j in range(nsub):
            q_lo, wstart, band, q_subs, w_scores = window_tile(j)
            outs = []
            for g in groups:
                s = jnp.where(band, w_scores(g, wstart, wk), -jnp.inf)
                p = _numerators(s, jnp.max(s, axis=0, keepdims=True))
                outs.append(_normalise(pv_w(g, p, wstart, wk)))
            o_w.append(tuple(outs))
        return o_s, tuple(o_w)

    o_slc, o_wins = lax.cond(
        excess > LAZY_MAX_SLACK, exact,
        lambda _: (tuple(_normalise(acc) for acc, top in fast_s), tuple(fast_w)), None)
    o_win = [[o_wins[j][g] for j in range(nsub)] for g in groups]

    sig = jax.nn.sigmoid(gatet_ref[0])
    for g in groups:
        for h in range(NSA_HPG):
            head = g * NSA_HPG + h
            o_w = jnp.concatenate([o_win[g][j][:, h * sub:(h + 1) * sub] for j in range(nsub)], axis=1)
            gate = lambda br: sig[br * NSA_HEADS + head:br * NSA_HEADS + head + 1, :]
            y_ref[0, head * HEAD_DIM:(head + 1) * HEAD_DIM, :] = (
                gate(0) * oct_ref[0, head * HEAD_DIM:(head + 1) * HEAD_DIM, :]
                + gate(1) * o_slc[g][:, h * tq:(h + 1) * tq] + gate(2) * o_w).astype(y_ref.dtype)


def _nsa_attn_call(qaugt, ksa, vst, kwp, vwt, oct, gatet):
    B, S, _ = ksa.shape
    tq = NSA_TQ
    feat = lambda r: pl.BlockSpec((1, r, tq), lambda b, i: (b, 0, i))
    seq_tok = lambda w: pl.BlockSpec((1, S, w), lambda b, i: (b, 0, 0))
    seq_feat = lambda r: pl.BlockSpec((1, r, S), lambda b, i: (b, 0, 0))
    return pl.pallas_call(
        _nsa_attn_kernel,
        grid=(B, S // tq),
        in_specs=[feat(NSA_HEADS * LANES), seq_tok(2 * LANES), seq_feat(NSA_KV_GROUPS * V_ROWS), seq_tok(2 * LANES),
                  seq_feat(NSA_KV_GROUPS * V_ROWS), feat(NSA_WIDTH), feat(GATE_ROWS)],
        out_specs=feat(NSA_WIDTH),
        out_shape=jax.ShapeDtypeStruct((B, NSA_WIDTH, S), Y_DTYPE),
        compiler_params=pltpu.CompilerParams(
            dimension_semantics=("arbitrary", "arbitrary"), vmem_limit_bytes=VMEM_LIMIT),
        name="nsa_attn",
    )(qaugt, ksa, vst, kwp, vwt, oct, gatet)


def _rms_cols(yt, g_col):
    return yt * lax.rsqrt(jnp.mean(yt * yt, axis=0, keepdims=True) + NORM_EPS) * g_col


def _out_kernel(x_ref, ymt_ref, ynt_ref, gm_ref, gn_ref, wo_ref, gmlp_ref, wup_ref, wdn_ref, gfin_ref, o_ref):
    mixed_t = jnp.concatenate([_rms_cols(ymt_ref[0].astype(jnp.float32), gm_ref[...]),
                               _rms_cols(ynt_ref[0].astype(jnp.float32), gn_ref[...])], axis=0)
    h = x_ref[0] + _dot_tn(mixed_t.astype(MXU_DTYPE), wo_ref[...])
    hn = _rms(h, gmlp_ref[...]).astype(MXU_DTYPE)
    acc = h
    for c in range(D_FF // FF_CHUNK):
        a = jnp.maximum(_dot(hn, wup_ref[:, c * FF_CHUNK:(c + 1) * FF_CHUNK]), 0.0)
        acc = acc + _dot((a * a).astype(MXU_DTYPE), wdn_ref[c * FF_CHUNK:(c + 1) * FF_CHUNK, :])
    o_ref[0] = _rms(acc, gfin_ref[...])


def _out_call(x, ymt, ynt, gm, gn, wo, gmlp, wup, wdn, gfin):
    B, S, D = x.shape
    rows = OUT_ROWS
    tok = pl.BlockSpec((1, rows, D), lambda b, i: (b, i, 0))
    feat = lambda r: pl.BlockSpec((1, r, rows), lambda b, i: (b, 0, i))
    full = lambda a: pl.BlockSpec(a.shape, lambda b, i: (0,) * a.ndim)
    return pl.pallas_call(
        _out_kernel,
        grid=(B, S // rows),
        in_specs=[tok, feat(MLA_WIDTH), feat(NSA_WIDTH), full(gm), full(gn), full(wo), full(gmlp),
                  full(wup), full(wdn), full(gfin)],
        out_specs=tok,
        out_shape=jax.ShapeDtypeStruct((B, S, D), jnp.float32),
        compiler_params=pltpu.CompilerParams(
            dimension_semantics=("arbitrary", "arbitrary"), vmem_limit_bytes=VMEM_LIMIT),
        name="out",
    )(x, ymt, ynt, gm, gn, wo, gmlp, wup, wdn, gfin)


def _rope_cs(pos, dim):
    inv_freq = jnp.exp(-math.log(ROPE_THETA) * jnp.arange(0, dim, 2, dtype=jnp.float32) / dim)
    ang = pos.astype(jnp.float32)[:, None] * inv_freq[None, :]
    return jnp.cos(ang), jnp.sin(ang)


def _nsa_rope_tables(pos, width):
    c, s = _rope_cs(pos, NSA_ROPE_DIM)
    n = pos.shape[0]
    rest = HEAD_DIM - NSA_ROPE_DIM
    ch = jnp.concatenate([c, c, jnp.ones((n, rest), jnp.float32)], axis=1)
    sh = jnp.concatenate([-s, s, jnp.zeros((n, rest), jnp.float32)], axis=1)
    reps = width // HEAD_DIM
    return jnp.tile(ch, (1, reps)), jnp.tile(sh, (1, reps))


def _proj_tables(S):
    pos = jnp.arange(S)
    c, s = _rope_cs(pos, MLA_ROPE_DIM)
    z = lambda w: jnp.zeros((S, w), jnp.float32)
    pad = LANES - MLA_NOPE_DIM - MLA_ROPE_DIM
    ck = jnp.concatenate([z(MLA_NOPE_DIM), c, c, z(pad)], axis=1)
    sk = jnp.concatenate([z(MLA_NOPE_DIM), -s, s, z(pad)], axis=1)
    cn, sn = _nsa_rope_tables(pos, LANES)
    onehot = (pos[:, None] // SLC_BLOCK == jnp.arange(LANES - HEAD_DIM)[None, :]).astype(jnp.float32)
    eneg = jnp.concatenate([z(HEAD_DIM), MASK_BIAS * onehot], axis=1)
    cn8, sn8 = _rope_cs(pos, NSA_ROPE_DIM)
    tabt = jnp.concatenate([c.T, s.T, cn8.T, sn8.T], axis=0)
    return jnp.concatenate([ck, sk, cn, sn, eneg], axis=1), tabt


def _compress_tables(nch):
    end = jnp.arange(nch) * CMP_STRIDE + CMP_BLOCK - 1
    c, s = _nsa_rope_tables(end, HEAD_DIM)
    z = jnp.zeros((nch, LANES - HEAD_DIM), jnp.float32)
    return jnp.concatenate([c, z, s, z], axis=1)


def _overlap_t(nch, nblk):
    cs = jnp.arange(nch)[None, :] * CMP_STRIDE
    ss = jnp.arange(nblk)[:, None] * SLC_BLOCK
    ov = jnp.clip(jnp.minimum(cs + CMP_BLOCK, ss + SLC_BLOCK) - jnp.maximum(cs, ss), 0, None)
    ov = ov.astype(jnp.float32) / CMP_BLOCK
    ov = jnp.where(jnp.arange(nch)[None, :] < nch - 1, ov, 0.0)
    return ov.astype(MXU_DTYPE)


def _pad_cols(w, width):
    return jnp.pad(w, ((0, 0), (0, width - w.shape[1])))


def _compress_consts(pe_k, w1_k, b1_k, w2_k, b2_k, pe_v, w1_v, b1_v, w2_v, b2_v):
    half = CMP_BLOCK // 2
    ncol = 2 * NSA_KV_GROUPS
    w1 = jnp.stack([w1_k, w1_k, w1_v, w1_v]).reshape(ncol, 2, half, HEAD_DIM, CMP_HIDDEN)
    eye = jnp.eye(ncol, dtype=w1.dtype)

    def spread(wh):
        return jnp.einsum('cldh,ce->lcdeh', wh, eye).reshape(half * ncol * HEAD_DIM, ncol * CMP_HIDDEN)

    pe = jnp.stack([pe_k, pe_k, pe_v, pe_v]).reshape(ncol, 2, half, HEAD_DIM)
    pe_row = lambda ph: ph.transpose(1, 0, 2).reshape(1, half * ncol * HEAD_DIM)
    b1 = jnp.concatenate([b1_k, b1_k, b1_v, b1_v]).reshape(1, ncol * CMP_HIDDEN)
    return [
        pe_row(pe[:, 0]), pe_row(pe[:, 1]),
        spread(w1[:, 0]).astype(MXU_DTYPE), spread(w1[:, 1]).astype(MXU_DTYPE), b1,
        _pad_cols(w2_k, LANES).astype(MXU_DTYPE), _pad_cols(b2_k[None, :], LANES),
        w2_v.T.astype(MXU_DTYPE), b2_v[:, None],
    ]


def _proj_weights(w_in, w_uq, w_ukv):
    d = w_in.shape[0]
    z = lambda w: jnp.zeros((d, w), w_in.dtype)
    o_kr = MLA_Q_RANK + MLA_KV_RANK
    o_qn = o_kr + MLA_ROPE_DIM
    o_kv = o_qn + NSA_WIDTH
    o_gate = o_kv + 6 * NSA_KV_WIDTH
    kv = lambda t: w_in[:, o_kv + t * NSA_KV_WIDTH:o_kv + (t + 1) * NSA_KV_WIDTH]
    win = jnp.concatenate([
        w_in[:, :o_kr],
        z(MLA_NOPE_DIM), w_in[:, o_kr:o_qn], z(LANES - MLA_NOPE_DIM - MLA_ROPE_DIM),
        kv(0), kv(1), kv(2), kv(4)], axis=1)
    assert win.shape[1] == _U_COLS
    wint = jnp.concatenate([w_in[:, o_qn:o_kv], kv(3), kv(5), w_in[:, o_gate:], z(GATE_ROWS - NSA_GATES)], axis=1).T
    assert wint.shape[0] == _UT_ROWS
    qd = MLA_NOPE_DIM + MLA_ROPE_DIM
    wq = jnp.pad(w_uq.reshape(MLA_Q_RANK, MLA_HEADS, qd), ((0, 0), (0, 0), (0, LANES - qd)))
    wqt = wq.reshape(MLA_Q_RANK, MLA_HEADS * LANES).T
    kvw = w_ukv.reshape(MLA_KV_RANK, MLA_HEADS, MLA_NOPE_DIM + MLA_V_DIM)
    zk = jnp.zeros((MLA_KV_RANK, MLA_HEADS, LANES - MLA_NOPE_DIM), w_ukv.dtype)
    wk = jnp.concatenate([kvw[..., :MLA_NOPE_DIM], zk], axis=-1).reshape(MLA_KV_RANK, MLA_HEADS * LANES)
    wvt = kvw[..., MLA_NOPE_DIM:].reshape(MLA_KV_RANK, MLA_WIDTH).T
    return (win.astype(MXU_DTYPE), wint.astype(MXU_DTYPE), wqt.astype(MXU_DTYPE), wk.astype(MXU_DTYPE),
            wvt.astype(MXU_DTYPE))


def kernel(x, g_mix_norm, w_in, g_cq, w_uq, g_ckv, w_ukv, cmp_pe_k, cmp_w1_k, cmp_b1_k, cmp_w2_k, cmp_b2_k,
           cmp_pe_v, cmp_w1_v, cmp_b1_v, cmp_w2_v, cmp_b2_v, g_out_mla, g_out_nsa, w_o, g_mlp_norm, w_up,
           w_down, g_final):
    B, S, D = x.shape
    nch = S // CMP_STRIDE
    nblk = S // SLC_BLOCK
    assert w_in.shape[0] == 1
    assert D == D_MODEL and S % MLA_TQ == 0 and S % PROJ_ROWS == 0 and S % OUT_ROWS == 0
    assert S % NSA_TQ == 0 and NSA_TQ % WIN_SUB == 0 and WINDOW % WIN_SUB == 0 and S >= WINDOW + WIN_SUB
    assert SLC_TOPK <= nblk <= LANES - HEAD_DIM
    row = lambda g: g.reshape(1, -1)
    col = lambda g: g.reshape(-1, 1)

    tabs, tabt = _proj_tables(S)
    ctab = _compress_tables(nch)
    ovt = _overlap_t(nch, LANES - HEAD_DIM)

    win, wint, wqt, wk, wvt = _proj_weights(w_in[0], w_uq[0], w_ukv[0])
    qmt, km, vmt, qnt, ksa, vst, kwp, vwt, kcv, gatet = _proj_call(
        x, tabs, tabt, row(g_mix_norm[0]), win, wint, row(g_cq[0]), wqt, row(g_ckv[0]), wk, wvt)
    consts = _compress_consts(cmp_pe_k[0], cmp_w1_k[0], cmp_b1_k[0], cmp_w2_k[0], cmp_b2_k[0],
                              cmp_pe_v[0], cmp_w1_v[0], cmp_b1_v[0], cmp_w2_v[0], cmp_b2_v[0])
    kc, vct = _compress_call(kcv, consts, ctab)
    y_mla_t = _mla_call(qmt, km, vmt)
    oct, qaugt = _nsa_cmp_call(qnt, kc, vct, ovt)
    y_nsa_t = _nsa_attn_call(qaugt, ksa, vst, kwp, vwt, oct, gatet)
    return _out_call(x, y_mla_t, y_nsa_t, col(g_out_mla[0]), col(g_out_nsa[0]), w_o[0].astype(MXU_DTYPE),
                     row(g_mlp_norm[0]), w_up[0].astype(MXU_DTYPE), w_down[0].astype(MXU_DTYPE), row(g_final))
```
